```python
import math
import jax, jax.numpy as jnp
from jax import lax
import numpy as np

D_MODEL = 2048
BATCH = 2
SEQ = 4096
DEPTH = 1
DEC_BATCH = 4
DEC_SEQ = 4096
PAST_LEN = 128

RWKV_HEADS = 16
RWKV_HEAD_DIM = 64
RWKV_DIM = RWKV_HEADS * RWKV_HEAD_DIM
DECAY_LORA = 64
ICLR_LORA = 64
GATE_LORA = 128
S5_GROUPS = 32
S5_GROUP_CH = 16
S5_DIM = S5_GROUPS * S5_GROUP_CH
S5_STATE = 64
DT_MIN = 1e-3
DT_MAX = 1e-1
N_BRANCH = 2
D_FF = 5504
RMS_EPS = 1e-6
GN_EPS = 64e-5
R_OFF = 0
K_OFF = RWKV_DIM
V_OFF = 2 * RWKV_DIM
WLOW_OFF = 3 * RWKV_DIM
ALOW_OFF = WLOW_OFF + 2 * DECAY_LORA
GLOW_OFF = ALOW_OFF + 2 * ICLR_LORA
SHIFT_COLS = GLOW_OFF + GATE_LORA
U_OFF = SHIFT_COLS
GATE_OFF = U_OFF + S5_DIM
IN_COLS = GATE_OFF + N_BRANCH * D_MODEL

kernel_name = 'hybrid_rwkv7_s5_macaron_encoder'


def _rmsnorm(x, g):
    xf = x.astype(jnp.float32)
    y = xf * lax.rsqrt(jnp.mean(xf * xf, -1, keepdims=True) + RMS_EPS)
    return (y * g.astype(jnp.float32)).astype(x.dtype)


def _swiglu(x, w_gate, w_up, w_down):
    return (jax.nn.silu(x @ w_gate) * (x @ w_up)) @ w_down


def _centred_shift(p, mu):
    prev = jnp.pad(p[:, :-1], ((0, 0), (1, 0), (0, 0)))
    nxt = jnp.pad(p[:, 1:], ((0, 0), (0, 1), (0, 0)))
    return p + (0.5 * (prev + nxt) - p) * mu


def _rwkv_step(S, inp):
    r, w, k, v, kk, b = inp
    sa = jnp.einsum('bhvk,bhk->bhv', S, -kk)
    S = S * w[:, :, None, :] + sa[..., None] * b[:, :, None, :] + v[..., None] * k[:, :, None, :]
    y = jnp.einsum('bhvk,bhk->bhv', S, r)
    return S, y


def _rwkv_mixer(p, w0, w2, a0, a2, g2, k_k, k_a, r_k, ln_w, ln_b):
    Bsz, L = p.shape[0], p.shape[1]
    H, N = RWKV_HEADS, RWKV_HEAD_DIM
    pf = p.astype(jnp.float32)
    r = pf[..., R_OFF:K_OFF].reshape(Bsz, L, H, N)
    k = pf[..., K_OFF:V_OFF].reshape(Bsz, L, H, N)
    v = pf[..., V_OFF:WLOW_OFF].reshape(Bsz, L, H, N)
    wlow = jnp.tanh(pf[..., WLOW_OFF:ALOW_OFF]).reshape(Bsz, L, 2, DECAY_LORA)
    alow = pf[..., ALOW_OFF:GLOW_OFF].reshape(Bsz, L, 2, ICLR_LORA)
    glow = pf[..., GLOW_OFF:SHIFT_COLS]
    wpre = jnp.einsum('bldr,drc->dblc', wlow, w2) + w0[:, None, None, :]
    decay = jnp.exp(-jnp.exp(-jax.nn.softplus(-wpre) - 0.5)).reshape(2, Bsz, L, H, N)
    a = jax.nn.sigmoid(jnp.einsum('bldr,drc->dblc', alow, a2) + a0[:, None, None, :]).reshape(2, Bsz, L, H, N)
    kk = k * k_k.reshape(H, N)
    kk = kk * lax.rsqrt(jnp.sum(kk * kk, -1, keepdims=True) + 1e-12)
    k_dir = k[None] * (1.0 + (a - 1.0) * k_a.reshape(H, N))
    b_dir = kk[None] * a

    def both(t):
        return jnp.swapaxes(jnp.concatenate([t, jnp.flip(t, 1)], 0), 0, 1)

    def per_dir(t):
        return jnp.swapaxes(jnp.concatenate([t[0], jnp.flip(t[1], 1)], 0), 0, 1)

    xs = (both(r), per_dir(decay), per_dir(k_dir), both(v), both(kk), per_dir(b_dir))
    s0 = jnp.zeros((2 * Bsz, H, N, N), jnp.float32)
    _, ys = lax.scan(_rwkv_step, s0, xs)
    ys = jnp.swapaxes(ys, 0, 1)
    y = ys[:Bsz] + jnp.flip(ys[Bsz:], 1)
    mean = jnp.mean(y, -1, keepdims=True)
    var = jnp.mean(jnp.square(y - mean), -1, keepdims=True)
    y = (y - mean) * lax.rsqrt(var + GN_EPS) * ln_w.reshape(H, N) + ln_b.reshape(H, N)
    bonus = jnp.sum(r * (k_dir[0] + k_dir[1]) * r_k.reshape(H, N), -1, keepdims=True) * v
    y = (y + bonus).reshape(Bsz, L, RWKV_DIM)
    g = jax.nn.sigmoid(glow) @ g2
    return y * g


def _s5_combine(e1, e2):
    a1, b1 = e1
    a2, b2 = e2
    return a2 * a1, a2 * b1 + b2


def _s5_direction(ug, a_re, a_im, log_step, b_re, b_im):
    lam = lax.complex(a_re.astype(jnp.float32), a_im.astype(jnp.float32))
    dt = jnp.exp(log_step.astype(jnp.float32))[:, None]
    lam_bar = jnp.exp(lam * dt)
    b_bar = ((lam_bar - 1.0) / lam)[..., None] * lax.complex(b_re.astype(jnp.float32), b_im.astype(jnp.float32))
    bu = lax.complex(jnp.einsum('blgc,gpc->blgp', ug, jnp.real(b_bar)),
                     jnp.einsum('blgc,gpc->blgp', ug, jnp.imag(b_bar)))
    lam_seq = jnp.broadcast_to(lam_bar, bu.shape)
    _, states = lax.associative_scan(_s5_combine, (lam_seq, bu), axis=1)
    return states


def _s5_mixer(u, a_re, a_im, log_step, b_re, b_im, c_re, c_im, d_skip, w_glu, b_glu):
    Bsz, L = u.shape[0], u.shape[1]
    uf = u.astype(jnp.float32)
    ug = uf.reshape(Bsz, L, S5_GROUPS, S5_GROUP_CH)
    s_fwd = _s5_direction(ug, a_re[0], a_im[0], log_step[0], b_re[0], b_im[0])
    s_bwd = jnp.flip(_s5_direction(jnp.flip(ug, 1), a_re[1], a_im[1], log_step[1], b_re[1], b_im[1]), 1)
    st = s_fwd + s_bwd
    y = (jnp.einsum('blgp,gcp->blgc', jnp.real(st), c_re)
         - jnp.einsum('blgp,gcp->blgc', jnp.imag(st), c_im))
    y = y.reshape(Bsz, L, S5_DIM) + d_skip * uf
    y = jax.nn.gelu(y)
    return y * jax.nn.sigmoid(y @ w_glu + b_glu)


def _layer(x, norm_ffn1, ffn1_w_gate, ffn1_w_up, ffn1_w_down, norm_mix, w_in, shift_mu,
           rwkv_w0, rwkv_w2, rwkv_a0, rwkv_a2, rwkv_g2, rwkv_k_k, rwkv_k_a, rwkv_r_k,
           rwkv_ln_w, rwkv_ln_b, s5_a_re, s5_a_im, s5_log_step, s5_b_re, s5_b_im,
           s5_c_re, s5_c_im, s5_d, s5_w_glu, s5_b_glu, proj_rwkv, proj_s5, w_out,
           norm_ffn2, ffn2_w_gate, ffn2_w_up, ffn2_w_down):
    x = x + 0.5 * _swiglu(_rmsnorm(x, norm_ffn1), ffn1_w_gate, ffn1_w_up, ffn1_w_down)
    h = _rmsnorm(x, norm_mix)
    proj = h @ w_in
    p_rwkv = _centred_shift(proj[..., :SHIFT_COLS], shift_mu)
    y_rwkv = _rwkv_mixer(p_rwkv, rwkv_w0, rwkv_w2, rwkv_a0, rwkv_a2, rwkv_g2, rwkv_k_k,
                         rwkv_k_a, rwkv_r_k, rwkv_ln_w, rwkv_ln_b).astype(x.dtype) @ proj_rwkv
    y_s5 = _s5_mixer(proj[..., U_OFF:GATE_OFF], s5_a_re, s5_a_im, s5_log_step, s5_b_re, s5_b_im,
                     s5_c_re, s5_c_im, s5_d, s5_w_glu, s5_b_glu).astype(x.dtype) @ proj_s5
    gates = jax.nn.sigmoid(proj[..., GATE_OFF:]).reshape(x.shape[0], x.shape[1], N_BRANCH, D_MODEL)
    merged = gates[..., 0, :] * y_rwkv + gates[..., 1, :] * y_s5
    x = x + merged @ w_out
    x = x + 0.5 * _swiglu(_rmsnorm(x, norm_ffn2), ffn2_w_gate, ffn2_w_up, ffn2_w_down)
    return x


def _trunk(x, layer_params, norm_final):
    for l in range(DEPTH):
        x = _layer(x, *[p[l] for p in layer_params])
    return _rmsnorm(x, norm_final)


def setup_inputs(seed: int = 0) -> dict:
    key = jax.random.key(seed)
    ks = iter(jax.random.split(key, 48))
    f32 = jnp.float32

    def nrm(shape, scale):
        return jax.random.normal(next(ks), shape, f32) * scale

    def gain(shape):
        return 1.0 + 0.05 * jax.random.normal(next(ks), shape, f32)

    Lr, P, G, CH = DEPTH, S5_STATE, S5_GROUPS, S5_GROUP_CH
    n_idx = jnp.arange(P, dtype=f32)
    return {
        'x_prompt': nrm((BATCH, SEQ, D_MODEL), 1.0),
        'x_sample': nrm((DEC_BATCH, DEC_SEQ, D_MODEL), 1.0),
        'norm_ffn1': gain((Lr, D_MODEL)),
        'ffn1_w_gate': nrm((Lr, D_MODEL, D_FF), D_MODEL ** -0.5),
        'ffn1_w_up': nrm((Lr, D_MODEL, D_FF), D_MODEL ** -0.5),
        'ffn1_w_down': nrm((Lr, D_FF, D_MODEL), D_FF ** -0.5),
        'norm_mix': gain((Lr, D_MODEL)),
        'w_in': nrm((Lr, D_MODEL, IN_COLS), D_MODEL ** -0.5),
        'shift_mu': jax.random.uniform(next(ks), (Lr, SHIFT_COLS), f32),
        'rwkv_w0': jax.random.uniform(next(ks), (Lr, 2, RWKV_DIM), f32, -6.0, 1.0),
        'rwkv_w2': nrm((Lr, 2, DECAY_LORA, RWKV_DIM), 0.1 * DECAY_LORA ** -0.5),
        'rwkv_a0': nrm((Lr, 2, RWKV_DIM), 0.5),
        'rwkv_a2': nrm((Lr, 2, ICLR_LORA, RWKV_DIM), 0.5 * ICLR_LORA ** -0.5),
        'rwkv_g2': nrm((Lr, GATE_LORA, RWKV_DIM), GATE_LORA ** -0.5),
        'rwkv_k_k': 0.85 + nrm((Lr, RWKV_DIM), 0.05),
        'rwkv_k_a': gain((Lr, RWKV_DIM)),
        'rwkv_r_k': nrm((Lr, RWKV_DIM), 0.1),
        'rwkv_ln_w': gain((Lr, RWKV_DIM)),
        'rwkv_ln_b': nrm((Lr, RWKV_DIM), 0.02),
        's5_a_re': -0.5 + nrm((Lr, 2, G, P), 0.01),
        's5_a_im': math.pi * n_idx + nrm((Lr, 2, G, P), 0.01),
        's5_log_step': jax.random.uniform(next(ks), (Lr, 2, G), f32, math.log(DT_MIN), math.log(DT_MAX)),
        's5_b_re': nrm((Lr, 2, G, P, CH), (2.0 * CH) ** -0.5),
        's5_b_im': nrm((Lr, 2, G, P, CH), (2.0 * CH) ** -0.5),
        's5_c_re': nrm((Lr, G, CH, P), (2.0 * P) ** -0.5),
        's5_c_im': nrm((Lr, G, CH, P), (2.0 * P) ** -0.5),
        's5_d': nrm((Lr, S5_DIM), 1.0),
        's5_w_glu': nrm((Lr, S5_DIM, S5_DIM), S5_DIM ** -0.5),
        's5_b_glu': nrm((Lr, S5_DIM), 0.02),
        'proj_rwkv': nrm((Lr, RWKV_DIM, D_MODEL), RWKV_DIM ** -0.5),
        'proj_s5': nrm((Lr, S5_DIM, D_MODEL), S5_DIM ** -0.5),
        'w_out': nrm((Lr, D_MODEL, D_MODEL), D_MODEL ** -0.5),
        'norm_ffn2': gain((Lr, D_MODEL)),
        'ffn2_w_gate': nrm((Lr, D_MODEL, D_FF), D_MODEL ** -0.5),
        'ffn2_w_up': nrm((Lr, D_MODEL, D_FF), D_MODEL ** -0.5),
        'ffn2_w_down': nrm((Lr, D_FF, D_MODEL), D_FF ** -0.5),
        'norm_final': gain((D_MODEL,)),
    }


def reference(x_prompt, x_sample, norm_ffn1, ffn1_w_gate, ffn1_w_up, ffn1_w_down, norm_mix,
              w_in, shift_mu, rwkv_w0, rwkv_w2, rwkv_a0, rwkv_a2, rwkv_g2, rwkv_k_k, rwkv_k_a,
              rwkv_r_k, rwkv_ln_w, rwkv_ln_b, s5_a_re, s5_a_im, s5_log_step, s5_b_re, s5_b_im,
              s5_c_re, s5_c_im, s5_d, s5_w_glu, s5_b_glu, proj_rwkv, proj_s5, w_out,
              norm_ffn2, ffn2_w_gate, ffn2_w_up, ffn2_w_down, norm_final):
    layer_params = (norm_ffn1, ffn1_w_gate, ffn1_w_up, ffn1_w_down, norm_mix, w_in, shift_mu,
                    rwkv_w0, rwkv_w2, rwkv_a0, rwkv_a2, rwkv_g2, rwkv_k_k, rwkv_k_a, rwkv_r_k,
                    rwkv_ln_w, rwkv_ln_b, s5_a_re, s5_a_im, s5_log_step, s5_b_re, s5_b_im,
                    s5_c_re, s5_c_im, s5_d, s5_w_glu, s5_b_glu, proj_rwkv, proj_s5, w_out,
                    norm_ffn2, ffn2_w_gate, ffn2_w_up, ffn2_w_down)
    y_prompt = _trunk(x_prompt, layer_params, norm_final)
    y_sample = _trunk(x_sample, layer_params, norm_final)
    return (y_prompt, y_sample)
```

```python
import functools
import math

import jax
import jax.numpy as jnp
from jax import lax
from jax.experimental import pallas as pl
from jax.experimental.pallas import tpu as pltpu

F32 = jnp.float32
BF16 = jnp.bfloat16

D_MODEL = 2048
D_FF = 5504
HEADS = 16
HEAD_DIM = 64
RWKV_DIM = HEADS * HEAD_DIM
LORA = 64
GATE_LORA = 128
S5_GROUPS = 32
S5_CH = 16
S5_DIM = S5_GROUPS * S5_CH
S5_STATE = 64
S5_STATES = S5_GROUPS * S5_STATE
SHIFT_COLS = 3 * RWKV_DIM + 4 * LORA + GATE_LORA
RMS_EPS = 1e-6
GN_EPS = 64e-5
KK_EPS = 1e-12

LANES = 128
SUBLANES = 8
VMEM_LIMIT = 56 * 1024 * 1024

FF_TILE = 512
TOK_TILE = 512
CHUNK = 64
QUAD = 4 * HEAD_DIM
S5_SEQ_PAD = SUBLANES
S5_TILE = 64


def _cparams(sem):
    return pltpu.CompilerParams(dimension_semantics=sem, vmem_limit_bytes=VMEM_LIMIT)


def _dot(a, b):
    return jnp.dot(a.astype(BF16), b.astype(BF16), preferred_element_type=F32)


def _dot_nt(a, b):
    return lax.dot_general(a.astype(BF16), b.astype(BF16), (((1,), (1,)), ((), ())),
                           preferred_element_type=F32)


def _dot_tn(a, b):
    return lax.dot_general(a.astype(BF16), b.astype(BF16), (((0,), (0,)), ((), ())),
                           preferred_element_type=F32)


def _split_dot(x, w, pieces):
    acc = None
    rem = x
    for _ in range(pieces):
        part = rem.astype(BF16)
        term = jnp.dot(part, w, preferred_element_type=F32)
        acc = term if acc is None else acc + term
        rem = rem - part.astype(F32)
    return acc


def _sigmoid(x):
    return 1.0 / (1.0 + jnp.exp(-x))


def _rms(x, g):
    ms = jnp.mean(x * x, axis=-1, keepdims=True)
    return x * lax.rsqrt(ms + RMS_EPS) * g


def _ffn_kernel(x_ref, g_ref, wg_ref, wu_ref, wd_ref, gf_ref, o_ref, xn_ref, acc_ref, *, final_norm):
    j = pl.program_id(1)

    @pl.when(j == 0)
    def _():
        xn_ref[...] = _rms(x_ref[...], g_ref[...]).astype(BF16)
        acc_ref[...] = jnp.zeros_like(acc_ref)

    xn = xn_ref[...]
    hg = jnp.dot(xn, wg_ref[...], preferred_element_type=F32)
    hu = jnp.dot(xn, wu_ref[...], preferred_element_type=F32)
    act = (hg * _sigmoid(hg)) * hu
    acc_ref[...] += jnp.dot(act.astype(BF16), wd_ref[...], preferred_element_type=F32)

    @pl.when(j == pl.num_programs(1) - 1)
    def _():
        y = x_ref[...] + 0.5 * acc_ref[...]
        if final_norm:
            y = _rms(y, gf_ref[...])
        o_ref[...] = y


def _ffn(x, g, wg, wu, wd, gf, *, final_norm):
    n, d = x.shape
    f = wg.shape[1]
    tm, tf = TOK_TILE, FF_TILE
    return pl.pallas_call(
        functools.partial(_ffn_kernel, final_norm=final_norm),
        grid=(n // tm, f // tf),
        in_specs=[
            pl.BlockSpec((tm, d), lambda i, j: (i, 0)),
            pl.BlockSpec((1, d), lambda i, j: (0, 0)),
            pl.BlockSpec((d, tf), lambda i, j: (0, j)),
            pl.BlockSpec((d, tf), lambda i, j: (0, j)),
            pl.BlockSpec((tf, d), lambda i, j: (j, 0)),
            pl.BlockSpec((1, d), lambda i, j: (0, 0)),
        ],
        out_specs=pl.BlockSpec((tm, d), lambda i, j: (i, 0)),
        out_shape=jax.ShapeDtypeStruct((n, d), F32),
        scratch_shapes=[pltpu.VMEM((tm, d), BF16), pltpu.VMEM((tm, d), F32)],
        compiler_params=_cparams(("parallel", "arbitrary")),
        name="ffn_final" if final_norm else "ffn",
    )(x, g, wg, wu, wd, gf)


def _norm_matmul_kernel(x_ref, g_ref, w_ref, o_ref, xn_ref):
    @pl.when(pl.program_id(1) == 0)
    def _():
        xn_ref[...] = _rms(x_ref[...], g_ref[...]).astype(BF16)

    o_ref[...] = jnp.dot(xn_ref[...], w_ref[...], preferred_element_type=F32)


def _norm_matmul(x, g, w, tn, name):
    n, d = x.shape
    c = w.shape[1]
    tm = TOK_TILE
    return pl.pallas_call(
        _norm_matmul_kernel,
        grid=(n // tm, c // tn),
        in_specs=[
            pl.BlockSpec((tm, d), lambda i, j: (i, 0)),
            pl.BlockSpec((1, d), lambda i, j: (0, 0)),
            pl.BlockSpec((d, tn), lambda i, j: (0, j)),
        ],
        out_specs=pl.BlockSpec((tm, tn), lambda i, j: (i, j)),
        out_shape=jax.ShapeDtypeStruct((n, c), F32),
        scratch_shapes=[pltpu.VMEM((tm, d), BF16)],
        compiler_params=_cparams(("parallel", "arbitrary")),
        name=name,
    )(x, g, w)


PREP_TILE = 256


def _head_sum(x, ones_bd):
    return _split_dot(x, ones_bd, 2)


def _rwkv_prep_kernel(main_ref, prev_ref, next_ref, mu_ref, w2_ref, w0_ref, a2_ref, a0_ref, g2_ref,
                      kk_ref, ka_ref, rk_ref, ones_ref,
                      r_out, v_out, kkn_out, lw_out, kd_out, bd_out, g_out, bonus_out):
    i = pl.program_id(1)
    tl = main_ref.shape[1]
    x = main_ref[0]
    row = lax.broadcasted_iota(jnp.int32, (tl, 1), 0)
    before = jnp.where(i == 0, 0.0, prev_ref[0, SUBLANES - 1:SUBLANES, :])
    after = jnp.where(i == pl.num_programs(1) - 1, 0.0, next_ref[0, 0:1, :])
    prev = jnp.where(row == 0, before, pltpu.roll(x, 1, 0))
    nxt = jnp.where(row == tl - 1, after, pltpu.roll(x, tl - 1, 0))
    p = x + (0.5 * (prev + nxt) - x) * mu_ref[...]

    c0 = 3 * RWKV_DIM
    r = p[:, 0:RWKV_DIM]
    k = p[:, RWKV_DIM:2 * RWKV_DIM]
    v = p[:, 2 * RWKV_DIM:c0]
    wlow = jnp.tanh(p[:, c0:c0 + 2 * LORA])
    alow = p[:, c0 + 2 * LORA:c0 + 4 * LORA]
    glow = p[:, c0 + 4 * LORA:c0 + 4 * LORA + GATE_LORA]
    ones_bd = ones_ref[...]

    wpre = _dot(wlow, w2_ref[...]) + w0_ref[...]
    lw = (-math.exp(-0.5)) * _sigmoid(wpre)
    a = _sigmoid(_dot(alow, a2_ref[...]) + a0_ref[...])

    kk = k * kk_ref[...]
    kkn = kk * lax.rsqrt(_head_sum(kk * kk, ones_bd) + KK_EPS)
    ka = ka_ref[...]
    ksum = None
    for d in range(2):
        a_d = a[:, d * RWKV_DIM:(d + 1) * RWKV_DIM]
        kd = k * (1.0 + (a_d - 1.0) * ka)
        kd_out[d, 0] = kd
        bd_out[d, 0] = kkn * a_d
        lw_out[d, 0] = lw[:, d * RWKV_DIM:(d + 1) * RWKV_DIM]
        ksum = kd if ksum is None else ksum + kd
    r_out[0] = r
    v_out[0] = v
    kkn_out[0] = kkn
    g_out[0] = _dot(_sigmoid(glow), g2_ref[...])
    bonus_out[0] = _head_sum(r * ksum * rk_ref[...], ones_bd) * v


def _rwkv_prep(pshift, mu, w2cat, w0cat, a2cat, a0cat, g2, k_k, k_a, r_k, ones_bd):
    b, l, c = pshift.shape
    tl = PREP_TILE
    nt = l // tl
    hb = tl // SUBLANES
    nhb = l // SUBLANES
    full = lambda shape: pl.BlockSpec(shape, lambda bi, i: (0,) * len(shape))
    tok = pl.BlockSpec((1, tl, RWKV_DIM), lambda bi, i: (bi, i, 0))
    tok2 = pl.BlockSpec((2, 1, tl, RWKV_DIM), lambda bi, i: (0, bi, i, 0))
    s1 = jax.ShapeDtypeStruct((b, l, RWKV_DIM), F32)
    s2 = jax.ShapeDtypeStruct((2, b, l, RWKV_DIM), F32)
    return pl.pallas_call(
        _rwkv_prep_kernel,
        grid=(b, nt),
        in_specs=[
            pl.BlockSpec((1, tl, c), lambda bi, i: (bi, i, 0)),
            pl.BlockSpec((1, SUBLANES, c), lambda bi, i: (bi, jnp.maximum(i * hb - 1, 0), 0)),
            pl.BlockSpec((1, SUBLANES, c), lambda bi, i: (bi, jnp.minimum((i + 1) * hb, nhb - 1), 0)),
            full((1, c)),
            full(w2cat.shape), full(w0cat.shape), full(a2cat.shape), full(a0cat.shape), full(g2.shape),
            full((1, RWKV_DIM)), full((1, RWKV_DIM)), full((1, RWKV_DIM)), full(ones_bd.shape),
        ],
        out_specs=[tok, tok, tok, tok2, tok2, tok2, tok, tok],
        out_shape=[s1, s1, s1, s2, s2, s2, s1, s1],
        compiler_params=_cparams(("parallel", "arbitrary")),
        name="rwkv_prep",
    )(pshift, pshift, pshift, mu, w2cat, w0cat, a2cat, a0cat, g2, k_k, k_a, r_k, ones_bd)


def _stack_heads(x, head_masks):
    return jnp.concatenate([jnp.where(m, x, 0.0) for m in head_masks], axis=0)


def _rwkv_scan_kernel(r_ref, v_ref, kk_ref, lw_ref, kd_ref, bd_ref, o_ref, s_ref):
    d = pl.program_id(1)
    t = CHUNK

    @pl.when(pl.program_id(2) == 0)
    def _():
        s_ref[...] = jnp.zeros_like(s_ref)

    row = lax.broadcasted_iota(jnp.int32, (t, QUAD), 0)
    lane = lax.broadcasted_iota(jnp.int32, (t, QUAD), 1)
    col = lane & (HEAD_DIM - 1)
    sign = 1 - 2 * d
    strict = (row - col) * sign > 0
    incl = (row - col) * sign >= 0
    same16 = (row >> 4) == (col >> 4)
    same32 = (row >> 5) == (col >> 5)
    eye = jnp.where(col == row, 1.0, 0.0)
    head_masks = [(lane >> 6) == h for h in range(4)]
    trow = lax.broadcasted_iota(jnp.int32, (t, t), 0)
    tcol = lax.broadcasted_iota(jnp.int32, (t, t), 1)
    tri = jnp.where((trow - tcol) * sign >= 0, 1.0, 0.0).astype(BF16)
    qrow = lax.broadcasted_iota(jnp.int32, (QUAD, QUAD), 0) >> 6
    qcol = lax.broadcasted_iota(jnp.int32, (QUAD, QUAD), 1) >> 6
    same_head = qrow == qcol
    stack = lambda x: _stack_heads(x, head_masks)

    for q in range(RWKV_DIM // QUAD):
        sl = slice(q * QUAD, (q + 1) * QUAD)
        r = r_ref[0, :, sl]
        v = v_ref[0, :, sl]
        kk = kk_ref[0, :, sl]
        lw = lw_ref[0, 0, :, sl]
        kd = kd_ref[0, 0, :, sl]
        bd = bd_ref[0, 0, :, sl]
        s0 = s_ref[q]

        cum = _cumsum(tri, lw)
        e_in = jnp.exp(cum)
        e_out = jnp.exp(-cum)
        at = -kk * jnp.exp(cum - lw)
        rt = r * e_in
        bt = bd * e_out
        kt = kd * e_out
        w_total = jnp.exp(jnp.sum(lw, axis=0, keepdims=True))

        lhs = jnp.concatenate([at, rt], axis=0)
        gram = _dot_nt(lhs, jnp.concatenate([stack(bt), stack(kt)], axis=0))
        a_ab = jnp.where(strict, gram[:t, :QUAD], 0.0)
        a_ak = jnp.where(strict, gram[:t, QUAD:], 0.0)
        a_rb = jnp.where(incl, gram[t:, :QUAD], 0.0)
        a_rk = jnp.where(incl, gram[t:, QUAD:], 0.0)
        from_state = _dot_nt(lhs, s0)
        from_v = _dot(jnp.concatenate([a_ak, a_rk], axis=0), stack(v))
        rhs_u = from_state[:t] + from_v[:t]

        a_d = jnp.where(same16, a_ab, 0.0)
        a_1 = jnp.where(same32 & jnp.logical_not(same16), a_ab, 0.0)
        a_2 = jnp.where(same32, 0.0, a_ab)
        pw = eye + a_d
        sq = _dot(a_d, stack(a_d))
        both = _dot(jnp.concatenate([pw, sq], axis=0), stack(sq))
        pw = pw + both[:t]
        sq = both[t:]
        both = _dot(jnp.concatenate([pw, sq], axis=0), stack(sq))
        pw = pw + both[:t]
        sq = both[t:]
        x0 = pw + _dot(pw, stack(sq))
        x1 = x0 + _dot(_dot(x0, stack(a_1)), stack(x0))
        z = _dot(x1, stack(rhs_u))
        z2 = _dot(a_2, stack(z))
        u = z + _dot(x1, stack(z2))

        o_ref[0, 0, :, sl] = from_state[t:] + from_v[t:] + _dot(a_rb, stack(u))
        upd = _dot_tn(jnp.concatenate([u, v], axis=0), jnp.concatenate([bt, kt], axis=0))
        s_ref[q] = (s0 + jnp.where(same_head, upd, 0.0)) * w_total


def _cumsum(tri, lw):
    acc = None
    rem = lw
    for _ in range(3):
        part = rem.astype(BF16)
        term = jnp.dot(tri, part, preferred_element_type=F32)
        acc = term if acc is None else acc + term
        rem = rem - part.astype(F32)
    return acc


def _rwkv_scan(r, v, kkn, lw, kd, bd):
    b, l, c = r.shape
    nc = l // CHUNK
    pos = lambda dd, i: i + dd * (nc - 1 - 2 * i)
    shared = pl.BlockSpec((1, CHUNK, c), lambda bi, dd, i: (bi, pos(dd, i), 0))
    per_dir = pl.BlockSpec((1, 1, CHUNK, c), lambda bi, dd, i: (dd, bi, pos(dd, i), 0))
    return pl.pallas_call(
        _rwkv_scan_kernel,
        grid=(b, 2, nc),
        in_specs=[shared, shared, shared, per_dir, per_dir, per_dir],
        out_specs=per_dir,
        out_shape=jax.ShapeDtypeStruct((2, b, l, c), F32),
        scratch_shapes=[pltpu.VMEM((c // QUAD, QUAD, QUAD), F32)],
        compiler_params=_cparams(("parallel", "parallel", "arbitrary")),
        name="rwkv_scan",
    )(r, v, kkn, lw, kd, bd)


def _rwkv_post_kernel(yf_ref, yb_ref, bonus_ref, g_ref, lnw_ref, lnb_ref, ones_ref, o_ref):
    ones_bd = ones_ref[...]
    y = yf_ref[0, 0] + yb_ref[0, 0]
    mean = _head_sum(y, ones_bd) * (1.0 / HEAD_DIM)
    yc = y - mean
    var = _head_sum(yc * yc, ones_bd) * (1.0 / HEAD_DIM)
    yn = yc * lax.rsqrt(var + GN_EPS) * lnw_ref[...] + lnb_ref[...]
    o_ref[0] = ((yn + bonus_ref[0]) * g_ref[0]).astype(BF16)


def _rwkv_post(y2, bonus, g, ln_w, ln_b, ones_bd):
    _, b, l, c = y2.shape
    tl = PREP_TILE
    tok = pl.BlockSpec((1, tl, c), lambda bi, i: (bi, i, 0))
    row = pl.BlockSpec((1, c), lambda bi, i: (0, 0))
    return pl.pallas_call(
        _rwkv_post_kernel,
        grid=(b, l // tl),
        in_specs=[
            pl.BlockSpec((1, 1, tl, c), lambda bi, i: (0, bi, i, 0)),
            pl.BlockSpec((1, 1, tl, c), lambda bi, i: (1, bi, i, 0)),
            tok, tok, row, row,
            pl.BlockSpec(ones_bd.shape, lambda bi, i: (0, 0)),
        ],
        out_specs=tok,
        out_shape=jax.ShapeDtypeStruct((b, l, c), BF16),
        compiler_params=_cparams(("parallel", "parallel")),
        name="rwkv_post",
    )(y2, y2, bonus, g, ln_w, ln_b, ones_bd)


def _s5_scan_kernel(u_ref, bdense_ref, cdense_ref, lam_ref, o_ref, st_ref, s_ref):
    d = pl.program_id(0)
    tl = u_ref.shape[0]
    ns = S5_STATES

    @pl.when(pl.program_id(1) == 0)
    def _():
        s_ref[...] = jnp.zeros_like(s_ref)

    u = u_ref[...].reshape(tl * S5_SEQ_PAD, S5_DIM)
    st_ref[...] = _dot(u, bdense_ref[0]).reshape(tl, S5_SEQ_PAD, 2 * ns)
    lam_re = lam_ref[0, 0]
    lam_im = lam_ref[0, 1]

    def step(j, carry):
        s_re, s_im = carry
        tt = jnp.where(d == 0, j, tl - 1 - j)
        n_re = lam_re * s_re - lam_im * s_im + st_ref[tt, :, :ns]
        n_im = lam_re * s_im + lam_im * s_re + st_ref[tt, :, ns:]
        st_ref[tt, :, :ns] = n_re
        st_ref[tt, :, ns:] = n_im
        return n_re, n_im

    s_re, s_im = lax.fori_loop(0, tl, step, (s_ref[0], s_ref[1]))
    s_ref[0] = s_re
    s_ref[1] = s_im
    y = _dot(st_ref[...].reshape(tl * S5_SEQ_PAD, 2 * ns), cdense_ref[...])
    o_ref[0] = y.reshape(tl, S5_SEQ_PAD, S5_DIM)


def _s5_scan(u_t, bdense, cdense, lam):
    l = u_t.shape[0]
    tl = S5_TILE
    nt = l // tl
    pos = lambda dd, i: i + dd * (nt - 1 - 2 * i)
    return pl.pallas_call(
        _s5_scan_kernel,
        grid=(2, nt),
        in_specs=[
            pl.BlockSpec((tl, S5_SEQ_PAD, S5_DIM), lambda dd, i: (pos(dd, i), 0, 0)),
            pl.BlockSpec((1, S5_DIM, 2 * S5_STATES), lambda dd, i: (dd, 0, 0)),
            pl.BlockSpec((2 * S5_STATES, S5_DIM), lambda dd, i: (0, 0)),
            pl.BlockSpec((1, 2, S5_SEQ_PAD, S5_STATES), lambda dd, i: (dd, 0, 0, 0)),
        ],
        out_specs=pl.BlockSpec((1, tl, S5_SEQ_PAD, S5_DIM), lambda dd, i: (dd, pos(dd, i), 0, 0)),
        out_shape=jax.ShapeDtypeStruct((2, l, S5_SEQ_PAD, S5_DIM), F32),
        scratch_shapes=[pltpu.VMEM((tl, S5_SEQ_PAD, 2 * S5_STATES), F32),
                        pltpu.VMEM((2, S5_SEQ_PAD, S5_STATES), F32)],
        compiler_params=_cparams(("parallel", "arbitrary")),
        name="s5_scan",
    )(u_t, bdense, cdense, lam)


def _s5_post_kernel(yf_ref, yb_ref, u_ref, dskip_ref, wglu_ref, bglu_ref, o_ref):
    tl = u_ref.shape[0]
    rows = tl * S5_SEQ_PAD
    u = u_ref[...].reshape(rows, S5_DIM)
    y = (yf_ref[0] + yb_ref[0]).reshape(rows, S5_DIM) + dskip_ref[...] * u
    y = 0.5 * y * (1.0 + jnp.tanh(math.sqrt(2.0 / math.pi) * (y + 0.044715 * (y * y * y))))
    gate = _sigmoid(_dot(y, wglu_ref[...]) + bglu_ref[...])
    o_ref[...] = (y * gate).astype(BF16).reshape(tl, S5_SEQ_PAD, S5_DIM)


def _s5_post(y2, u_t, d_skip, w_glu, b_glu):
    l = u_t.shape[0]
    tl = S5_TILE
    tok = pl.BlockSpec((tl, S5_SEQ_PAD, S5_DIM), lambda i: (i, 0, 0))
    row = pl.BlockSpec((1, S5_DIM), lambda i: (0, 0))
    return pl.pallas_call(
        _s5_post_kernel,
        grid=(l // tl,),
        in_specs=[
            pl.BlockSpec((1, tl, S5_SEQ_PAD, S5_DIM), lambda i: (0, i, 0, 0)),
            pl.BlockSpec((1, tl, S5_SEQ_PAD, S5_DIM), lambda i: (1, i, 0, 0)),
            tok, row,
            pl.BlockSpec((S5_DIM, S5_DIM), lambda i: (0, 0)),
            row,
        ],
        out_specs=tok,
        out_shape=jax.ShapeDtypeStruct((l, S5_SEQ_PAD, S5_DIM), BF16),
        compiler_params=_cparams(("parallel",)),
        name="s5_post",
    )(y2, y2, u_t, d_skip, w_glu, b_glu)


def _s5_params(a_re, a_im, log_step, b_re, b_im, c_re, c_im):
    dt = jnp.exp(log_step)[..., None]
    z_re, z_im = a_re * dt, a_im * dt
    mag = jnp.exp(z_re)
    lam_re, lam_im = mag * jnp.cos(z_im), mag * jnp.sin(z_im)
    den = a_re * a_re + a_im * a_im
    q_re = ((lam_re - 1.0) * a_re + lam_im * a_im) / den
    q_im = (lam_im * a_re - (lam_re - 1.0) * a_im) / den
    bb_re = q_re[..., None] * b_re - q_im[..., None] * b_im
    bb_im = q_re[..., None] * b_im + q_im[..., None] * b_re
    eye = jnp.eye(S5_GROUPS, dtype=F32)

    def dense_b(x):
        return jnp.einsum('dgpc,gh->dgchp', x, eye).reshape(2, S5_DIM, S5_STATES)

    def dense_c(x):
        return jnp.einsum('gcp,gh->gphc', x, eye).reshape(S5_STATES, S5_DIM)

    bdense = jnp.concatenate([dense_b(bb_re), dense_b(bb_im)], axis=-1).astype(BF16)
    cdense = jnp.concatenate([dense_c(c_re), -dense_c(c_im)], axis=0).astype(BF16)
    lam = jnp.stack([lam_re.reshape(2, S5_STATES), lam_im.reshape(2, S5_STATES)], axis=1)
    lam = jnp.broadcast_to(lam[:, :, None, :], (2, 2, S5_SEQ_PAD, S5_STATES))
    return bdense, cdense, lam


MERGE_TILE = 256


def _merge_kernel(x_ref, yr_ref, ys_ref, gate_ref, pr_ref, ps_ref, wo_ref, o_ref):
    y_rwkv = jnp.dot(yr_ref[...], pr_ref[...], preferred_element_type=F32)
    y_s5 = jnp.dot(ys_ref[...], ps_ref[...], preferred_element_type=F32)
    gates = _sigmoid(gate_ref[...])
    merged = gates[:, :D_MODEL] * y_rwkv + gates[:, D_MODEL:] * y_s5
    o_ref[...] = x_ref[...] + _dot(merged, wo_ref[...])


def _merge(x, yr, ys, gates, proj_rwkv, proj_s5, w_out):
    n, d = x.shape
    tm = MERGE_TILE
    tok = lambda c: pl.BlockSpec((tm, c), lambda i: (i, 0))
    full = lambda a: pl.BlockSpec(a.shape, lambda i: (0, 0))
    return pl.pallas_call(
        _merge_kernel,
        grid=(n // tm,),
        in_specs=[tok(d), tok(RWKV_DIM), tok(S5_DIM), tok(2 * d),
                  full(proj_rwkv), full(proj_s5), full(w_out)],
        out_specs=tok(d),
        out_shape=jax.ShapeDtypeStruct((n, d), F32),
        compiler_params=_cparams(("parallel",)),
        name="merge",
    )(x, yr, ys, gates, proj_rwkv, proj_s5, w_out)


def _pad_ff(w, axis):
    pad = (-w.shape[axis]) % FF_TILE
    widths = [(0, 0)] * w.ndim
    widths[axis] = (0, pad)
    return jnp.pad(w, widths).astype(BF16)


def _lora_cat(w):
    z = jnp.zeros_like(w[0])
    return jnp.concatenate([jnp.concatenate([w[0], z], axis=1),
                            jnp.concatenate([z, w[1]], axis=1)], axis=0).astype(BF16)


def _forward(x, norm_ffn1, ffn1_w_gate, ffn1_w_up, ffn1_w_down, norm_mix, w_in, shift_mu,
             rwkv_w0, rwkv_w2, rwkv_a0, rwkv_a2, rwkv_g2, rwkv_k_k, rwkv_k_a, rwkv_r_k,
             rwkv_ln_w, rwkv_ln_b, s5_a_re, s5_a_im, s5_log_step, s5_b_re, s5_b_im,
             s5_c_re, s5_c_im, s5_d, s5_w_glu, s5_b_glu, proj_rwkv, proj_s5, w_out,
             norm_ffn2, ffn2_w_gate, ffn2_w_up, ffn2_w_down, norm_final):
    b, l, d = x.shape
    n = b * l
    row = lambda p: p.reshape(1, -1)
    xf = x.reshape(n, d)

    x1 = _ffn(xf, row(norm_ffn1), _pad_ff(ffn1_w_gate, 1), _pad_ff(ffn1_w_up, 1),
              _pad_ff(ffn1_w_down, 0), row(norm_final), final_norm=False)

    u_off = SHIFT_COLS
    g_off = SHIFT_COLS + S5_DIM
    gmix = row(norm_mix)
    pshift = _norm_matmul(x1, gmix, w_in[:, :u_off].astype(BF16), 1152, "proj_shift")
    u = _norm_matmul(x1, gmix, w_in[:, u_off:g_off].astype(BF16), S5_DIM, "proj_u")
    gates = _norm_matmul(x1, gmix, w_in[:, g_off:].astype(BF16), 1024, "proj_gates")

    head_id = jnp.arange(RWKV_DIM) // HEAD_DIM
    ones_bd = (head_id[:, None] == head_id[None, :]).astype(BF16)
    r, v, kkn, lw, kd, bd, g, bonus = _rwkv_prep(
        pshift.reshape(b, l, SHIFT_COLS), row(shift_mu),
        _lora_cat(rwkv_w2), rwkv_w0.reshape(1, -1), _lora_cat(rwkv_a2), rwkv_a0.reshape(1, -1),
        rwkv_g2.astype(BF16), row(rwkv_k_k), row(rwkv_k_a), row(rwkv_r_k), ones_bd)
    y2 = _rwkv_scan(r, v, kkn, lw, kd, bd)
    yr = _rwkv_post(y2, bonus, g, row(rwkv_ln_w), row(rwkv_ln_b), ones_bd).reshape(n, RWKV_DIM)

    u_t = jnp.transpose(u.reshape(b, l, S5_DIM), (1, 0, 2))
    u_t = jnp.pad(u_t, ((0, 0), (0, S5_SEQ_PAD - b), (0, 0)))
    bdense, cdense, lam = _s5_params(s5_a_re, s5_a_im, s5_log_step, s5_b_re, s5_b_im, s5_c_re, s5_c_im)
    ys2 = _s5_scan(u_t, bdense, cdense, lam)
    ys_t = _s5_post(ys2, u_t, row(s5_d), s5_w_glu.astype(BF16), row(s5_b_glu))
    ys = jnp.transpose(ys_t[:, :b], (1, 0, 2)).reshape(n, S5_DIM)

    x2 = _merge(x1, yr, ys, gates, proj_rwkv.astype(BF16), proj_s5.astype(BF16), w_out.astype(BF16))
    out = _ffn(x2, row(norm_ffn2), _pad_ff(ffn2_w_gate, 1), _pad_ff(ffn2_w_up, 1),
               _pad_ff(ffn2_w_down, 0), row(norm_final), final_norm=True)
    return out.reshape(b, l, d)


def kernel(x_prompt, x_sample, norm_ffn1, ffn1_w_gate, ffn1_w_up, ffn1_w_down, norm_mix, w_in, shift_mu, rwkv_w0, rwkv_w2, rwkv_a0, rwkv_a2, rwkv_g2, rwkv_k_k, rwkv_k_a, rwkv_r_k, rwkv_ln_w, rwkv_ln_b, s5_a_re, s5_a_im, s5_log_step, s5_b_re, s5_b_im, s5_c_re, s5_c_im, s5_d, s5_w_glu, s5_b_glu, proj_rwkv, proj_s5, w_out, norm_ffn2, ffn2_w_gate, ffn2_w_up, ffn2_w_down, norm_final):
    layer = (norm_ffn1, ffn1_w_gate, ffn1_w_up, ffn1_w_down, norm_mix, w_in, shift_mu,
             rwkv_w0, rwkv_w2, rwkv_a0, rwkv_a2, rwkv_g2, rwkv_k_k, rwkv_k_a, rwkv_r_k,
             rwkv_ln_w, rwkv_ln_b, s5_a_re, s5_a_im, s5_log_step, s5_b_re, s5_b_im,
             s5_c_re, s5_c_im, s5_d, s5_w_glu, s5_b_glu, proj_rwkv, proj_s5, w_out,
             norm_ffn2, ffn2_w_gate, ffn2_w_up, ffn2_w_down)
    assert all(p.shape[0] == 1 for p in layer), "single-layer block"
    assert x_prompt.shape[1:] == x_sample.shape[1:]
    nb = x_prompt.shape[0]
    x = jnp.concatenate([x_prompt, x_sample], axis=0)
    y = _forward(x, *[p[0] for p in layer], norm_final)
    return (y[:nb], y[nb:])
```

```python
import functools
import math

import jax
import jax.numpy as jnp
from jax import lax
from jax.experimental import pallas as pl
from jax.experimental.pallas import tpu as pltpu

F32 = jnp.float32
BF16 = jnp.bfloat16

D_MODEL = 2048
D_FF = 5504
HEADS = 16
HEAD_DIM = 64
RWKV_DIM = HEADS * HEAD_DIM
LORA = 64
GATE_LORA = 128
S5_GROUPS = 32
S5_CH = 16
S5_DIM = S5_GROUPS * S5_CH
S5_STATE = 64
S5_STATES = S5_GROUPS * S5_STATE
SHIFT_COLS = 3 * RWKV_DIM + 4 * LORA + GATE_LORA
RMS_EPS = 1e-6
GN_EPS = 64e-5
KK_EPS = 1e-12

LANES = 128
SUBLANES = 8
VMEM_LIMIT = 56 * 1024 * 1024

FF_TILE = 512
TOK_TILE = 512
CHUNK = 64
QUAD = 4 * HEAD_DIM
S5_SEQ_PAD = SUBLANES
S5_TILE = 64
S5_SPLIT = 2
PROJ_TOK_TILE = 1024
PROJ_COL_TILE = 1024
PROJ_U_OFF = 3584
PROJ_GATE_OFF = 4096
PROJ_COLS = 8192


def _cparams(sem):
    return pltpu.CompilerParams(dimension_semantics=sem, vmem_limit_bytes=VMEM_LIMIT)


def _dot(a, b):
    return jnp.dot(a.astype(BF16), b.astype(BF16), preferred_element_type=F32)


def _dot_nt(a, b):
    return lax.dot_general(a.astype(BF16), b.astype(BF16), (((1,), (1,)), ((), ())),
                           preferred_element_type=F32)


def _dot_tn(a, b):
    return lax.dot_general(a.astype(BF16), b.astype(BF16), (((0,), (0,)), ((), ())),
                           preferred_element_type=F32)


def _split_dot(x, w, pieces):
    acc = None
    rem = x
    for _ in range(pieces):
        part = rem.astype(BF16)
        term = jnp.dot(part, w, preferred_element_type=F32)
        acc = term if acc is None else acc + term
        rem = rem - part.astype(F32)
    return acc


def _sigmoid(x):
    return 1.0 / (1.0 + jnp.exp(-x))


def _rms(x, g):
    ms = jnp.mean(x * x, axis=-1, keepdims=True)
    return x * lax.rsqrt(ms + RMS_EPS) * g


def _ffn_kernel(*refs, n_x, n_out, first_tiles, final_norm):
    x_refs = refs[:n_x]
    g_ref, wg_ref, wu_ref, wd_ref, gf_ref = refs[n_x:n_x + 5]
    o_refs = refs[n_x + 5:n_x + 5 + n_out]
    xn_ref, acc_ref = refs[n_x + 5 + n_out:]
    i = pl.program_id(0)
    j = pl.program_id(1)

    def x_tile():
        if n_x == 1:
            return x_refs[0][...]
        return jnp.where(i < first_tiles, x_refs[0][...], x_refs[1][...])

    @pl.when(j == 0)
    def _():
        xn_ref[...] = _rms(x_tile(), g_ref[...]).astype(BF16)
        acc_ref[...] = jnp.zeros_like(acc_ref)

    xn = xn_ref[...]
    hg = jnp.dot(xn, wg_ref[...], preferred_element_type=F32)
    hu = jnp.dot(xn, wu_ref[...], preferred_element_type=F32)
    act = (hg * _sigmoid(hg)) * hu
    acc_ref[...] += jnp.dot(act.astype(BF16), wd_ref[...], preferred_element_type=F32)

    last = j == pl.num_programs(1) - 1

    def result():
        y = x_tile() + 0.5 * acc_ref[...]
        return _rms(y, gf_ref[...]) if final_norm else y

    if n_out == 1:
        @pl.when(last)
        def _():
            o_refs[0][...] = result()
    else:
        @pl.when(last & (i < first_tiles))
        def _():
            o_refs[0][...] = result()

        @pl.when(last & (i >= first_tiles))
        def _():
            o_refs[1][...] = result()


def _ffn(xs, g, wg, wu, wd, gf, *, out_rows, final_norm):
    d = xs[0].shape[1]
    f = wg.shape[1]
    tm, tf = TOK_TILE, FF_TILE
    n = sum(x.shape[0] for x in xs)
    assert n == sum(out_rows) and all(r % tm == 0 for r in out_rows)
    assert all(x.shape[0] % tm == 0 for x in xs)
    first_rows = xs[0].shape[0] if len(xs) == 2 else out_rows[0]
    if len(xs) == 2 and len(out_rows) == 2:
        assert xs[0].shape[0] == out_rows[0]
    first_tiles = first_rows // tm

    def split_specs(count):
        if count == 1:
            return [pl.BlockSpec((tm, d), lambda i, j: (i, 0))]
        return [pl.BlockSpec((tm, d), lambda i, j: (jnp.minimum(i, first_tiles - 1), 0)),
                pl.BlockSpec((tm, d), lambda i, j: (jnp.maximum(i - first_tiles, 0), 0))]

    outs = pl.pallas_call(
        functools.partial(_ffn_kernel, n_x=len(xs), n_out=len(out_rows), first_tiles=first_tiles,
                          final_norm=final_norm),
        grid=(n // tm, f // tf),
        in_specs=split_specs(len(xs)) + [
            pl.BlockSpec((1, d), lambda i, j: (0, 0)),
            pl.BlockSpec((d, tf), lambda i, j: (0, j)),
            pl.BlockSpec((d, tf), lambda i, j: (0, j)),
            pl.BlockSpec((tf, d), lambda i, j: (j, 0)),
            pl.BlockSpec((1, d), lambda i, j: (0, 0)),
        ],
        out_specs=split_specs(len(out_rows)),
        out_shape=[jax.ShapeDtypeStruct((r, d), F32) for r in out_rows],
        scratch_shapes=[pltpu.VMEM((tm, d), BF16), pltpu.VMEM((tm, d), F32)],
        compiler_params=_cparams(("arbitrary", "arbitrary")),
        name="ffn_final" if final_norm else "ffn",
    )(*xs, g, wg, wu, wd, gf)
    return outs


def _norm_matmul_kernel(x_ref, g_ref, w_ref, o_ref, xn_ref):
    @pl.when(pl.program_id(1) == 0)
    def _():
        xn_ref[...] = _rms(x_ref[...], g_ref[...]).astype(BF16)

    o_ref[...] = jnp.dot(xn_ref[...], w_ref[...], preferred_element_type=F32)


def _norm_matmul(x, g, w, tm, tn, name):
    n, d = x.shape
    c = w.shape[1]
    return pl.pallas_call(
        _norm_matmul_kernel,
        grid=(n // tm, c // tn),
        in_specs=[
            pl.BlockSpec((tm, d), lambda i, j: (i, 0)),
            pl.BlockSpec((1, d), lambda i, j: (0, 0)),
            pl.BlockSpec((d, tn), lambda i, j: (0, j)),
        ],
        out_specs=pl.BlockSpec((tm, tn), lambda i, j: (i, j)),
        out_shape=jax.ShapeDtypeStruct((n, c), F32),
        scratch_shapes=[pltpu.VMEM((tm, d), BF16)],
        compiler_params=_cparams(("parallel", "arbitrary")),
        name=name,
    )(x, g, w)


PREP_TILE = 256


def _head_sum(x, ones_bd):
    return _split_dot(x, ones_bd, 2)


def _rwkv_prep_kernel(main_ref, prev_ref, next_ref, mu_ref, w2_ref, w0_ref, a2_ref, a0_ref, g2_ref,
                      kk_ref, ka_ref, rk_ref, ones_ref,
                      r_out, v_out, kkn_out, lw_out, kd_out, bd_out, g_out, bonus_out):
    i = pl.program_id(1)
    tl = main_ref.shape[1]
    x = main_ref[0]
    row = lax.broadcasted_iota(jnp.int32, (tl, 1), 0)
    before = jnp.where(i == 0, 0.0, prev_ref[0, SUBLANES - 1:SUBLANES, :])
    after = jnp.where(i == pl.num_programs(1) - 1, 0.0, next_ref[0, 0:1, :])
    prev = jnp.where(row == 0, before, pltpu.roll(x, 1, 0))
    nxt = jnp.where(row == tl - 1, after, pltpu.roll(x, tl - 1, 0))
    p = x + (0.5 * (prev + nxt) - x) * mu_ref[...]

    c0 = 3 * RWKV_DIM
    r = p[:, 0:RWKV_DIM]
    k = p[:, RWKV_DIM:2 * RWKV_DIM]
    v = p[:, 2 * RWKV_DIM:c0]
    wlow = jnp.tanh(p[:, c0:c0 + 2 * LORA])
    alow = p[:, c0 + 2 * LORA:c0 + 4 * LORA]
    glow = p[:, c0 + 4 * LORA:c0 + 4 * LORA + GATE_LORA]
    ones_bd = ones_ref[...]

    wpre = _dot(wlow, w2_ref[...]) + w0_ref[...]
    lw = (-math.exp(-0.5)) * _sigmoid(wpre)
    a = _sigmoid(_dot(alow, a2_ref[...]) + a0_ref[...])

    kk = k * kk_ref[...]
    kkn = kk * lax.rsqrt(_head_sum(kk * kk, ones_bd) + KK_EPS)
    ka = ka_ref[...]
    ksum = None
    for d in range(2):
        a_d = a[:, d * RWKV_DIM:(d + 1) * RWKV_DIM]
        kd = k * (1.0 + (a_d - 1.0) * ka)
        kd_out[d, 0] = kd
        bd_out[d, 0] = kkn * a_d
        lw_out[d, 0] = lw[:, d * RWKV_DIM:(d + 1) * RWKV_DIM]
        ksum = kd if ksum is None else ksum + kd
    r_out[0] = r
    v_out[0] = v
    kkn_out[0] = kkn
    g_out[0] = _dot(_sigmoid(glow), g2_ref[...])
    bonus_out[0] = _head_sum(r * ksum * rk_ref[...], ones_bd) * v


def _rwkv_prep(pshift, mu, w2cat, w0cat, a2cat, a0cat, g2, k_k, k_a, r_k, ones_bd):
    b, l, _ = pshift.shape
    c = SHIFT_COLS
    tl = PREP_TILE
    nt = l // tl
    hb = tl // SUBLANES
    nhb = l // SUBLANES
    full = lambda shape: pl.BlockSpec(shape, lambda bi, i: (0,) * len(shape))
    tok = pl.BlockSpec((1, tl, RWKV_DIM), lambda bi, i: (bi, i, 0))
    tok2 = pl.BlockSpec((2, 1, tl, RWKV_DIM), lambda bi, i: (0, bi, i, 0))
    s1 = jax.ShapeDtypeStruct((b, l, RWKV_DIM), F32)
    s2 = jax.ShapeDtypeStruct((2, b, l, RWKV_DIM), F32)
    return pl.pallas_call(
        _rwkv_prep_kernel,
        grid=(b, nt),
        in_specs=[
            pl.BlockSpec((1, tl, c), lambda bi, i: (bi, i, 0)),
            pl.BlockSpec((1, SUBLANES, c), lambda bi, i: (bi, jnp.maximum(i * hb - 1, 0), 0)),
            pl.BlockSpec((1, SUBLANES, c), lambda bi, i: (bi, jnp.minimum((i + 1) * hb, nhb - 1), 0)),
            full((1, c)),
            full(w2cat.shape), full(w0cat.shape), full(a2cat.shape), full(a0cat.shape), full(g2.shape),
            full((1, RWKV_DIM)), full((1, RWKV_DIM)), full((1, RWKV_DIM)), full(ones_bd.shape),
        ],
        out_specs=[tok, tok, tok, tok2, tok2, tok2, tok, tok],
        out_shape=[s1, s1, s1, s2, s2, s2, s1, s1],
        compiler_params=_cparams(("parallel", "arbitrary")),
        name="rwkv_prep",
    )(pshift, pshift, pshift, mu, w2cat, w0cat, a2cat, a0cat, g2, k_k, k_a, r_k, ones_bd)


def _stack_heads(x, head_masks):
    return jnp.concatenate([jnp.where(m, x, 0.0) for m in head_masks], axis=0)


def _rwkv_scan_kernel(rf_ref, vf_ref, kkf_ref, rb_ref, vb_ref, kkb_ref,
                      lwf_ref, kdf_ref, bdf_ref, lwb_ref, kdb_ref, bdb_ref,
                      of_ref, ob_ref, s_ref):
    t = CHUNK

    @pl.when(pl.program_id(1) == 0)
    def _():
        s_ref[...] = jnp.zeros_like(s_ref)

    row = lax.broadcasted_iota(jnp.int32, (t, QUAD), 0)
    lane = lax.broadcasted_iota(jnp.int32, (t, QUAD), 1)
    col = lane & (HEAD_DIM - 1)
    same16 = (row >> 4) == (col >> 4)
    same32 = (row >> 5) == (col >> 5)
    in32_off16 = same32 & jnp.logical_not(same16)
    eye = jnp.where(col == row, 1.0, 0.0)
    head_masks = [(lane >> 6) == h for h in range(4)]
    trow = lax.broadcasted_iota(jnp.int32, (t, t), 0)
    tcol = lax.broadcasted_iota(jnp.int32, (t, t), 1)
    strict = (col < row, col > row)
    incl = (col <= row, col >= row)
    tri = (jnp.where(tcol <= trow, 1.0, 0.0).astype(BF16),
           jnp.where(tcol >= trow, 1.0, 0.0).astype(BF16))
    qrow = lax.broadcasted_iota(jnp.int32, (QUAD, QUAD), 0) >> 6
    qcol = lax.broadcasted_iota(jnp.int32, (QUAD, QUAD), 1) >> 6
    same_head = qrow == qcol
    stack = lambda x: _stack_heads(x, head_masks)
    cat = lambda a, b: jnp.concatenate([a, b], axis=0)

    refs = ((rf_ref, vf_ref, kkf_ref, lwf_ref, kdf_ref, bdf_ref, of_ref),
            (rb_ref, vb_ref, kkb_ref, lwb_ref, kdb_ref, bdb_ref, ob_ref))
    chains = [(d, q) for q in range(RWKV_DIM // QUAD) for d in range(2)]
    each = lambda fn, *lists: [fn(*args) for args in zip(*lists)]
    sl = lambda q: slice(q * QUAD, (q + 1) * QUAD)

    lw = [refs[d][3][0, 0, :, sl(q)] for d, q in chains]
    cum = [_cumsum(tri[d], x) for (d, _), x in zip(chains, lw)]
    e_out = each(lambda c: jnp.exp(-c), cum)
    at = [-refs[d][2][0, :, sl(q)] * jnp.exp(c - x) for (d, q), c, x in zip(chains, cum, lw)]
    rt = [refs[d][0][0, :, sl(q)] * jnp.exp(c) for (d, q), c in zip(chains, cum)]
    bt = [refs[d][5][0, 0, :, sl(q)] * e for (d, q), e in zip(chains, e_out)]
    kt = [refs[d][4][0, 0, :, sl(q)] * e for (d, q), e in zip(chains, e_out)]
    v = [refs[d][1][0, :, sl(q)] for d, q in chains]
    w_total = each(lambda x: jnp.exp(jnp.sum(x, axis=0, keepdims=True)), lw)
    lhs = each(cat, at, rt)
    bk = each(cat, bt, kt)

    gram = each(lambda l_, b_, k_: _dot_nt(l_, cat(stack(b_), stack(k_))), lhs, bt, kt)
    a_ab = [jnp.where(strict[d], g[:t, :QUAD], 0.0) for (d, _), g in zip(chains, gram)]
    a_ak = [jnp.where(strict[d], g[:t, QUAD:], 0.0) for (d, _), g in zip(chains, gram)]
    a_rb = [jnp.where(incl[d], g[t:, :QUAD], 0.0) for (d, _), g in zip(chains, gram)]
    a_rk = [jnp.where(incl[d], g[t:, QUAD:], 0.0) for (d, _), g in zip(chains, gram)]
    from_v = each(lambda p, q_, v_: _dot(cat(p, q_), stack(v_)), a_ak, a_rk, v)

    a_d = each(lambda a: jnp.where(same16, a, 0.0), a_ab)
    pw = each(lambda a: eye + a, a_d)
    sq = each(lambda a: _dot(a, stack(a)), a_d)
    for _ in range(2):
        both = each(lambda p, s: _dot(cat(p, s), stack(s)), pw, sq)
        pw = each(lambda p, bo: p + bo[:t], pw, both)
        sq = each(lambda bo: bo[t:], both)
    x0 = each(lambda p, s: p + _dot(p, stack(s)), pw, sq)
    m1 = each(lambda x, a: _dot(x, stack(jnp.where(in32_off16, a, 0.0))), x0, a_ab)
    x1 = each(lambda x, m: x + _dot(m, stack(x)), x0, m1)
    m2 = each(lambda x, a: _dot(x, stack(jnp.where(same32, 0.0, a))), x1, a_ab)
    x2 = each(lambda x, m: x + _dot(m, stack(x)), x1, m2)

    s0 = [s_ref[d, q] for d, q in chains]
    from_state = each(_dot_nt, lhs, s0)
    u = each(lambda x, fs, fv: _dot(x, stack(fs[:t] + fv[:t])), x2, from_state, from_v)
    y = each(lambda fs, fv, a, u_: fs[t:] + fv[t:] + _dot(a, stack(u_)), from_state, from_v, a_rb, u)
    upd = each(lambda u_, v_, bk_: _dot_tn(cat(u_, v_), bk_), u, v, bk)
    for (d, q), y_, s_, up, w in zip(chains, y, s0, upd, w_total):
        refs[d][6][0, :, sl(q)] = y_
        s_ref[d, q] = (s_ + jnp.where(same_head, up, 0.0)) * w


def _cumsum(tri, lw):
    acc = None
    rem = lw
    for _ in range(3):
        part = rem.astype(BF16)
        term = jnp.dot(tri, part, preferred_element_type=F32)
        acc = term if acc is None else acc + term
        rem = rem - part.astype(F32)
    return acc


def _rwkv_scan(r, v, kkn, lw, kd, bd):
    b, l, c = r.shape
    nc = l // CHUNK
    fwd = pl.BlockSpec((1, CHUNK, c), lambda bi, i: (bi, i, 0))
    bwd = pl.BlockSpec((1, CHUNK, c), lambda bi, i: (bi, nc - 1 - i, 0))
    fwd_d = pl.BlockSpec((1, 1, CHUNK, c), lambda bi, i: (0, bi, i, 0))
    bwd_d = pl.BlockSpec((1, 1, CHUNK, c), lambda bi, i: (1, bi, nc - 1 - i, 0))
    return pl.pallas_call(
        _rwkv_scan_kernel,
        grid=(b, nc),
        in_specs=[fwd, fwd, fwd, bwd, bwd, bwd, fwd_d, fwd_d, fwd_d, bwd_d, bwd_d, bwd_d],
        out_specs=[fwd, bwd],
        out_shape=[jax.ShapeDtypeStruct((b, l, c), F32)] * 2,
        scratch_shapes=[pltpu.VMEM((2, c // QUAD, QUAD, QUAD), F32)],
        compiler_params=_cparams(("parallel", "arbitrary")),
        name="rwkv_scan",
    )(r, v, kkn, r, v, kkn, lw, kd, bd, lw, kd, bd)


def _rwkv_post_kernel(yf_ref, yb_ref, bonus_ref, g_ref, lnw_ref, lnb_ref, ones_ref, o_ref):
    ones_bd = ones_ref[...]
    y = yf_ref[0] + yb_ref[0]
    mean = _head_sum(y, ones_bd) * (1.0 / HEAD_DIM)
    yc = y - mean
    var = _head_sum(yc * yc, ones_bd) * (1.0 / HEAD_DIM)
    yn = yc * lax.rsqrt(var + GN_EPS) * lnw_ref[...] + lnb_ref[...]
    o_ref[0] = ((yn + bonus_ref[0]) * g_ref[0]).astype(BF16)


def _rwkv_post(yf, yb, bonus, g, ln_w, ln_b, ones_bd):
    b, l, c = yf.shape
    tl = PREP_TILE
    tok = pl.BlockSpec((1, tl, c), lambda bi, i: (bi, i, 0))
    row = pl.BlockSpec((1, c), lambda bi, i: (0, 0))
    return pl.pallas_call(
        _rwkv_post_kernel,
        grid=(b, l // tl),
        in_specs=[tok, tok, tok, tok, row, row, pl.BlockSpec(ones_bd.shape, lambda bi, i: (0, 0))],
        out_specs=tok,
        out_shape=jax.ShapeDtypeStruct((b, l, c), BF16),
        compiler_params=_cparams(("parallel", "parallel")),
        name="rwkv_post",
    )(yf, yb, bonus, g, ln_w, ln_b, ones_bd)


def _s5_scan_kernel(u_ref, bdense_ref, cdense_ref, lam_ref, o_ref, st_ref, s_ref):
    d = pl.program_id(0)
    tl = u_ref.shape[0]
    ns = S5_STATES

    @pl.when(pl.program_id(1) == 0)
    def _():
        s_ref[...] = jnp.zeros_like(s_ref)

    rows = tl * S5_SEQ_PAD
    hs = ns // S5_SPLIT
    hc = S5_DIM // S5_SPLIT
    u = u_ref[...].reshape(rows, S5_DIM)
    for h in range(S5_SPLIT):
        bu = _dot(u[:, h * hc:(h + 1) * hc], bdense_ref[0, h])
        st_ref[:, :, h * hs:(h + 1) * hs] = bu[:, :hs].reshape(tl, S5_SEQ_PAD, hs)
        st_ref[:, :, ns + h * hs:ns + (h + 1) * hs] = bu[:, hs:].reshape(tl, S5_SEQ_PAD, hs)
    lam_re = lam_ref[0, 0]
    lam_im = lam_ref[0, 1]

    def step(j, carry):
        s_re, s_im = carry
        tt = jnp.where(d == 0, j, tl - 1 - j)
        n_re = lam_re * s_re - lam_im * s_im + st_ref[tt, :, :ns]
        n_im = lam_re * s_im + lam_im * s_re + st_ref[tt, :, ns:]
        st_ref[tt, :, :ns] = n_re
        st_ref[tt, :, ns:] = n_im
        return n_re, n_im

    s_re, s_im = lax.fori_loop(0, tl, step, (s_ref[0], s_ref[1]))
    s_ref[0] = s_re
    s_ref[1] = s_im
    for h in range(S5_SPLIT):
        st_re = st_ref[:, :, h * hs:(h + 1) * hs].reshape(rows, hs)
        st_im = st_ref[:, :, ns + h * hs:ns + (h + 1) * hs].reshape(rows, hs)
        y = _dot(st_re, cdense_ref[h, :hs]) + _dot(st_im, cdense_ref[h, hs:])
        o_ref[0, :, :, h * hc:(h + 1) * hc] = y.reshape(tl, S5_SEQ_PAD, hc)


def _s5_scan(u_t, bdense, cdense, lam):
    l = u_t.shape[0]
    tl = S5_TILE
    nt = l // tl
    pos = lambda dd, i: i + dd * (nt - 1 - 2 * i)
    return pl.pallas_call(
        _s5_scan_kernel,
        grid=(2, nt),
        in_specs=[
            pl.BlockSpec((tl, S5_SEQ_PAD, S5_DIM), lambda dd, i: (pos(dd, i), 0, 0)),
            pl.BlockSpec((1,) + bdense.shape[1:], lambda dd, i: (dd, 0, 0, 0)),
            pl.BlockSpec(cdense.shape, lambda dd, i: (0, 0, 0)),
            pl.BlockSpec((1, 2, S5_SEQ_PAD, S5_STATES), lambda dd, i: (dd, 0, 0, 0)),
        ],
        out_specs=pl.BlockSpec((1, tl, S5_SEQ_PAD, S5_DIM), lambda dd, i: (dd, pos(dd, i), 0, 0)),
        out_shape=jax.ShapeDtypeStruct((2, l, S5_SEQ_PAD, S5_DIM), F32),
        scratch_shapes=[pltpu.VMEM((tl, S5_SEQ_PAD, 2 * S5_STATES), F32),
                        pltpu.VMEM((2, S5_SEQ_PAD, S5_STATES), F32)],
        compiler_params=_cparams(("parallel", "arbitrary")),
        name="s5_scan",
    )(u_t, bdense, cdense, lam)


def _s5_post_kernel(yf_ref, yb_ref, u_ref, dskip_ref, wglu_ref, bglu_ref, o_ref):
    tl = u_ref.shape[0]
    rows = tl * S5_SEQ_PAD
    u = u_ref[...].reshape(rows, S5_DIM)
    y = (yf_ref[0] + yb_ref[0]).reshape(rows, S5_DIM) + dskip_ref[...] * u
    y = 0.5 * y * (1.0 + jnp.tanh(math.sqrt(2.0 / math.pi) * (y + 0.044715 * (y * y * y))))
    gate = _sigmoid(_dot(y, wglu_ref[...]) + bglu_ref[...])
    o_ref[...] = (y * gate).astype(BF16).reshape(tl, S5_SEQ_PAD, S5_DIM)


def _s5_post(y2, u_t, d_skip, w_glu, b_glu):
    l = u_t.shape[0]
    tl = S5_TILE
    tok = pl.BlockSpec((tl, S5_SEQ_PAD, S5_DIM), lambda i: (i, 0, 0))
    row = pl.BlockSpec((1, S5_DIM), lambda i: (0, 0))
    return pl.pallas_call(
        _s5_post_kernel,
        grid=(l // tl,),
        in_specs=[
            pl.BlockSpec((1, tl, S5_SEQ_PAD, S5_DIM), lambda i: (0, i, 0, 0)),
            pl.BlockSpec((1, tl, S5_SEQ_PAD, S5_DIM), lambda i: (1, i, 0, 0)),
            tok, row,
            pl.BlockSpec((S5_DIM, S5_DIM), lambda i: (0, 0)),
            row,
        ],
        out_specs=tok,
        out_shape=jax.ShapeDtypeStruct((l, S5_SEQ_PAD, S5_DIM), BF16),
        compiler_params=_cparams(("parallel",)),
        name="s5_post",
    )(y2, y2, u_t, d_skip, w_glu, b_glu)


def _s5_params(a_re, a_im, log_step, b_re, b_im, c_re, c_im):
    dt = jnp.exp(log_step)[..., None]
    z_re, z_im = a_re * dt, a_im * dt
    mag = jnp.exp(z_re)
    lam_re, lam_im = mag * jnp.cos(z_im), mag * jnp.sin(z_im)
    den = a_re * a_re + a_im * a_im
    q_re = ((lam_re - 1.0) * a_re + lam_im * a_im) / den
    q_im = (lam_im * a_re - (lam_re - 1.0) * a_im) / den
    bb_re = q_re[..., None] * b_re - q_im[..., None] * b_im
    bb_im = q_re[..., None] * b_im + q_im[..., None] * b_re
    gb = S5_GROUPS // S5_SPLIT
    eye = jnp.eye(gb, dtype=F32)

    def dense_b(x):
        x = x.reshape(2, S5_SPLIT, gb, S5_STATE, S5_CH)
        return jnp.einsum('dhgpc,gk->dhgckp', x, eye).reshape(2, S5_SPLIT, gb * S5_CH, gb * S5_STATE)

    def dense_c(x):
        x = x.reshape(S5_SPLIT, gb, S5_CH, S5_STATE)
        return jnp.einsum('hgcp,gk->hgpkc', x, eye).reshape(S5_SPLIT, gb * S5_STATE, gb * S5_CH)

    bdense = jnp.concatenate([dense_b(bb_re), dense_b(bb_im)], axis=-1).astype(BF16)
    cdense = jnp.concatenate([dense_c(c_re), -dense_c(c_im)], axis=1).astype(BF16)
    lam = jnp.stack([lam_re.reshape(2, S5_STATES), lam_im.reshape(2, S5_STATES)], axis=1)
    lam = jnp.broadcast_to(lam[:, :, None, :], (2, 2, S5_SEQ_PAD, S5_STATES))
    return bdense, cdense, lam


MERGE_TILE = 256


def _merge_kernel(x_ref, yr_ref, ys_ref, gate_ref, pr_ref, ps_ref, wo_ref, o_ref):
    y_rwkv = jnp.dot(yr_ref[...], pr_ref[...], preferred_element_type=F32)
    y_s5 = jnp.dot(ys_ref[...], ps_ref[...], preferred_element_type=F32)
    gates = _sigmoid(gate_ref[...])
    merged = gates[:, :D_MODEL] * y_rwkv + gates[:, D_MODEL:] * y_s5
    o_ref[...] = x_ref[...] + _dot(merged, wo_ref[...])


def _merge(x, yr, ys, proj, proj_rwkv, proj_s5, w_out):
    n, d = x.shape
    tm = MERGE_TILE
    assert PROJ_GATE_OFF == 2 * d and proj.shape[1] == PROJ_GATE_OFF + 2 * d
    tok = lambda c: pl.BlockSpec((tm, c), lambda i: (i, 0))
    full = lambda a: pl.BlockSpec(a.shape, lambda i: (0, 0))
    return pl.pallas_call(
        _merge_kernel,
        grid=(n // tm,),
        in_specs=[tok(d), tok(RWKV_DIM), tok(S5_DIM), pl.BlockSpec((tm, 2 * d), lambda i: (i, 1)),
                  full(proj_rwkv), full(proj_s5), full(w_out)],
        out_specs=tok(d),
        out_shape=jax.ShapeDtypeStruct((n, d), F32),
        compiler_params=_cparams(("parallel",)),
        name="merge",
    )(x, yr, ys, proj, proj_rwkv, proj_s5, w_out)


def _pad_ff(w, axis):
    pad = (-w.shape[axis]) % FF_TILE
    widths = [(0, 0)] * w.ndim
    widths[axis] = (0, pad)
    return jnp.pad(w, widths).astype(BF16)


def _lora_cat(w):
    z = jnp.zeros_like(w[0])
    return jnp.concatenate([jnp.concatenate([w[0], z], axis=1),
                            jnp.concatenate([z, w[1]], axis=1)], axis=0).astype(BF16)


def _forward(x_a, x_b, norm_ffn1, ffn1_w_gate, ffn1_w_up, ffn1_w_down, norm_mix, w_in, shift_mu,
             rwkv_w0, rwkv_w2, rwkv_a0, rwkv_a2, rwkv_g2, rwkv_k_k, rwkv_k_a, rwkv_r_k,
             rwkv_ln_w, rwkv_ln_b, s5_a_re, s5_a_im, s5_log_step, s5_b_re, s5_b_im,
             s5_c_re, s5_c_im, s5_d, s5_w_glu, s5_b_glu, proj_rwkv, proj_s5, w_out,
             norm_ffn2, ffn2_w_gate, ffn2_w_up, ffn2_w_down, norm_final):
    l, d = x_a.shape[1:]
    b = x_a.shape[0] + x_b.shape[0]
    n_a, n_b = x_a.shape[0] * l, x_b.shape[0] * l
    n = n_a + n_b
    row = lambda p: p.reshape(1, -1)

    (x1,) = _ffn([x_a.reshape(n_a, d), x_b.reshape(n_b, d)], row(norm_ffn1), _pad_ff(ffn1_w_gate, 1),
                 _pad_ff(ffn1_w_up, 1), _pad_ff(ffn1_w_down, 0), row(norm_final),
                 out_rows=[n], final_norm=False)

    assert w_in.shape[1] == SHIFT_COLS + S5_DIM + 2 * d
    w_proj = jnp.concatenate(
        [w_in[:, :SHIFT_COLS], jnp.zeros((d, PROJ_U_OFF - SHIFT_COLS), w_in.dtype), w_in[:, SHIFT_COLS:]],
        axis=1).astype(BF16)
    proj = _norm_matmul(x1, row(norm_mix), w_proj, PROJ_TOK_TILE, PROJ_COL_TILE, "proj_in")
    u = proj[:, PROJ_U_OFF:PROJ_U_OFF + S5_DIM]

    head_id = jnp.arange(RWKV_DIM) // HEAD_DIM
    ones_bd = (head_id[:, None] == head_id[None, :]).astype(BF16)
    r, v, kkn, lw, kd, bd, g, bonus = _rwkv_prep(
        proj.reshape(b, l, PROJ_COLS), row(shift_mu),
        _lora_cat(rwkv_w2), rwkv_w0.reshape(1, -1), _lora_cat(rwkv_a2), rwkv_a0.reshape(1, -1),
        rwkv_g2.astype(BF16), row(rwkv_k_k), row(rwkv_k_a), row(rwkv_r_k), ones_bd)
    yf, yb = _rwkv_scan(r, v, kkn, lw, kd, bd)
    yr = _rwkv_post(yf, yb, bonus, g, row(rwkv_ln_w), row(rwkv_ln_b), ones_bd).reshape(n, RWKV_DIM)

    u_t = jnp.transpose(u.reshape(b, l, S5_DIM), (1, 0, 2))
    u_t = jnp.pad(u_t, ((0, 0), (0, S5_SEQ_PAD - b), (0, 0)))
    bdense, cdense, lam = _s5_params(s5_a_re, s5_a_im, s5_log_step, s5_b_re, s5_b_im, s5_c_re, s5_c_im)
    ys2 = _s5_scan(u_t, bdense, cdense, lam)
    ys_t = _s5_post(ys2, u_t, row(s5_d), s5_w_glu.astype(BF16), row(s5_b_glu))
    ys = jnp.transpose(ys_t[:, :b], (1, 0, 2)).reshape(n, S5_DIM)

    x2 = _merge(x1, yr, ys, proj, proj_rwkv.astype(BF16), proj_s5.astype(BF16), w_out.astype(BF16))
    y_a, y_b = _ffn([x2], row(norm_ffn2), _pad_ff(ffn2_w_gate, 1), _pad_ff(ffn2_w_up, 1),
                    _pad_ff(ffn2_w_down, 0), row(norm_final), out_rows=[n_a, n_b], final_norm=True)
    return y_a.reshape(x_a.shape), y_b.reshape(x_b.shape)


def kernel(x_prompt, x_sample, norm_ffn1, ffn1_w_gate, ffn1_w_up, ffn1_w_down, norm_mix, w_in, shift_mu, rwkv_w0, rwkv_w2, rwkv_a0, rwkv_a2, rwkv_g2, rwkv_k_k, rwkv_k_a, rwkv_r_k, rwkv_ln_w, rwkv_ln_b, s5_a_re, s5_a_im, s5_log_step, s5_b_re, s5_b_im, s5_c_re, s5_c_im, s5_d, s5_w_glu, s5_b_glu, proj_rwkv, proj_s5, w_out, norm_ffn2, ffn2_w_gate, ffn2_w_up, ffn2_w_down, norm_final):
    layer = (norm_ffn1, ffn1_w_gate, ffn1_w_up, ffn1_w_down, norm_mix, w_in, shift_mu,
             rwkv_w0, rwkv_w2, rwkv_a0, rwkv_a2, rwkv_g2, rwkv_k_k, rwkv_k_a, rwkv_r_k,
             rwkv_ln_w, rwkv_ln_b, s5_a_re, s5_a_im, s5_log_step, s5_b_re, s5_b_im,
             s5_c_re, s5_c_im, s5_d, s5_w_glu, s5_b_glu, proj_rwkv, proj_s5, w_out,
             norm_ffn2, ffn2_w_gate, ffn2_w_up, ffn2_w_down)
    assert all(p.shape[0] == 1 for p in layer), "single-layer block"
    assert x_prompt.shape[1:] == x_sample.shape[1:]
    return _forward(x_prompt, x_sample, *[p[0] for p in layer], norm_final)
```

```python
import functools
import math

import jax
import jax.numpy as jnp
from jax import lax
from jax.experimental import pallas as pl
from jax.experimental.pallas import tpu as pltpu

F32 = jnp.float32
BF16 = jnp.bfloat16

D_MODEL = 2048
D_FF = 5504
HEADS = 16
HEAD_DIM = 64
RWKV_DIM = HEADS * HEAD_DIM
LORA = 64
GATE_LORA = 128
S5_GROUPS = 32
S5_CH = 16
S5_DIM = S5_GROUPS * S5_CH
S5_STATE = 64
S5_STATES = S5_GROUPS * S5_STATE
SHIFT_COLS = 3 * RWKV_DIM + 4 * LORA + GATE_LORA
RMS_EPS = 1e-6
GN_EPS = 64e-5
KK_EPS = 1e-12

LANES = 128
SUBLANES = 8
VMEM_LIMIT = 56 * 1024 * 1024

FF_TILE = 512
TOK_TILE = 512
CHUNK = 64
QUAD = 4 * HEAD_DIM
SCAN_CHUNKS = 2
S5_SEQ_PAD = SUBLANES
S5_TILE = 64
S5_SPLIT = 2
PROJ_TOK_TILE = 1024
PROJ_COL_TILE = 1024
PROJ_U_OFF = 3584
PROJ_GATE_OFF = 4096
PROJ_COLS = 8192


def _cparams(sem):
    return pltpu.CompilerParams(dimension_semantics=sem, vmem_limit_bytes=VMEM_LIMIT)


def _dot(a, b):
    return jnp.dot(a.astype(BF16), b.astype(BF16), preferred_element_type=F32)


def _dot_nt(a, b):
    return lax.dot_general(a.astype(BF16), b.astype(BF16), (((1,), (1,)), ((), ())),
                           preferred_element_type=F32)


def _dot_tn(a, b):
    return lax.dot_general(a.astype(BF16), b.astype(BF16), (((0,), (0,)), ((), ())),
                           preferred_element_type=F32)


def _split_dot(x, w, pieces):
    acc = None
    rem = x
    for _ in range(pieces):
        part = rem.astype(BF16)
        term = jnp.dot(part, w, preferred_element_type=F32)
        acc = term if acc is None else acc + term
        rem = rem - part.astype(F32)
    return acc


def _sigmoid(x):
    return 1.0 / (1.0 + jnp.exp(-x))


def _rms(x, g):
    ms = jnp.mean(x * x, axis=-1, keepdims=True)
    return x * lax.rsqrt(ms + RMS_EPS) * g


def _ffn_kernel(*refs, n_x, n_out, first_tiles, final_norm):
    x_refs = refs[:n_x]
    g_ref, wg_ref, wu_ref, wd_ref, gf_ref = refs[n_x:n_x + 5]
    o_refs = refs[n_x + 5:n_x + 5 + n_out]
    xn_ref, acc_ref = refs[n_x + 5 + n_out:]
    i = pl.program_id(0)
    j = pl.program_id(1)
    in_first = i < first_tiles
    parts = [(in_first, x_refs[0], o_refs[0]), (jnp.logical_not(in_first), x_refs[-1], o_refs[-1])]
    if n_x == 1 and n_out == 1:
        parts = [(True, x_refs[0], o_refs[0])]

    for cond, x_ref, _ in parts:
        @pl.when((j == 0) & cond)
        def _(x_ref=x_ref):
            xn_ref[...] = _rms(x_ref[...], g_ref[...]).astype(BF16)
            acc_ref[...] = jnp.zeros_like(acc_ref)

    xn = xn_ref[...]
    hg = jnp.dot(xn, wg_ref[...], preferred_element_type=F32)
    hu = jnp.dot(xn, wu_ref[...], preferred_element_type=F32)
    act = (hg * _sigmoid(hg)) * hu
    acc_ref[...] += jnp.dot(act.astype(BF16), wd_ref[...], preferred_element_type=F32)

    for cond, x_ref, o_ref in parts:
        @pl.when((j == pl.num_programs(1) - 1) & cond)
        def _(x_ref=x_ref, o_ref=o_ref):
            y = x_ref[...] + 0.5 * acc_ref[...]
            o_ref[...] = _rms(y, gf_ref[...]) if final_norm else y


def _ffn(xs, g, wg, wu, wd, gf, *, out_rows, final_norm):
    d = xs[0].shape[1]
    f = wg.shape[1]
    tm, tf = TOK_TILE, FF_TILE
    n = sum(x.shape[0] for x in xs)
    assert n == sum(out_rows) and all(r % tm == 0 for r in out_rows)
    assert all(x.shape[0] % tm == 0 for x in xs)
    first_rows = xs[0].shape[0] if len(xs) == 2 else out_rows[0]
    if len(xs) == 2 and len(out_rows) == 2:
        assert xs[0].shape[0] == out_rows[0]
    first_tiles = first_rows // tm

    def split_specs(count):
        if count == 1:
            return [pl.BlockSpec((tm, d), lambda i, j: (i, 0))]
        return [pl.BlockSpec((tm, d), lambda i, j: (jnp.minimum(i, first_tiles - 1), 0)),
                pl.BlockSpec((tm, d), lambda i, j: (jnp.maximum(i - first_tiles, 0), 0))]

    outs = pl.pallas_call(
        functools.partial(_ffn_kernel, n_x=len(xs), n_out=len(out_rows), first_tiles=first_tiles,
                          final_norm=final_norm),
        grid=(n // tm, f // tf),
        in_specs=split_specs(len(xs)) + [
            pl.BlockSpec((1, d), lambda i, j: (0, 0)),
            pl.BlockSpec((d, tf), lambda i, j: (0, j)),
            pl.BlockSpec((d, tf), lambda i, j: (0, j)),
            pl.BlockSpec((tf, d), lambda i, j: (j, 0)),
            pl.BlockSpec((1, d), lambda i, j: (0, 0)),
        ],
        out_specs=split_specs(len(out_rows)),
        out_shape=[jax.ShapeDtypeStruct((r, d), F32) for r in out_rows],
        scratch_shapes=[pltpu.VMEM((tm, d), BF16), pltpu.VMEM((tm, d), F32)],
        compiler_params=_cparams(("arbitrary", "arbitrary")),
        name="ffn_final" if final_norm else "ffn",
    )(*xs, g, wg, wu, wd, gf)
    return outs


def _norm_matmul_kernel(x_ref, g_ref, w_ref, o_ref, xn_ref):
    @pl.when(pl.program_id(1) == 0)
    def _():
        xn_ref[...] = _rms(x_ref[...], g_ref[...]).astype(BF16)

    o_ref[...] = jnp.dot(xn_ref[...], w_ref[...], preferred_element_type=F32)


def _norm_matmul(x, g, w, tm, tn, name):
    n, d = x.shape
    c = w.shape[1]
    return pl.pallas_call(
        _norm_matmul_kernel,
        grid=(n // tm, c // tn),
        in_specs=[
            pl.BlockSpec((tm, d), lambda i, j: (i, 0)),
            pl.BlockSpec((1, d), lambda i, j: (0, 0)),
            pl.BlockSpec((d, tn), lambda i, j: (0, j)),
        ],
        out_specs=pl.BlockSpec((tm, tn), lambda i, j: (i, j)),
        out_shape=jax.ShapeDtypeStruct((n, c), F32),
        scratch_shapes=[pltpu.VMEM((tm, d), BF16)],
        compiler_params=_cparams(("parallel", "arbitrary")),
        name=name,
    )(x, g, w)


PREP_TILE = 256


def _head_sum(x, ones_quad):
    return jnp.concatenate([_split_dot(x[:, c:c + QUAD], ones_quad, 2)
                            for c in range(0, x.shape[1], QUAD)], axis=1)


def _rwkv_prep_kernel(main_ref, prev_ref, next_ref, mu_ref, w2_ref, w0_ref, a2_ref, a0_ref, g2_ref,
                      kk_ref, ka_ref, rk_ref, ones_ref,
                      r_out, v_out, kkn_out, lw_out, kd_out, bd_out, g_out, bonus_out):
    i = pl.program_id(1)
    tl = main_ref.shape[1]
    x = main_ref[0]
    row = lax.broadcasted_iota(jnp.int32, (tl, 1), 0)
    before = jnp.where(i == 0, 0.0, prev_ref[0, SUBLANES - 1:SUBLANES, :])
    after = jnp.where(i == pl.num_programs(1) - 1, 0.0, next_ref[0, 0:1, :])
    prev = jnp.where(row == 0, before, pltpu.roll(x, 1, 0))
    nxt = jnp.where(row == tl - 1, after, pltpu.roll(x, tl - 1, 0))
    p = x + (0.5 * (prev + nxt) - x) * mu_ref[...]

    c0 = 3 * RWKV_DIM
    r = p[:, 0:RWKV_DIM]
    k = p[:, RWKV_DIM:2 * RWKV_DIM]
    v = p[:, 2 * RWKV_DIM:c0]
    wlow = jnp.tanh(p[:, c0:c0 + 2 * LORA])
    alow = p[:, c0 + 2 * LORA:c0 + 4 * LORA]
    glow = p[:, c0 + 4 * LORA:c0 + 4 * LORA + GATE_LORA]
    ones_bd = ones_ref[...]

    wpre = _dot(wlow, w2_ref[...]) + w0_ref[...]
    lw = (-math.exp(-0.5)) * _sigmoid(wpre)
    a = _sigmoid(_dot(alow, a2_ref[...]) + a0_ref[...])

    kk = k * kk_ref[...]
    kkn = kk * lax.rsqrt(_head_sum(kk * kk, ones_bd) + KK_EPS)
    ka = ka_ref[...]
    ksum = None
    for d in range(2):
        a_d = a[:, d * RWKV_DIM:(d + 1) * RWKV_DIM]
        kd = k * (1.0 + (a_d - 1.0) * ka)
        kd_out[d, 0] = kd
        bd_out[d, 0] = kkn * a_d
        lw_out[d, 0] = lw[:, d * RWKV_DIM:(d + 1) * RWKV_DIM]
        ksum = kd if ksum is None else ksum + kd
    r_out[0] = r
    v_out[0] = v
    kkn_out[0] = kkn
    g_out[0] = _dot(_sigmoid(glow), g2_ref[...])
    bonus_out[0] = _head_sum(r * ksum * rk_ref[...], ones_bd) * v


def _rwkv_prep(pshift, mu, w2cat, w0cat, a2cat, a0cat, g2, k_k, k_a, r_k, ones_bd):
    b, l, _ = pshift.shape
    c = SHIFT_COLS
    tl = PREP_TILE
    nt = l // tl
    hb = tl // SUBLANES
    nhb = l // SUBLANES
    full = lambda shape: pl.BlockSpec(shape, lambda bi, i: (0,) * len(shape))
    tok = pl.BlockSpec((1, tl, RWKV_DIM), lambda bi, i: (bi, i, 0))
    tok2 = pl.BlockSpec((2, 1, tl, RWKV_DIM), lambda bi, i: (0, bi, i, 0))
    s1 = jax.ShapeDtypeStruct((b, l, RWKV_DIM), F32)
    s2 = jax.ShapeDtypeStruct((2, b, l, RWKV_DIM), F32)
    return pl.pallas_call(
        _rwkv_prep_kernel,
        grid=(b, nt),
        in_specs=[
            pl.BlockSpec((1, tl, c), lambda bi, i: (bi, i, 0)),
            pl.BlockSpec((1, SUBLANES, c), lambda bi, i: (bi, jnp.maximum(i * hb - 1, 0), 0)),
            pl.BlockSpec((1, SUBLANES, c), lambda bi, i: (bi, jnp.minimum((i + 1) * hb, nhb - 1), 0)),
            full((1, c)),
            full(w2cat.shape), full(w0cat.shape), full(a2cat.shape), full(a0cat.shape), full(g2.shape),
            full((1, RWKV_DIM)), full((1, RWKV_DIM)), full((1, RWKV_DIM)), full(ones_bd.shape),
        ],
        out_specs=[tok, tok, tok, tok2, tok2, tok2, tok, tok],
        out_shape=[s1, s1, s1, s2, s2, s2, s1, s1],
        compiler_params=_cparams(("parallel", "arbitrary")),
        name="rwkv_prep",
    )(pshift, pshift, pshift, mu, w2cat, w0cat, a2cat, a0cat, g2, k_k, k_a, r_k, ones_bd)


def _stack_heads(x, head_masks):
    return jnp.concatenate([jnp.where(m, x, 0.0) for m in head_masks], axis=0)


def _rwkv_scan_kernel(rf_ref, vf_ref, kkf_ref, rb_ref, vb_ref, kkb_ref,
                      lwf_ref, kdf_ref, bdf_ref, lwb_ref, kdb_ref, bdb_ref,
                      of_ref, ob_ref, s_ref):
    t = CHUNK

    @pl.when(pl.program_id(1) == 0)
    def _():
        s_ref[...] = jnp.zeros_like(s_ref)

    row = lax.broadcasted_iota(jnp.int32, (t, QUAD), 0)
    lane = lax.broadcasted_iota(jnp.int32, (t, QUAD), 1)
    col = lane & (HEAD_DIM - 1)
    same16 = (row >> 4) == (col >> 4)
    same32 = (row >> 5) == (col >> 5)
    in32_off16 = same32 & jnp.logical_not(same16)
    eye = jnp.where(col == row, 1.0, 0.0)
    head_masks = [(lane >> 6) == h for h in range(4)]
    trow = lax.broadcasted_iota(jnp.int32, (t, t), 0)
    tcol = lax.broadcasted_iota(jnp.int32, (t, t), 1)
    strict = (col < row, col > row)
    incl = (col <= row, col >= row)
    tri = (jnp.where(tcol <= trow, 1.0, 0.0).astype(BF16),
           jnp.where(tcol >= trow, 1.0, 0.0).astype(BF16))
    qrow = lax.broadcasted_iota(jnp.int32, (QUAD, QUAD), 0) >> 6
    qcol = lax.broadcasted_iota(jnp.int32, (QUAD, QUAD), 1) >> 6
    same_head = qrow == qcol
    stack = lambda x: _stack_heads(x, head_masks)
    cat = lambda a, b: jnp.concatenate([a, b], axis=0)

    refs = ((rf_ref, vf_ref, kkf_ref, lwf_ref, kdf_ref, bdf_ref, of_ref),
            (rb_ref, vb_ref, kkb_ref, lwb_ref, kdb_ref, bdb_ref, ob_ref))
    n_sub = rf_ref.shape[1] // t
    chains = [(d, q, c) for c in range(n_sub) for q in range(RWKV_DIM // QUAD) for d in range(2)]
    each = lambda fn, *lists: [fn(*args) for args in zip(*lists)]
    sl = lambda q: slice(q * QUAD, (q + 1) * QUAD)

    def rows(d, c):
        c = c if d == 0 else n_sub - 1 - c
        return slice(c * t, (c + 1) * t)

    lw = [refs[d][3][0, 0, rows(d, c), sl(q)] for d, q, c in chains]
    cum = [_cumsum(tri[d], x) for (d, _, _), x in zip(chains, lw)]
    e_out = each(lambda c_: jnp.exp(-c_), cum)
    at = [-refs[d][2][0, rows(d, c), sl(q)] * jnp.exp(c_ - x) for (d, q, c), c_, x in zip(chains, cum, lw)]
    rt = [refs[d][0][0, rows(d, c), sl(q)] * jnp.exp(c_) for (d, q, c), c_ in zip(chains, cum)]
    bt = [refs[d][5][0, 0, rows(d, c), sl(q)] * e for (d, q, c), e in zip(chains, e_out)]
    kt = [refs[d][4][0, 0, rows(d, c), sl(q)] * e for (d, q, c), e in zip(chains, e_out)]
    v = [refs[d][1][0, rows(d, c), sl(q)] for d, q, c in chains]
    w_total = each(lambda x: jnp.exp(jnp.sum(x, axis=0, keepdims=True)), lw)
    lhs = each(cat, at, rt)
    bk = each(cat, bt, kt)

    gram = each(lambda l_, b_, k_: _dot_nt(l_, cat(stack(b_), stack(k_))), lhs, bt, kt)
    a_ab = [jnp.where(strict[d], g[:t, :QUAD], 0.0) for (d, _, _), g in zip(chains, gram)]
    a_ak = [jnp.where(strict[d], g[:t, QUAD:], 0.0) for (d, _, _), g in zip(chains, gram)]
    a_rb = [jnp.where(incl[d], g[t:, :QUAD], 0.0) for (d, _, _), g in zip(chains, gram)]
    a_rk = [jnp.where(incl[d], g[t:, QUAD:], 0.0) for (d, _, _), g in zip(chains, gram)]
    from_v = each(lambda p, q_, v_: _dot(cat(p, q_), stack(v_)), a_ak, a_rk, v)

    a_d = each(lambda a: jnp.where(same16, a, 0.0), a_ab)
    pw = each(lambda a: eye + a, a_d)
    sq = each(lambda a: _dot(a, stack(a)), a_d)
    for _ in range(2):
        both = each(lambda p, s: _dot(cat(p, s), stack(s)), pw, sq)
        pw = each(lambda p, bo: p + bo[:t], pw, both)
        sq = each(lambda bo: bo[t:], both)
    x0 = each(lambda p, s: p + _dot(p, stack(s)), pw, sq)
    m1 = each(lambda x, a: _dot(x, stack(jnp.where(in32_off16, a, 0.0))), x0, a_ab)
    x1 = each(lambda x, m: x + _dot(m, stack(x)), x0, m1)
    m2 = each(lambda x, a: _dot(x, stack(jnp.where(same32, 0.0, a))), x1, a_ab)
    x2 = each(lambda x, m: x + _dot(m, stack(x)), x1, m2)

    per_chunk = len(chains) // n_sub
    state = [s_ref[d, q] for d, q, _ in chains[:per_chunk]]
    for c in range(n_sub):
        pick = lambda xs: xs[c * per_chunk:(c + 1) * per_chunk]
        from_state = each(_dot_nt, pick(lhs), state)
        u = each(lambda x, fs, fv: _dot(x, stack(fs[:t] + fv[:t])), pick(x2), from_state, pick(from_v))
        y = each(lambda fs, fv, a, u_: fs[t:] + fv[t:] + _dot(a, stack(u_)),
                 from_state, pick(from_v), pick(a_rb), u)
        upd = each(lambda u_, v_, bk_: _dot_tn(cat(u_, v_), bk_), u, pick(v), pick(bk))
        for (d, q, _), y_ in zip(pick(chains), y):
            refs[d][6][0, rows(d, c), sl(q)] = y_
        state = each(lambda s_, up, w: (s_ + jnp.where(same_head, up, 0.0)) * w, state, upd, pick(w_total))
    for (d, q, _), s_ in zip(chains[:per_chunk], state):
        s_ref[d, q] = s_


def _cumsum(tri, lw):
    acc = None
    rem = lw
    for _ in range(3):
        part = rem.astype(BF16)
        term = jnp.dot(tri, part, preferred_element_type=F32)
        acc = term if acc is None else acc + term
        rem = rem - part.astype(F32)
    return acc


def _rwkv_scan(r, v, kkn, lw, kd, bd):
    b, l, c = r.shape
    tl = SCAN_CHUNKS * CHUNK
    nc = l // tl
    fwd = pl.BlockSpec((1, tl, c), lambda bi, i: (bi, i, 0))
    bwd = pl.BlockSpec((1, tl, c), lambda bi, i: (bi, nc - 1 - i, 0))
    fwd_d = pl.BlockSpec((1, 1, tl, c), lambda bi, i: (0, bi, i, 0))
    bwd_d = pl.BlockSpec((1, 1, tl, c), lambda bi, i: (1, bi, nc - 1 - i, 0))
    return pl.pallas_call(
        _rwkv_scan_kernel,
        grid=(b, nc),
        in_specs=[fwd, fwd, fwd, bwd, bwd, bwd, fwd_d, fwd_d, fwd_d, bwd_d, bwd_d, bwd_d],
        out_specs=[fwd, bwd],
        out_shape=[jax.ShapeDtypeStruct((b, l, c), F32)] * 2,
        scratch_shapes=[pltpu.VMEM((2, c // QUAD, QUAD, QUAD), F32)],
        compiler_params=_cparams(("parallel", "arbitrary")),
        name="rwkv_scan",
    )(r, v, kkn, r, v, kkn, lw, kd, bd, lw, kd, bd)


def _rwkv_post_kernel(yf_ref, yb_ref, bonus_ref, g_ref, lnw_ref, lnb_ref, ones_ref, o_ref):
    ones_bd = ones_ref[...]
    y = yf_ref[0] + yb_ref[0]
    mean = _head_sum(y, ones_bd) * (1.0 / HEAD_DIM)
    yc = y - mean
    var = _head_sum(yc * yc, ones_bd) * (1.0 / HEAD_DIM)
    yn = yc * lax.rsqrt(var + GN_EPS) * lnw_ref[...] + lnb_ref[...]
    o_ref[0] = ((yn + bonus_ref[0]) * g_ref[0]).astype(BF16)


def _rwkv_post(yf, yb, bonus, g, ln_w, ln_b, ones_bd):
    b, l, c = yf.shape
    tl = PREP_TILE
    tok = pl.BlockSpec((1, tl, c), lambda bi, i: (bi, i, 0))
    row = pl.BlockSpec((1, c), lambda bi, i: (0, 0))
    return pl.pallas_call(
        _rwkv_post_kernel,
        grid=(b, l // tl),
        in_specs=[tok, tok, tok, tok, row, row, pl.BlockSpec(ones_bd.shape, lambda bi, i: (0, 0))],
        out_specs=tok,
        out_shape=jax.ShapeDtypeStruct((b, l, c), BF16),
        compiler_params=_cparams(("parallel", "parallel")),
        name="rwkv_post",
    )(yf, yb, bonus, g, ln_w, ln_b, ones_bd)


def _s5_scan_kernel(u_ref, bdense_ref, cdense_ref, lam_ref, o_ref, st_ref, s_ref):
    d = pl.program_id(0)
    tl = u_ref.shape[0]
    ns = S5_STATES

    @pl.when(pl.program_id(1) == 0)
    def _():
        s_ref[...] = jnp.zeros_like(s_ref)

    rows = tl * S5_SEQ_PAD
    hs = ns // S5_SPLIT
    hc = S5_DIM // S5_SPLIT
    u = u_ref[...].reshape(rows, S5_DIM)
    for h in range(S5_SPLIT):
        bu = _dot(u[:, h * hc:(h + 1) * hc], bdense_ref[0, h])
        st_ref[:, :, h * hs:(h + 1) * hs] = bu[:, :hs].reshape(tl, S5_SEQ_PAD, hs)
        st_ref[:, :, ns + h * hs:ns + (h + 1) * hs] = bu[:, hs:].reshape(tl, S5_SEQ_PAD, hs)
    lam_re = lam_ref[0, 0]
    lam_im = lam_ref[0, 1]

    def step(j, carry):
        s_re, s_im = carry
        tt = jnp.where(d == 0, j, tl - 1 - j)
        n_re = lam_re * s_re - lam_im * s_im + st_ref[tt, :, :ns]
        n_im = lam_re * s_im + lam_im * s_re + st_ref[tt, :, ns:]
        st_ref[tt, :, :ns] = n_re
        st_ref[tt, :, ns:] = n_im
        return n_re, n_im

    s_re, s_im = lax.fori_loop(0, tl, step, (s_ref[0], s_ref[1]))
    s_ref[0] = s_re
    s_ref[1] = s_im
    for h in range(S5_SPLIT):
        st_re = st_ref[:, :, h * hs:(h + 1) * hs].reshape(rows, hs)
        st_im = st_ref[:, :, ns + h * hs:ns + (h + 1) * hs].reshape(rows, hs)
        y = _dot(st_re, cdense_ref[h, :hs]) + _dot(st_im, cdense_ref[h, hs:])
        o_ref[0, :, :, h * hc:(h + 1) * hc] = y.reshape(tl, S5_SEQ_PAD, hc)


def _s5_scan(u_t, bdense, cdense, lam):
    l = u_t.shape[0]
    tl = S5_TILE
    nt = l // tl
    pos = lambda dd, i: i + dd * (nt - 1 - 2 * i)
    return pl.pallas_call(
        _s5_scan_kernel,
        grid=(2, nt),
        in_specs=[
            pl.BlockSpec((tl, S5_SEQ_PAD, S5_DIM), lambda dd, i: (pos(dd, i), 0, 0)),
            pl.BlockSpec((1,) + bdense.shape[1:], lambda dd, i: (dd, 0, 0, 0)),
            pl.BlockSpec(cdense.shape, lambda dd, i: (0, 0, 0)),
            pl.BlockSpec((1, 2, S5_SEQ_PAD, S5_STATES), lambda dd, i: (dd, 0, 0, 0)),
        ],
        out_specs=pl.BlockSpec((1, tl, S5_SEQ_PAD, S5_DIM), lambda dd, i: (dd, pos(dd, i), 0, 0)),
        out_shape=jax.ShapeDtypeStruct((2, l, S5_SEQ_PAD, S5_DIM), F32),
        scratch_shapes=[pltpu.VMEM((tl, S5_SEQ_PAD, 2 * S5_STATES), F32),
                        pltpu.VMEM((2, S5_SEQ_PAD, S5_STATES), F32)],
        compiler_params=_cparams(("parallel", "arbitrary")),
        name="s5_scan",
    )(u_t, bdense, cdense, lam)


def _s5_post_kernel(yf_ref, yb_ref, u_ref, dskip_ref, wglu_ref, bglu_ref, o_ref):
    tl = u_ref.shape[0]
    rows = tl * S5_SEQ_PAD
    u = u_ref[...].reshape(rows, S5_DIM)
    y = (yf_ref[0] + yb_ref[0]).reshape(rows, S5_DIM) + dskip_ref[...] * u
    y = 0.5 * y * (1.0 + jnp.tanh(math.sqrt(2.0 / math.pi) * (y + 0.044715 * (y * y * y))))
    gate = _sigmoid(_dot(y, wglu_ref[...]) + bglu_ref[...])
    o_ref[...] = (y * gate).astype(BF16).reshape(tl, S5_SEQ_PAD, S5_DIM)


def _s5_post(y2, u_t, d_skip, w_glu, b_glu):
    l = u_t.shape[0]
    tl = S5_TILE
    tok = pl.BlockSpec((tl, S5_SEQ_PAD, S5_DIM), lambda i: (i, 0, 0))
    row = pl.BlockSpec((1, S5_DIM), lambda i: (0, 0))
    return pl.pallas_call(
        _s5_post_kernel,
        grid=(l // tl,),
        in_specs=[
            pl.BlockSpec((1, tl, S5_SEQ_PAD, S5_DIM), lambda i: (0, i, 0, 0)),
            pl.BlockSpec((1, tl, S5_SEQ_PAD, S5_DIM), lambda i: (1, i, 0, 0)),
            tok, row,
            pl.BlockSpec((S5_DIM, S5_DIM), lambda i: (0, 0)),
            row,
        ],
        out_specs=tok,
        out_shape=jax.ShapeDtypeStruct((l, S5_SEQ_PAD, S5_DIM), BF16),
        compiler_params=_cparams(("parallel",)),
        name="s5_post",
    )(y2, y2, u_t, d_skip, w_glu, b_glu)


def _s5_params(a_re, a_im, log_step, b_re, b_im, c_re, c_im):
    dt = jnp.exp(log_step)[..., None]
    z_re, z_im = a_re * dt, a_im * dt
    mag = jnp.exp(z_re)
    lam_re, lam_im = mag * jnp.cos(z_im), mag * jnp.sin(z_im)
    den = a_re * a_re + a_im * a_im
    q_re = ((lam_re - 1.0) * a_re + lam_im * a_im) / den
    q_im = (lam_im * a_re - (lam_re - 1.0) * a_im) / den
    bb_re = q_re[..., None] * b_re - q_im[..., None] * b_im
    bb_im = q_re[..., None] * b_im + q_im[..., None] * b_re
    gb = S5_GROUPS // S5_SPLIT
    eye = jnp.eye(gb, dtype=F32)

    def dense_b(x):
        x = x.reshape(2, S5_SPLIT, gb, S5_STATE, S5_CH)
        return jnp.einsum('dhgpc,gk->dhgckp', x, eye).reshape(2, S5_SPLIT, gb * S5_CH, gb * S5_STATE)

    def dense_c(x):
        x = x.reshape(S5_SPLIT, gb, S5_CH, S5_STATE)
        return jnp.einsum('hgcp,gk->hgpkc', x, eye).reshape(S5_SPLIT, gb * S5_STATE, gb * S5_CH)

    bdense = jnp.concatenate([dense_b(bb_re), dense_b(bb_im)], axis=-1).astype(BF16)
    cdense = jnp.concatenate([dense_c(c_re), -dense_c(c_im)], axis=1).astype(BF16)
    lam = jnp.stack([lam_re.reshape(2, S5_STATES), lam_im.reshape(2, S5_STATES)], axis=1)
    lam = jnp.broadcast_to(lam[:, :, None, :], (2, 2, S5_SEQ_PAD, S5_STATES))
    return bdense, cdense, lam


MERGE_TILE = 256


def _merge_kernel(x_ref, yr_ref, ys_ref, gate_ref, pr_ref, ps_ref, wo_ref, o_ref):
    y_rwkv = jnp.dot(yr_ref[...], pr_ref[...], preferred_element_type=F32)
    y_s5 = jnp.dot(ys_ref[...], ps_ref[...], preferred_element_type=F32)
    gates = _sigmoid(gate_ref[...])
    merged = gates[:, :D_MODEL] * y_rwkv + gates[:, D_MODEL:] * y_s5
    o_ref[...] = x_ref[...] + _dot(merged, wo_ref[...])


def _merge(x, yr, ys, proj, proj_rwkv, proj_s5, w_out):
    n, d = x.shape
    tm = MERGE_TILE
    assert PROJ_GATE_OFF == 2 * d and proj.shape[1] == PROJ_GATE_OFF + 2 * d
    tok = lambda c: pl.BlockSpec((tm, c), lambda i: (i, 0))
    full = lambda a: pl.BlockSpec(a.shape, lambda i: (0, 0))
    return pl.pallas_call(
        _merge_kernel,
        grid=(n // tm,),
        in_specs=[tok(d), tok(RWKV_DIM), tok(S5_DIM), pl.BlockSpec((tm, 2 * d), lambda i: (i, 1)),
                  full(proj_rwkv), full(proj_s5), full(w_out)],
        out_specs=tok(d),
        out_shape=jax.ShapeDtypeStruct((n, d), F32),
        compiler_params=_cparams(("parallel",)),
        name="merge",
    )(x, yr, ys, proj, proj_rwkv, proj_s5, w_out)


def _pad_ff(w, axis):
    pad = (-w.shape[axis]) % FF_TILE
    widths = [(0, 0)] * w.ndim
    widths[axis] = (0, pad)
    return jnp.pad(w.astype(BF16), widths)


def _lora_cat(w):
    z = jnp.zeros_like(w[0])
    return jnp.concatenate([jnp.concatenate([w[0], z], axis=1),
                            jnp.concatenate([z, w[1]], axis=1)], axis=0).astype(BF16)


def _forward(x_a, x_b, norm_ffn1, ffn1_w_gate, ffn1_w_up, ffn1_w_down, norm_mix, w_in, shift_mu,
             rwkv_w0, rwkv_w2, rwkv_a0, rwkv_a2, rwkv_g2, rwkv_k_k, rwkv_k_a, rwkv_r_k,
             rwkv_ln_w, rwkv_ln_b, s5_a_re, s5_a_im, s5_log_step, s5_b_re, s5_b_im,
             s5_c_re, s5_c_im, s5_d, s5_w_glu, s5_b_glu, proj_rwkv, proj_s5, w_out,
             norm_ffn2, ffn2_w_gate, ffn2_w_up, ffn2_w_down, norm_final):
    l, d = x_a.shape[1:]
    b = x_a.shape[0] + x_b.shape[0]
    n_a, n_b = x_a.shape[0] * l, x_b.shape[0] * l
    n = n_a + n_b
    row = lambda p: p.reshape(1, -1)

    (x1,) = _ffn([x_a.reshape(n_a, d), x_b.reshape(n_b, d)], row(norm_ffn1), _pad_ff(ffn1_w_gate, 1),
                 _pad_ff(ffn1_w_up, 1), _pad_ff(ffn1_w_down, 0), row(norm_final),
                 out_rows=[n], final_norm=False)

    assert w_in.shape[1] == SHIFT_COLS + S5_DIM + 2 * d
    w_proj = jnp.concatenate(
        [w_in[:, :SHIFT_COLS], jnp.zeros((d, PROJ_U_OFF - SHIFT_COLS), w_in.dtype), w_in[:, SHIFT_COLS:]],
        axis=1).astype(BF16)
    proj = _norm_matmul(x1, row(norm_mix), w_proj, PROJ_TOK_TILE, PROJ_COL_TILE, "proj_in")
    u = proj[:, PROJ_U_OFF:PROJ_U_OFF + S5_DIM]

    head_id = jnp.arange(QUAD) // HEAD_DIM
    ones_bd = (head_id[:, None] == head_id[None, :]).astype(BF16)
    r, v, kkn, lw, kd, bd, g, bonus = _rwkv_prep(
        proj.reshape(b, l, PROJ_COLS), row(shift_mu),
        _lora_cat(rwkv_w2), rwkv_w0.reshape(1, -1), _lora_cat(rwkv_a2), rwkv_a0.reshape(1, -1),
        rwkv_g2.astype(BF16), row(rwkv_k_k), row(rwkv_k_a), row(rwkv_r_k), ones_bd)
    yf, yb = _rwkv_scan(r, v, kkn, lw, kd, bd)
    yr = _rwkv_post(yf, yb, bonus, g, row(rwkv_ln_w), row(rwkv_ln_b), ones_bd).reshape(n, RWKV_DIM)

    u_t = jnp.transpose(u.reshape(b, l, S5_DIM), (1, 0, 2))
    u_t = jnp.pad(u_t, ((0, 0), (0, S5_SEQ_PAD - b), (0, 0)))
    bdense, cdense, lam = _s5_params(s5_a_re, s5_a_im, s5_log_step, s5_b_re, s5_b_im, s5_c_re, s5_c_im)
    ys2 = _s5_scan(u_t, bdense, cdense, lam)
    ys_t = _s5_post(ys2, u_t, row(s5_d), s5_w_glu.astype(BF16), row(s5_b_glu))
    ys = jnp.transpose(ys_t[:, :b], (1, 0, 2)).reshape(n, S5_DIM)

    x2 = _merge(x1, yr, ys, proj, proj_rwkv.astype(BF16), proj_s5.astype(BF16), w_out.astype(BF16))
    y_a, y_b = _ffn([x2], row(norm_ffn2), _pad_ff(ffn2_w_gate, 1), _pad_ff(ffn2_w_up, 1),
                    _pad_ff(ffn2_w_down, 0), row(norm_final), out_rows=[n_a, n_b], final_norm=True)
    return y_a.reshape(x_a.shape), y_b.reshape(x_b.shape)


def kernel(x_prompt, x_sample, norm_ffn1, ffn1_w_gate, ffn1_w_up, ffn1_w_down, norm_mix, w_in, shift_mu, rwkv_w0, rwkv_w2, rwkv_a0, rwkv_a2, rwkv_g2, rwkv_k_k, rwkv_k_a, rwkv_r_k, rwkv_ln_w, rwkv_ln_b, s5_a_re, s5_a_im, s5_log_step, s5_b_re, s5_b_im, s5_c_re, s5_c_im, s5_d, s5_w_glu, s5_b_glu, proj_rwkv, proj_s5, w_out, norm_ffn2, ffn2_w_gate, ffn2_w_up, ffn2_w_down, norm_final):
    layer = (norm_ffn1, ffn1_w_gate, ffn1_w_up, ffn1_w_down, norm_mix, w_in, shift_mu,
             rwkv_w0, rwkv_w2, rwkv_a0, rwkv_a2, rwkv_g2, rwkv_k_k, rwkv_k_a, rwkv_r_k,
             rwkv_ln_w, rwkv_ln_b, s5_a_re, s5_a_im, s5_log_step, s5_b_re, s5_b_im,
             s5_c_re, s5_c_im, s5_d, s5_w_glu, s5_b_glu, proj_rwkv, proj_s5, w_out,
             norm_ffn2, ffn2_w_gate, ffn2_w_up, ffn2_w_down)
    assert all(p.shape[0] == 1 for p in layer), "single-layer block"
    assert x_prompt.shape[1:] == x_sample.shape[1:]
    return _forward(x_prompt, x_sample, *[p[0] for p in layer], norm_final)
```

```python
import functools
import math

import jax
import jax.numpy as jnp
from jax import lax
from jax.experimental import pallas as pl
from jax.experimental.pallas import tpu as pltpu

F32 = jnp.float32
BF16 = jnp.bfloat16

D_MODEL = 2048
D_FF = 5504
HEADS = 16
HEAD_DIM = 64
RWKV_DIM = HEADS * HEAD_DIM
LORA = 64
GATE_LORA = 128
S5_GROUPS = 32
S5_CH = 16
S5_DIM = S5_GROUPS * S5_CH
S5_STATE = 64
S5_STATES = S5_GROUPS * S5_STATE
SHIFT_COLS = 3 * RWKV_DIM + 4 * LORA + GATE_LORA
RMS_EPS = 1e-6
GN_EPS = 64e-5
KK_EPS = 1e-12

LANES = 128
SUBLANES = 8
VMEM_LIMIT = 56 * 1024 * 1024

FF_TILE = 512
TOK_TILE = 512
CHUNK = 64
QUAD = 4 * HEAD_DIM
SCAN_CHUNKS = 2
S5_SEQ_PAD = SUBLANES
S5_TILE = 64
S5_SPLIT = 2
PROJ_TOK_TILE = 1024
PROJ_COL_TILE = 1024
PROJ_U_OFF = 3584
PROJ_GATE_OFF = 4096
PROJ_COLS = 8192


def _cparams(sem):
    return pltpu.CompilerParams(dimension_semantics=sem, vmem_limit_bytes=VMEM_LIMIT)


def _dot(a, b):
    return jnp.dot(a.astype(BF16), b.astype(BF16), preferred_element_type=F32)


def _dot_nt(a, b):
    return lax.dot_general(a.astype(BF16), b.astype(BF16), (((1,), (1,)), ((), ())),
                           preferred_element_type=F32)


def _dot_tn(a, b):
    return lax.dot_general(a.astype(BF16), b.astype(BF16), (((0,), (0,)), ((), ())),
                           preferred_element_type=F32)


def _split_dot(x, w, pieces):
    acc = None
    rem = x
    for _ in range(pieces):
        part = rem.astype(BF16)
        term = jnp.dot(part, w, preferred_element_type=F32)
        acc = term if acc is None else acc + term
        rem = rem - part.astype(F32)
    return acc


def _sigmoid(x):
    return 1.0 / (1.0 + jnp.exp(-x))


def _rms(x, g):
    ms = jnp.mean(x * x, axis=-1, keepdims=True)
    return x * lax.rsqrt(ms + RMS_EPS) * g


def _ffn_kernel(*refs, n_x, n_out, first_tiles, final_norm):
    x_refs = refs[:n_x]
    g_ref, wg_ref, wu_ref, wd_ref, gf_ref = refs[n_x:n_x + 5]
    o_refs = refs[n_x + 5:n_x + 5 + n_out]
    xn_ref, acc_ref = refs[n_x + 5 + n_out:]
    i = pl.program_id(0)
    j = pl.program_id(1)
    in_first = i < first_tiles
    parts = [(in_first, x_refs[0], o_refs[0]), (jnp.logical_not(in_first), x_refs[-1], o_refs[-1])]
    if n_x == 1 and n_out == 1:
        parts = [(True, x_refs[0], o_refs[0])]

    for cond, x_ref, _ in parts:
        @pl.when((j == 0) & cond)
        def _(x_ref=x_ref):
            xn_ref[...] = _rms(x_ref[...], g_ref[...]).astype(BF16)
            acc_ref[...] = jnp.zeros_like(acc_ref)

    xn = xn_ref[...]
    hg = jnp.dot(xn, wg_ref[...], preferred_element_type=F32)
    hu = jnp.dot(xn, wu_ref[...], preferred_element_type=F32)
    act = (hg * _sigmoid(hg)) * hu
    acc_ref[...] += jnp.dot(act.astype(BF16), wd_ref[...], preferred_element_type=F32)

    for cond, x_ref, o_ref in parts:
        @pl.when((j == pl.num_programs(1) - 1) & cond)
        def _(x_ref=x_ref, o_ref=o_ref):
            y = x_ref[...] + 0.5 * acc_ref[...]
            o_ref[...] = _rms(y, gf_ref[...]) if final_norm else y


def _ffn(xs, g, wg, wu, wd, gf, *, out_rows, final_norm):
    d = xs[0].shape[1]
    f = wg.shape[1]
    tm, tf = TOK_TILE, FF_TILE
    n = sum(x.shape[0] for x in xs)
    assert n == sum(out_rows) and all(r % tm == 0 for r in out_rows)
    assert all(x.shape[0] % tm == 0 for x in xs)
    first_rows = xs[0].shape[0] if len(xs) == 2 else out_rows[0]
    if len(xs) == 2 and len(out_rows) == 2:
        assert xs[0].shape[0] == out_rows[0]
    first_tiles = first_rows // tm

    def split_specs(count):
        if count == 1:
            return [pl.BlockSpec((tm, d), lambda i, j: (i, 0))]
        return [pl.BlockSpec((tm, d), lambda i, j: (jnp.minimum(i, first_tiles - 1), 0)),
                pl.BlockSpec((tm, d), lambda i, j: (jnp.maximum(i - first_tiles, 0), 0))]

    outs = pl.pallas_call(
        functools.partial(_ffn_kernel, n_x=len(xs), n_out=len(out_rows), first_tiles=first_tiles,
                          final_norm=final_norm),
        grid=(n // tm, f // tf),
        in_specs=split_specs(len(xs)) + [
            pl.BlockSpec((1, d), lambda i, j: (0, 0)),
            pl.BlockSpec((d, tf), lambda i, j: (0, j)),
            pl.BlockSpec((d, tf), lambda i, j: (0, j)),
            pl.BlockSpec((tf, d), lambda i, j: (j, 0)),
            pl.BlockSpec((1, d), lambda i, j: (0, 0)),
        ],
        out_specs=split_specs(len(out_rows)),
        out_shape=[jax.ShapeDtypeStruct((r, d), F32) for r in out_rows],
        scratch_shapes=[pltpu.VMEM((tm, d), BF16), pltpu.VMEM((tm, d), F32)],
        compiler_params=_cparams(("arbitrary", "arbitrary")),
        name="ffn_final" if final_norm else "ffn",
    )(*xs, g, wg, wu, wd, gf)
    return outs


def _norm_matmul_kernel(x_ref, g_ref, w_ref, o_ref, xn_ref):
    @pl.when(pl.program_id(1) == 0)
    def _():
        xn_ref[...] = _rms(x_ref[...], g_ref[...]).astype(BF16)

    o_ref[...] = jnp.dot(xn_ref[...], w_ref[...], preferred_element_type=F32)


def _norm_matmul(x, g, w, tm, tn, name):
    n, d = x.shape
    c = w.shape[1]
    return pl.pallas_call(
        _norm_matmul_kernel,
        grid=(n // tm, c // tn),
        in_specs=[
            pl.BlockSpec((tm, d), lambda i, j: (i, 0)),
            pl.BlockSpec((1, d), lambda i, j: (0, 0)),
            pl.BlockSpec((d, tn), lambda i, j: (0, j)),
        ],
        out_specs=pl.BlockSpec((tm, tn), lambda i, j: (i, j)),
        out_shape=jax.ShapeDtypeStruct((n, c), F32),
        scratch_shapes=[pltpu.VMEM((tm, d), BF16)],
        compiler_params=_cparams(("parallel", "arbitrary")),
        name=name,
    )(x, g, w)


PREP_TILE = 256


def _head_sum(x, ones_quad):
    return jnp.concatenate([_split_dot(x[:, c:c + QUAD], ones_quad, 2)
                            for c in range(0, x.shape[1], QUAD)], axis=1)


def _rwkv_prep_kernel(main_ref, prev_ref, next_ref, mu_ref, w2_ref, w0_ref, a2_ref, a0_ref, g2_ref,
                      kk_ref, ka_ref, rk_ref, ones_ref, tri_ref,
                      at_out, rt_out, bt_out, kt_out, wt_out, v_out, g_out, bonus_out):
    i = pl.program_id(1)
    tl = main_ref.shape[1]
    x = main_ref[0]
    row = lax.broadcasted_iota(jnp.int32, (tl, 1), 0)
    before = jnp.where(i == 0, 0.0, prev_ref[0, SUBLANES - 1:SUBLANES, :])
    after = jnp.where(i == pl.num_programs(1) - 1, 0.0, next_ref[0, 0:1, :])
    prev = jnp.where(row == 0, before, pltpu.roll(x, 1, 0))
    nxt = jnp.where(row == tl - 1, after, pltpu.roll(x, tl - 1, 0))
    p = x + (0.5 * (prev + nxt) - x) * mu_ref[...]

    c0 = 3 * RWKV_DIM
    r = p[:, 0:RWKV_DIM]
    k = p[:, RWKV_DIM:2 * RWKV_DIM]
    v = p[:, 2 * RWKV_DIM:c0]
    wlow = jnp.tanh(p[:, c0:c0 + 2 * LORA])
    alow = p[:, c0 + 2 * LORA:c0 + 4 * LORA]
    glow = p[:, c0 + 4 * LORA:c0 + 4 * LORA + GATE_LORA]
    ones_bd = ones_ref[...]

    wpre = _dot(wlow, w2_ref[...]) + w0_ref[...]
    lw = (-math.exp(-0.5)) * _sigmoid(wpre)
    a = _sigmoid(_dot(alow, a2_ref[...]) + a0_ref[...])

    kk = k * kk_ref[...]
    kkn = kk * lax.rsqrt(_head_sum(kk * kk, ones_bd) + KK_EPS)
    ka = ka_ref[...]
    ksum = None
    n_chunks = tl // CHUNK
    for d in range(2):
        a_d = a[:, d * RWKV_DIM:(d + 1) * RWKV_DIM]
        lw_d = lw[:, d * RWKV_DIM:(d + 1) * RWKV_DIM]
        kd = k * (1.0 + (a_d - 1.0) * ka)
        cum = _cumsum(tri_ref[d], lw_d)
        e_out = jnp.exp(-cum)
        at_out[d, 0] = (-kkn * jnp.exp(cum - lw_d)).astype(BF16)
        rt_out[d, 0] = (r * jnp.exp(cum)).astype(BF16)
        bt_out[d, 0] = (kkn * a_d * e_out).astype(BF16)
        kt_out[d, 0] = (kd * e_out).astype(BF16)
        total = jnp.exp(jnp.sum(lw_d.reshape(n_chunks, CHUNK, RWKV_DIM), axis=1))
        wt_out[d, 0] = jnp.broadcast_to(total[:, None, :], (n_chunks, SUBLANES, RWKV_DIM)).reshape(
            n_chunks * SUBLANES, RWKV_DIM)
        ksum = kd if ksum is None else ksum + kd
    v_out[0] = v.astype(BF16)
    g_out[0] = _dot(_sigmoid(glow), g2_ref[...])
    bonus_out[0] = _head_sum(r * ksum * rk_ref[...], ones_bd) * v


def _rwkv_prep(pshift, mu, w2cat, w0cat, a2cat, a0cat, g2, k_k, k_a, r_k, ones_bd):
    b, l, _ = pshift.shape
    c = SHIFT_COLS
    tl = PREP_TILE
    nt = l // tl
    hb = tl // SUBLANES
    nhb = l // SUBLANES
    full = lambda shape: pl.BlockSpec(shape, lambda bi, i: (0,) * len(shape))
    tok = pl.BlockSpec((1, tl, RWKV_DIM), lambda bi, i: (bi, i, 0))
    tok2 = pl.BlockSpec((2, 1, tl, RWKV_DIM), lambda bi, i: (0, bi, i, 0))
    wrows = tl // CHUNK * SUBLANES
    tokw = pl.BlockSpec((2, 1, wrows, RWKV_DIM), lambda bi, i: (0, bi, i, 0))
    s1 = jax.ShapeDtypeStruct((b, l, RWKV_DIM), F32)
    s1h = jax.ShapeDtypeStruct((b, l, RWKV_DIM), BF16)
    s2h = jax.ShapeDtypeStruct((2, b, l, RWKV_DIM), BF16)
    sw = jax.ShapeDtypeStruct((2, b, l // CHUNK * SUBLANES, RWKV_DIM), F32)
    pos = jnp.arange(tl)
    same_chunk = (pos[:, None] // CHUNK) == (pos[None, :] // CHUNK)
    tri = jnp.stack([same_chunk & (pos[None, :] <= pos[:, None]),
                     same_chunk & (pos[None, :] >= pos[:, None])]).astype(BF16)
    return pl.pallas_call(
        _rwkv_prep_kernel,
        grid=(b, nt),
        in_specs=[
            pl.BlockSpec((1, tl, c), lambda bi, i: (bi, i, 0)),
            pl.BlockSpec((1, SUBLANES, c), lambda bi, i: (bi, jnp.maximum(i * hb - 1, 0), 0)),
            pl.BlockSpec((1, SUBLANES, c), lambda bi, i: (bi, jnp.minimum((i + 1) * hb, nhb - 1), 0)),
            full((1, c)),
            full(w2cat.shape), full(w0cat.shape), full(a2cat.shape), full(a0cat.shape), full(g2.shape),
            full((1, RWKV_DIM)), full((1, RWKV_DIM)), full((1, RWKV_DIM)), full(ones_bd.shape),
            full(tri.shape),
        ],
        out_specs=[tok2, tok2, tok2, tok2, tokw, tok, tok, tok],
        out_shape=[s2h, s2h, s2h, s2h, sw, s1h, s1, s1],
        compiler_params=_cparams(("parallel", "arbitrary")),
        name="rwkv_prep",
    )(pshift, pshift, pshift, mu, w2cat, w0cat, a2cat, a0cat, g2, k_k, k_a, r_k, ones_bd, tri)


def _stack_heads(x, head_masks):
    return jnp.concatenate([jnp.where(m, x, 0.0) for m in head_masks], axis=0)


def _rwkv_scan_kernel(atf_ref, rtf_ref, btf_ref, ktf_ref, wtf_ref, vf_ref,
                      atb_ref, rtb_ref, btb_ref, ktb_ref, wtb_ref, vb_ref,
                      of_ref, ob_ref, s_ref):
    t = CHUNK

    @pl.when(pl.program_id(1) == 0)
    def _():
        s_ref[...] = jnp.zeros_like(s_ref)

    row = lax.broadcasted_iota(jnp.int32, (t, QUAD), 0)
    lane = lax.broadcasted_iota(jnp.int32, (t, QUAD), 1)
    col = lane & (HEAD_DIM - 1)
    same16 = (row >> 4) == (col >> 4)
    same32 = (row >> 5) == (col >> 5)
    in32_off16 = same32 & jnp.logical_not(same16)
    eye = jnp.where(col == row, 1.0, 0.0)
    head_masks = [(lane >> 6) == h for h in range(4)]
    strict = (col < row, col > row)
    incl = (col <= row, col >= row)
    qrow = lax.broadcasted_iota(jnp.int32, (QUAD, QUAD), 0) >> 6
    qcol = lax.broadcasted_iota(jnp.int32, (QUAD, QUAD), 1) >> 6
    same_head = qrow == qcol
    stack = lambda x: _stack_heads(x, head_masks)
    cat = lambda a, b: jnp.concatenate([a, b], axis=0)

    refs = ((atf_ref, rtf_ref, btf_ref, ktf_ref, wtf_ref, vf_ref, of_ref),
            (atb_ref, rtb_ref, btb_ref, ktb_ref, wtb_ref, vb_ref, ob_ref))
    n_sub = vf_ref.shape[1] // t
    chains = [(d, q, c) for c in range(n_sub) for q in range(RWKV_DIM // QUAD) for d in range(2)]
    each = lambda fn, *lists: [fn(*args) for args in zip(*lists)]
    sl = lambda q: slice(q * QUAD, (q + 1) * QUAD)

    def chunk(d, c):
        return c if d == 0 else n_sub - 1 - c

    def rows(d, c):
        return slice(chunk(d, c) * t, (chunk(d, c) + 1) * t)

    load = lambda j: [refs[d][j][0, 0, rows(d, c), sl(q)].astype(F32) for d, q, c in chains]
    at, rt, bt, kt = load(0), load(1), load(2), load(3)
    v = [refs[d][5][0, rows(d, c), sl(q)].astype(F32) for d, q, c in chains]
    w_total = [refs[d][4][0, 0, chunk(d, c) * SUBLANES:chunk(d, c) * SUBLANES + 1, sl(q)]
               for d, q, c in chains]
    lhs = each(cat, at, rt)
    bk = each(cat, bt, kt)

    gram = each(lambda l_, b_, k_: _dot_nt(l_, cat(stack(b_), stack(k_))), lhs, bt, kt)
    a_ab = [jnp.where(strict[d], g[:t, :QUAD], 0.0) for (d, _, _), g in zip(chains, gram)]
    a_ak = [jnp.where(strict[d], g[:t, QUAD:], 0.0) for (d, _, _), g in zip(chains, gram)]
    a_rb = [jnp.where(incl[d], g[t:, :QUAD], 0.0) for (d, _, _), g in zip(chains, gram)]
    a_rk = [jnp.where(incl[d], g[t:, QUAD:], 0.0) for (d, _, _), g in zip(chains, gram)]
    from_v = each(lambda p, q_, v_: _dot(cat(p, q_), stack(v_)), a_ak, a_rk, v)

    a_d = each(lambda a: jnp.where(same16, a, 0.0), a_ab)
    pw = each(lambda a: eye + a, a_d)
    sq = each(lambda a: _dot(a, stack(a)), a_d)
    for _ in range(2):
        both = each(lambda p, s: _dot(cat(p, s), stack(s)), pw, sq)
        pw = each(lambda p, bo: p + bo[:t], pw, both)
        sq = each(lambda bo: bo[t:], both)
    x0 = each(lambda p, s: p + _dot(p, stack(s)), pw, sq)
    m1 = each(lambda x, a: _dot(x, stack(jnp.where(in32_off16, a, 0.0))), x0, a_ab)
    x1 = each(lambda x, m: x + _dot(m, stack(x)), x0, m1)
    m2 = each(lambda x, a: _dot(x, stack(jnp.where(same32, 0.0, a))), x1, a_ab)
    x2 = each(lambda x, m: x + _dot(m, stack(x)), x1, m2)

    per_chunk = len(chains) // n_sub
    state = [s_ref[d, q] for d, q, _ in chains[:per_chunk]]
    for c in range(n_sub):
        pick = lambda xs: xs[c * per_chunk:(c + 1) * per_chunk]
        from_state = each(_dot_nt, pick(lhs), state)
        u = each(lambda x, fs, fv: _dot(x, stack(fs[:t] + fv[:t])), pick(x2), from_state, pick(from_v))
        y = each(lambda fs, fv, a, u_: fs[t:] + fv[t:] + _dot(a, stack(u_)),
                 from_state, pick(from_v), pick(a_rb), u)
        upd = each(lambda u_, v_, bk_: _dot_tn(cat(u_, v_), bk_), u, pick(v), pick(bk))
        for (d, q, _), y_ in zip(pick(chains), y):
            refs[d][6][0, rows(d, c), sl(q)] = y_
        state = each(lambda s_, up, w: (s_ + jnp.where(same_head, up, 0.0)) * w, state, upd, pick(w_total))
    for (d, q, _), s_ in zip(chains[:per_chunk], state):
        s_ref[d, q] = s_


def _cumsum(tri, lw):
    acc = None
    rem = lw
    for _ in range(3):
        part = rem.astype(BF16)
        term = jnp.dot(tri, part, preferred_element_type=F32)
        acc = term if acc is None else acc + term
        rem = rem - part.astype(F32)
    return acc


def _rwkv_scan(at, rt, bt, kt, wt, v):
    b, l, c = v.shape
    tl = SCAN_CHUNKS * CHUNK
    nc = l // tl
    wrows = SCAN_CHUNKS * SUBLANES
    fwd = pl.BlockSpec((1, tl, c), lambda bi, i: (bi, i, 0))
    bwd = pl.BlockSpec((1, tl, c), lambda bi, i: (bi, nc - 1 - i, 0))
    fwd_d = pl.BlockSpec((1, 1, tl, c), lambda bi, i: (0, bi, i, 0))
    bwd_d = pl.BlockSpec((1, 1, tl, c), lambda bi, i: (1, bi, nc - 1 - i, 0))
    fwd_w = pl.BlockSpec((1, 1, wrows, c), lambda bi, i: (0, bi, i, 0))
    bwd_w = pl.BlockSpec((1, 1, wrows, c), lambda bi, i: (1, bi, nc - 1 - i, 0))
    return pl.pallas_call(
        _rwkv_scan_kernel,
        grid=(b, nc),
        in_specs=[fwd_d, fwd_d, fwd_d, fwd_d, fwd_w, fwd, bwd_d, bwd_d, bwd_d, bwd_d, bwd_w, bwd],
        out_specs=[fwd, bwd],
        out_shape=[jax.ShapeDtypeStruct((b, l, c), F32)] * 2,
        scratch_shapes=[pltpu.VMEM((2, c // QUAD, QUAD, QUAD), F32)],
        compiler_params=_cparams(("parallel", "arbitrary")),
        name="rwkv_scan",
    )(at, rt, bt, kt, wt, v, at, rt, bt, kt, wt, v)


def _s5_scan_kernel(u_ref, bdense_ref, cdense_ref, lam_ref, o_ref, st_ref, s_ref):
    d = pl.program_id(0)
    tl = u_ref.shape[0]
    ns = S5_STATES

    @pl.when(pl.program_id(1) == 0)
    def _():
        s_ref[...] = jnp.zeros_like(s_ref)

    rows = tl * S5_SEQ_PAD
    hs = ns // S5_SPLIT
    hc = S5_DIM // S5_SPLIT
    u = u_ref[...].reshape(rows, S5_DIM)
    for h in range(S5_SPLIT):
        bu = _dot(u[:, h * hc:(h + 1) * hc], bdense_ref[0, h])
        st_ref[:, :, h * hs:(h + 1) * hs] = bu[:, :hs].reshape(tl, S5_SEQ_PAD, hs)
        st_ref[:, :, ns + h * hs:ns + (h + 1) * hs] = bu[:, hs:].reshape(tl, S5_SEQ_PAD, hs)
    lam_re = lam_ref[0, 0]
    lam_im = lam_ref[0, 1]

    def step(j, carry):
        s_re, s_im = carry
        tt = jnp.where(d == 0, j, tl - 1 - j)
        n_re = lam_re * s_re - lam_im * s_im + st_ref[tt, :, :ns]
        n_im = lam_re * s_im + lam_im * s_re + st_ref[tt, :, ns:]
        st_ref[tt, :, :ns] = n_re
        st_ref[tt, :, ns:] = n_im
        return n_re, n_im

    s_re, s_im = lax.fori_loop(0, tl, step, (s_ref[0], s_ref[1]), unroll=2)
    s_ref[0] = s_re
    s_ref[1] = s_im
    for h in range(S5_SPLIT):
        st_re = st_ref[:, :, h * hs:(h + 1) * hs].reshape(rows, hs)
        st_im = st_ref[:, :, ns + h * hs:ns + (h + 1) * hs].reshape(rows, hs)
        y = _dot(st_re, cdense_ref[h, :hs]) + _dot(st_im, cdense_ref[h, hs:])
        o_ref[0, :, :, h * hc:(h + 1) * hc] = y.reshape(tl, S5_SEQ_PAD, hc)


def _s5_scan(u_t, bdense, cdense, lam):
    l = u_t.shape[0]
    tl = S5_TILE
    nt = l // tl
    pos = lambda dd, i: i + dd * (nt - 1 - 2 * i)
    return pl.pallas_call(
        _s5_scan_kernel,
        grid=(2, nt),
        in_specs=[
            pl.BlockSpec((tl, S5_SEQ_PAD, S5_DIM), lambda dd, i: (pos(dd, i), 0, 0)),
            pl.BlockSpec((1,) + bdense.shape[1:], lambda dd, i: (dd, 0, 0, 0)),
            pl.BlockSpec(cdense.shape, lambda dd, i: (0, 0, 0)),
            pl.BlockSpec((1, 2, S5_SEQ_PAD, S5_STATES), lambda dd, i: (dd, 0, 0, 0)),
        ],
        out_specs=pl.BlockSpec((1, tl, S5_SEQ_PAD, S5_DIM), lambda dd, i: (dd, pos(dd, i), 0, 0)),
        out_shape=jax.ShapeDtypeStruct((2, l, S5_SEQ_PAD, S5_DIM), F32),
        scratch_shapes=[pltpu.VMEM((tl, S5_SEQ_PAD, 2 * S5_STATES), F32),
                        pltpu.VMEM((2, S5_SEQ_PAD, S5_STATES), F32)],
        compiler_params=_cparams(("parallel", "arbitrary")),
        name="s5_scan",
    )(u_t, bdense, cdense, lam)


def _s5_post_kernel(yf_ref, yb_ref, u_ref, dskip_ref, wglu_ref, bglu_ref, o_ref):
    tl = u_ref.shape[0]
    rows = tl * S5_SEQ_PAD
    u = u_ref[...].reshape(rows, S5_DIM)
    y = (yf_ref[0] + yb_ref[0]).reshape(rows, S5_DIM) + dskip_ref[...] * u
    y = 0.5 * y * (1.0 + jnp.tanh(math.sqrt(2.0 / math.pi) * (y + 0.044715 * (y * y * y))))
    gate = _sigmoid(_dot(y, wglu_ref[...]) + bglu_ref[...])
    o_ref[...] = (y * gate).astype(BF16).reshape(tl, S5_SEQ_PAD, S5_DIM)


def _s5_post(y2, u_t, d_skip, w_glu, b_glu):
    l = u_t.shape[0]
    tl = S5_TILE
    tok = pl.BlockSpec((tl, S5_SEQ_PAD, S5_DIM), lambda i: (i, 0, 0))
    row = pl.BlockSpec((1, S5_DIM), lambda i: (0, 0))
    return pl.pallas_call(
        _s5_post_kernel,
        grid=(l // tl,),
        in_specs=[
            pl.BlockSpec((1, tl, S5_SEQ_PAD, S5_DIM), lambda i: (0, i, 0, 0)),
            pl.BlockSpec((1, tl, S5_SEQ_PAD, S5_DIM), lambda i: (1, i, 0, 0)),
            tok, row,
            pl.BlockSpec((S5_DIM, S5_DIM), lambda i: (0, 0)),
            row,
        ],
        out_specs=tok,
        out_shape=jax.ShapeDtypeStruct((l, S5_SEQ_PAD, S5_DIM), BF16),
        compiler_params=_cparams(("parallel",)),
        name="s5_post",
    )(y2, y2, u_t, d_skip, w_glu, b_glu)


def _s5_params(a_re, a_im, log_step, b_re, b_im, c_re, c_im):
    dt = jnp.exp(log_step)[..., None]
    z_re, z_im = a_re * dt, a_im * dt
    mag = jnp.exp(z_re)
    lam_re, lam_im = mag * jnp.cos(z_im), mag * jnp.sin(z_im)
    den = a_re * a_re + a_im * a_im
    q_re = ((lam_re - 1.0) * a_re + lam_im * a_im) / den
    q_im = (lam_im * a_re - (lam_re - 1.0) * a_im) / den
    bb_re = q_re[..., None] * b_re - q_im[..., None] * b_im
    bb_im = q_re[..., None] * b_im + q_im[..., None] * b_re
    gb = S5_GROUPS // S5_SPLIT
    eye = jnp.eye(gb, dtype=F32)

    def dense_b(x):
        x = x.reshape(2, S5_SPLIT, gb, S5_STATE, S5_CH)
        return jnp.einsum('dhgpc,gk->dhgckp', x, eye).reshape(2, S5_SPLIT, gb * S5_CH, gb * S5_STATE)

    def dense_c(x):
        x = x.reshape(S5_SPLIT, gb, S5_CH, S5_STATE)
        return jnp.einsum('hgcp,gk->hgpkc', x, eye).reshape(S5_SPLIT, gb * S5_STATE, gb * S5_CH)

    bdense = jnp.concatenate([dense_b(bb_re), dense_b(bb_im)], axis=-1).astype(BF16)
    cdense = jnp.concatenate([dense_c(c_re), -dense_c(c_im)], axis=1).astype(BF16)
    lam = jnp.stack([lam_re.reshape(2, S5_STATES), lam_im.reshape(2, S5_STATES)], axis=1)
    lam = jnp.broadcast_to(lam[:, :, None, :], (2, 2, S5_SEQ_PAD, S5_STATES))
    return bdense, cdense, lam


MERGE_TILE = 256


def _merge_kernel(x_ref, yf_ref, yb_ref, bonus_ref, g_ref, ys_ref, gate_ref,
                  lnw_ref, lnb_ref, ones_ref, pr_ref, ps_ref, wo_ref, o_ref):
    ones_bd = ones_ref[...]
    y = yf_ref[...] + yb_ref[...]
    mean = _head_sum(y, ones_bd) * (1.0 / HEAD_DIM)
    yc = y - mean
    var = _head_sum(yc * yc, ones_bd) * (1.0 / HEAD_DIM)
    yn = yc * lax.rsqrt(var + GN_EPS) * lnw_ref[...] + lnb_ref[...]
    yr = (yn + bonus_ref[...]) * g_ref[...]
    y_rwkv = _dot(yr, pr_ref[...])
    y_s5 = jnp.dot(ys_ref[...], ps_ref[...], preferred_element_type=F32)
    gates = _sigmoid(gate_ref[...])
    merged = gates[:, :D_MODEL] * y_rwkv + gates[:, D_MODEL:] * y_s5
    o_ref[...] = x_ref[...] + _dot(merged, wo_ref[...])


def _merge(x, yf, yb, bonus, g, ys, proj, ln_w, ln_b, ones_bd, proj_rwkv, proj_s5, w_out):
    n, d = x.shape
    tm = MERGE_TILE
    assert PROJ_GATE_OFF == 2 * d and proj.shape[1] == PROJ_GATE_OFF + 2 * d
    tok = lambda c: pl.BlockSpec((tm, c), lambda i: (i, 0))
    const = lambda a: pl.BlockSpec(a.shape, lambda i: (0, 0), pipeline_mode=pl.Buffered(1))
    return pl.pallas_call(
        _merge_kernel,
        grid=(n // tm,),
        in_specs=[tok(d), tok(RWKV_DIM), tok(RWKV_DIM), tok(RWKV_DIM), tok(RWKV_DIM), tok(S5_DIM),
                  pl.BlockSpec((tm, 2 * d), lambda i: (i, 1)),
                  const(ln_w), const(ln_b), const(ones_bd), const(proj_rwkv), const(proj_s5), const(w_out)],
        out_specs=tok(d),
        out_shape=jax.ShapeDtypeStruct((n, d), F32),
        compiler_params=_cparams(("parallel",)),
        name="merge",
    )(x, yf, yb, bonus, g, ys, proj, ln_w, ln_b, ones_bd, proj_rwkv, proj_s5, w_out)


def _cast_pad_kernel(x_ref, o_ref, *, valid_tiles):
    cols = x_ref.shape[1]

    @pl.when(pl.program_id(0) < valid_tiles)
    def _():
        o_ref[:, :cols] = x_ref[...].astype(BF16)
        if o_ref.shape[1] > cols:
            o_ref[:, cols:] = jnp.zeros((o_ref.shape[0], o_ref.shape[1] - cols), BF16)

    @pl.when(pl.program_id(0) >= valid_tiles)
    def _():
        o_ref[...] = jnp.zeros_like(o_ref)


def _pad_ff(w, axis):
    rows, cols = w.shape
    pad = (-w.shape[axis]) % FF_TILE
    out_rows, out_cols = (rows + pad, cols) if axis == 0 else (rows, cols + pad)
    tr = LANES if axis == 0 else 2 * LANES
    assert rows % tr == 0 and out_rows % tr == 0 and cols % LANES == 0
    valid_tiles = rows // tr
    return pl.pallas_call(
        functools.partial(_cast_pad_kernel, valid_tiles=valid_tiles),
        grid=(out_rows // tr,),
        in_specs=[pl.BlockSpec((tr, cols), lambda i: (jnp.minimum(i, valid_tiles - 1), 0))],
        out_specs=pl.BlockSpec((tr, out_cols), lambda i: (i, 0)),
        out_shape=jax.ShapeDtypeStruct((out_rows, out_cols), BF16),
        compiler_params=_cparams(("parallel",)),
        name="cast_pad",
    )(w)


def _lora_cat(w):
    z = jnp.zeros_like(w[0])
    return jnp.concatenate([jnp.concatenate([w[0], z], axis=1),
                            jnp.concatenate([z, w[1]], axis=1)], axis=0).astype(BF16)


def _forward(x_a, x_b, norm_ffn1, ffn1_w_gate, ffn1_w_up, ffn1_w_down, norm_mix, w_in, shift_mu,
             rwkv_w0, rwkv_w2, rwkv_a0, rwkv_a2, rwkv_g2, rwkv_k_k, rwkv_k_a, rwkv_r_k,
             rwkv_ln_w, rwkv_ln_b, s5_a_re, s5_a_im, s5_log_step, s5_b_re, s5_b_im,
             s5_c_re, s5_c_im, s5_d, s5_w_glu, s5_b_glu, proj_rwkv, proj_s5, w_out,
             norm_ffn2, ffn2_w_gate, ffn2_w_up, ffn2_w_down, norm_final):
    l, d = x_a.shape[1:]
    b = x_a.shape[0] + x_b.shape[0]
    n_a, n_b = x_a.shape[0] * l, x_b.shape[0] * l
    n = n_a + n_b
    row = lambda p: p.reshape(1, -1)

    (x1,) = _ffn([x_a.reshape(n_a, d), x_b.reshape(n_b, d)], row(norm_ffn1), _pad_ff(ffn1_w_gate, 1),
                 _pad_ff(ffn1_w_up, 1), _pad_ff(ffn1_w_down, 0), row(norm_final),
                 out_rows=[n], final_norm=False)

    assert w_in.shape[1] == SHIFT_COLS + S5_DIM + 2 * d
    w_proj = jnp.concatenate(
        [w_in[:, :SHIFT_COLS], jnp.zeros((d, PROJ_U_OFF - SHIFT_COLS), w_in.dtype), w_in[:, SHIFT_COLS:]],
        axis=1).astype(BF16)
    proj = _norm_matmul(x1, row(norm_mix), w_proj, PROJ_TOK_TILE, PROJ_COL_TILE, "proj_in")
    u = proj[:, PROJ_U_OFF:PROJ_U_OFF + S5_DIM]

    head_id = jnp.arange(QUAD) // HEAD_DIM
    ones_bd = (head_id[:, None] == head_id[None, :]).astype(BF16)
    at, rt, bt, kt, wt, v, g, bonus = _rwkv_prep(
        proj.reshape(b, l, PROJ_COLS), row(shift_mu),
        _lora_cat(rwkv_w2), rwkv_w0.reshape(1, -1), _lora_cat(rwkv_a2), rwkv_a0.reshape(1, -1),
        rwkv_g2.astype(BF16), row(rwkv_k_k), row(rwkv_k_a), row(rwkv_r_k), ones_bd)
    yf, yb = _rwkv_scan(at, rt, bt, kt, wt, v)

    u_t = jnp.transpose(u.reshape(b, l, S5_DIM), (1, 0, 2))
    u_t = jnp.pad(u_t, ((0, 0), (0, S5_SEQ_PAD - b), (0, 0)))
    bdense, cdense, lam = _s5_params(s5_a_re, s5_a_im, s5_log_step, s5_b_re, s5_b_im, s5_c_re, s5_c_im)
    ys2 = _s5_scan(u_t, bdense, cdense, lam)
    ys_t = _s5_post(ys2, u_t, row(s5_d), s5_w_glu.astype(BF16), row(s5_b_glu))
    ys = jnp.transpose(ys_t[:, :b], (1, 0, 2)).reshape(n, S5_DIM)

    flat = lambda a: a.reshape(n, RWKV_DIM)
    x2 = _merge(x1, flat(yf), flat(yb), flat(bonus), flat(g), ys, proj, row(rwkv_ln_w), row(rwkv_ln_b),
                ones_bd, proj_rwkv.astype(BF16), proj_s5.astype(BF16), w_out.astype(BF16))
    y_a, y_b = _ffn([x2], row(norm_ffn2), _pad_ff(ffn2_w_gate, 1), _pad_ff(ffn2_w_up, 1),
                    _pad_ff(ffn2_w_down, 0), row(norm_final), out_rows=[n_a, n_b], final_norm=True)
    return y_a.reshape(x_a.shape), y_b.reshape(x_b.shape)


def kernel(x_prompt, x_sample, norm_ffn1, ffn1_w_gate, ffn1_w_up, ffn1_w_down, norm_mix, w_in, shift_mu, rwkv_w0, rwkv_w2, rwkv_a0, rwkv_a2, rwkv_g2, rwkv_k_k, rwkv_k_a, rwkv_r_k, rwkv_ln_w, rwkv_ln_b, s5_a_re, s5_a_im, s5_log_step, s5_b_re, s5_b_im, s5_c_re, s5_c_im, s5_d, s5_w_glu, s5_b_glu, proj_rwkv, proj_s5, w_out, norm_ffn2, ffn2_w_gate, ffn2_w_up, ffn2_w_down, norm_final):
    layer = (norm_ffn1, ffn1_w_gate, ffn1_w_up, ffn1_w_down, norm_mix, w_in, shift_mu,
             rwkv_w0, rwkv_w2, rwkv_a0, rwkv_a2, rwkv_g2, rwkv_k_k, rwkv_k_a, rwkv_r_k,
             rwkv_ln_w, rwkv_ln_b, s5_a_re, s5_a_im, s5_log_step, s5_b_re, s5_b_im,
             s5_c_re, s5_c_im, s5_d, s5_w_glu, s5_b_glu, proj_rwkv, proj_s5, w_out,
             norm_ffn2, ffn2_w_gate, ffn2_w_up, ffn2_w_down)
    assert all(p.shape[0] == 1 for p in layer), "single-layer block"
    assert x_prompt.shape[1:] == x_sample.shape[1:]
    return _forward(x_prompt, x_sample, *[p[0] for p in layer], norm_final)
```

```python
import functools
import math

import jax
import jax.numpy as jnp
from jax import lax
from jax.experimental import pallas as pl
from jax.experimental.pallas import tpu as pltpu

F32 = jnp.float32
BF16 = jnp.bfloat16

D_MODEL = 2048
D_FF = 5504
HEADS = 16
HEAD_DIM = 64
RWKV_DIM = HEADS * HEAD_DIM
LORA = 64
GATE_LORA = 128
S5_GROUPS = 32
S5_CH = 16
S5_DIM = S5_GROUPS * S5_CH
S5_STATE = 64
S5_STATES = S5_GROUPS * S5_STATE
SHIFT_COLS = 3 * RWKV_DIM + 4 * LORA + GATE_LORA
RMS_EPS = 1e-6
GN_EPS = 64e-5
KK_EPS = 1e-12

LANES = 128
SUBLANES = 8
VMEM_LIMIT = 56 * 1024 * 1024

FF_TILE = 512
TOK_TILE = 512
CHUNK = 64
QUAD = 4 * HEAD_DIM
SCAN_CHUNKS = 2
S5_SEQ_PAD = SUBLANES
S5_TILE = 64
S5_SPLIT = 2
PROJ_TOK_TILE = 1024
PROJ_COL_TILE = 1024
PROJ_U_OFF = 3584
PROJ_GATE_OFF = 4096
PROJ_COLS = 8192


def _cparams(sem):
    return pltpu.CompilerParams(dimension_semantics=sem, vmem_limit_bytes=VMEM_LIMIT)


def _dot(a, b):
    return jnp.dot(a.astype(BF16), b.astype(BF16), preferred_element_type=F32)


def _dot_nt(a, b):
    return lax.dot_general(a.astype(BF16), b.astype(BF16), (((1,), (1,)), ((), ())),
                           preferred_element_type=F32)


def _dot_tn(a, b):
    return lax.dot_general(a.astype(BF16), b.astype(BF16), (((0,), (0,)), ((), ())),
                           preferred_element_type=F32)


def _split_dot(x, w, pieces):
    acc = None
    rem = x
    for _ in range(pieces):
        part = rem.astype(BF16)
        term = jnp.dot(part, w, preferred_element_type=F32)
        acc = term if acc is None else acc + term
        rem = rem - part.astype(F32)
    return acc


def _sigmoid(x):
    return 1.0 / (1.0 + jnp.exp(-x))


def _rms(x, g):
    ms = jnp.mean(x * x, axis=-1, keepdims=True)
    return x * lax.rsqrt(ms + RMS_EPS) * g


def _ffn_kernel(*refs, n_x, n_out, first_tiles, final_norm):
    x_refs = refs[:n_x]
    g_ref, wg_ref, wu_ref, wd_ref, gf_ref = refs[n_x:n_x + 5]
    o_refs = refs[n_x + 5:n_x + 5 + n_out]
    xn_ref, acc_ref = refs[n_x + 5 + n_out:]
    i = pl.program_id(0)
    j = pl.program_id(1)
    in_first = i < first_tiles
    parts = [(in_first, x_refs[0], o_refs[0]), (jnp.logical_not(in_first), x_refs[-1], o_refs[-1])]
    if n_x == 1 and n_out == 1:
        parts = [(True, x_refs[0], o_refs[0])]

    for cond, x_ref, _ in parts:
        @pl.when((j == 0) & cond)
        def _(x_ref=x_ref):
            xn_ref[...] = _rms(x_ref[...], g_ref[...]).astype(BF16)
            acc_ref[...] = jnp.zeros_like(acc_ref)

    xn = xn_ref[...]
    hg = jnp.dot(xn, wg_ref[...], preferred_element_type=F32)
    hu = jnp.dot(xn, wu_ref[...], preferred_element_type=F32)
    act = (hg * _sigmoid(hg)) * hu
    acc_ref[...] += jnp.dot(act.astype(BF16), wd_ref[...], preferred_element_type=F32)

    for cond, x_ref, o_ref in parts:
        @pl.when((j == pl.num_programs(1) - 1) & cond)
        def _(x_ref=x_ref, o_ref=o_ref):
            y = x_ref[...] + 0.5 * acc_ref[...]
            o_ref[...] = _rms(y, gf_ref[...]) if final_norm else y


def _ffn(xs, g, wg, wu, wd, gf, *, out_rows, final_norm):
    d = xs[0].shape[1]
    f = wg.shape[1]
    tm, tf = TOK_TILE, FF_TILE
    n = sum(x.shape[0] for x in xs)
    assert n == sum(out_rows) and all(r % tm == 0 for r in out_rows)
    assert all(x.shape[0] % tm == 0 for x in xs)
    first_rows = xs[0].shape[0] if len(xs) == 2 else out_rows[0]
    if len(xs) == 2 and len(out_rows) == 2:
        assert xs[0].shape[0] == out_rows[0]
    first_tiles = first_rows // tm

    def split_specs(count):
        if count == 1:
            return [pl.BlockSpec((tm, d), lambda i, j: (i, 0))]
        return [pl.BlockSpec((tm, d), lambda i, j: (jnp.minimum(i, first_tiles - 1), 0)),
                pl.BlockSpec((tm, d), lambda i, j: (jnp.maximum(i - first_tiles, 0), 0))]

    outs = pl.pallas_call(
        functools.partial(_ffn_kernel, n_x=len(xs), n_out=len(out_rows), first_tiles=first_tiles,
                          final_norm=final_norm),
        grid=(n // tm, f // tf),
        in_specs=split_specs(len(xs)) + [
            pl.BlockSpec((1, d), lambda i, j: (0, 0)),
            pl.BlockSpec((d, tf), lambda i, j: (0, j)),
            pl.BlockSpec((d, tf), lambda i, j: (0, j)),
            pl.BlockSpec((tf, d), lambda i, j: (j, 0)),
            pl.BlockSpec((1, d), lambda i, j: (0, 0)),
        ],
        out_specs=split_specs(len(out_rows)),
        out_shape=[jax.ShapeDtypeStruct((r, d), F32) for r in out_rows],
        scratch_shapes=[pltpu.VMEM((tm, d), BF16), pltpu.VMEM((tm, d), F32)],
        compiler_params=_cparams(("arbitrary", "arbitrary")),
        name="ffn_final" if final_norm else "ffn",
    )(*xs, g, wg, wu, wd, gf)
    return outs


def _in_proj_kernel(x_ref, g_ref, w_ref, mix_ref, gate_ref, xn_ref, *, mix_tiles):
    j = pl.program_id(1)

    @pl.when(j == 0)
    def _():
        xn_ref[...] = _rms(x_ref[...], g_ref[...]).astype(BF16)

    out = jnp.dot(xn_ref[...], w_ref[...], preferred_element_type=F32)

    @pl.when(j < mix_tiles)
    def _():
        mix_ref[...] = out

    @pl.when(j >= mix_tiles)
    def _():
        gate_ref[...] = out.astype(BF16)


def _in_proj(x, g, w):
    n, d = x.shape
    c = w.shape[1]
    tm, tn = PROJ_TOK_TILE, PROJ_COL_TILE
    mix_tiles = PROJ_GATE_OFF // tn
    return pl.pallas_call(
        functools.partial(_in_proj_kernel, mix_tiles=mix_tiles),
        grid=(n // tm, c // tn),
        in_specs=[
            pl.BlockSpec((tm, d), lambda i, j: (i, 0)),
            pl.BlockSpec((1, d), lambda i, j: (0, 0)),
            pl.BlockSpec((d, tn), lambda i, j: (0, j)),
        ],
        out_specs=[pl.BlockSpec((tm, tn), lambda i, j: (i, jnp.minimum(j, mix_tiles - 1))),
                   pl.BlockSpec((tm, tn), lambda i, j: (i, jnp.maximum(j - mix_tiles, 0)))],
        out_shape=[jax.ShapeDtypeStruct((n, PROJ_GATE_OFF), F32),
                   jax.ShapeDtypeStruct((n, c - PROJ_GATE_OFF), BF16)],
        scratch_shapes=[pltpu.VMEM((tm, d), BF16)],
        compiler_params=_cparams(("arbitrary", "arbitrary")),
        name="proj_in",
    )(x, g, w)


PREP_TILE = 256


def _head_sum(x, ones_quad):
    return jnp.concatenate([_split_dot(x[:, c:c + QUAD], ones_quad, 2)
                            for c in range(0, x.shape[1], QUAD)], axis=1)


def _rwkv_prep_kernel(main_ref, prev_ref, next_ref, mu_ref, w2_ref, w0_ref, a2_ref, a0_ref, g2_ref,
                      kk_ref, ka_ref, rk_ref, ones_ref, tri_ref,
                      at_out, rt_out, bt_out, kt_out, wt_out, v_out, g_out, bonus_out):
    i = pl.program_id(1)
    tl = main_ref.shape[1]
    x = main_ref[0]
    row = lax.broadcasted_iota(jnp.int32, (tl, 1), 0)
    before = jnp.where(i == 0, 0.0, prev_ref[0, SUBLANES - 1:SUBLANES, :])
    after = jnp.where(i == pl.num_programs(1) - 1, 0.0, next_ref[0, 0:1, :])
    prev = jnp.where(row == 0, before, pltpu.roll(x, 1, 0))
    nxt = jnp.where(row == tl - 1, after, pltpu.roll(x, tl - 1, 0))
    p = x + (0.5 * (prev + nxt) - x) * mu_ref[...]

    c0 = 3 * RWKV_DIM
    r = p[:, 0:RWKV_DIM]
    k = p[:, RWKV_DIM:2 * RWKV_DIM]
    v = p[:, 2 * RWKV_DIM:c0]
    wlow = jnp.tanh(p[:, c0:c0 + 2 * LORA])
    alow = p[:, c0 + 2 * LORA:c0 + 4 * LORA]
    glow = p[:, c0 + 4 * LORA:c0 + 4 * LORA + GATE_LORA]
    ones_bd = ones_ref[...]

    wpre = _dot(wlow, w2_ref[...]) + w0_ref[...]
    lw = (-math.exp(-0.5) * math.log2(math.e)) * _sigmoid(wpre)
    a = _sigmoid(_dot(alow, a2_ref[...]) + a0_ref[...])

    kk = k * kk_ref[...]
    kkn = kk * lax.rsqrt(_head_sum(kk * kk, ones_bd) + KK_EPS)
    ka = ka_ref[...]
    ksum = None
    n_chunks = tl // CHUNK
    for d in range(2):
        a_d = a[:, d * RWKV_DIM:(d + 1) * RWKV_DIM]
        lw_d = lw[:, d * RWKV_DIM:(d + 1) * RWKV_DIM]
        kd = k * (1.0 + (a_d - 1.0) * ka)
        cum = _cumsum(tri_ref[d], lw_d)
        e_out = jnp.exp2(-cum)
        at_out[d, 0] = (-kkn * jnp.exp2(cum - lw_d)).astype(BF16)
        rt_out[d, 0] = (r * jnp.exp2(cum)).astype(BF16)
        bt_out[d, 0] = (kkn * a_d * e_out).astype(BF16)
        kt_out[d, 0] = (kd * e_out).astype(BF16)
        total = jnp.exp2(jnp.sum(lw_d.reshape(n_chunks, CHUNK, RWKV_DIM), axis=1))
        wt_out[d, 0] = jnp.broadcast_to(total[:, None, :], (n_chunks, SUBLANES, RWKV_DIM)).reshape(
            n_chunks * SUBLANES, RWKV_DIM)
        ksum = kd if ksum is None else ksum + kd
    v_out[0] = v.astype(BF16)
    g_out[0] = _dot(_sigmoid(glow), g2_ref[...])
    bonus_out[0] = _head_sum(r * ksum * rk_ref[...], ones_bd) * v


def _rwkv_prep(pshift, mu, w2cat, w0cat, a2cat, a0cat, g2, k_k, k_a, r_k, ones_bd):
    b, l, _ = pshift.shape
    c = SHIFT_COLS
    tl = PREP_TILE
    nt = l // tl
    hb = tl // SUBLANES
    nhb = l // SUBLANES
    full = lambda shape: pl.BlockSpec(shape, lambda bi, i: (0,) * len(shape))
    tok = pl.BlockSpec((1, tl, RWKV_DIM), lambda bi, i: (bi, i, 0))
    tok2 = pl.BlockSpec((2, 1, tl, RWKV_DIM), lambda bi, i: (0, bi, i, 0))
    wrows = tl // CHUNK * SUBLANES
    tokw = pl.BlockSpec((2, 1, wrows, RWKV_DIM), lambda bi, i: (0, bi, i, 0))
    s1 = jax.ShapeDtypeStruct((b, l, RWKV_DIM), F32)
    s1h = jax.ShapeDtypeStruct((b, l, RWKV_DIM), BF16)
    s2h = jax.ShapeDtypeStruct((2, b, l, RWKV_DIM), BF16)
    sw = jax.ShapeDtypeStruct((2, b, l // CHUNK * SUBLANES, RWKV_DIM), F32)
    pos = jnp.arange(tl)
    same_chunk = (pos[:, None] // CHUNK) == (pos[None, :] // CHUNK)
    tri = jnp.stack([same_chunk & (pos[None, :] <= pos[:, None]),
                     same_chunk & (pos[None, :] >= pos[:, None])]).astype(BF16)
    return pl.pallas_call(
        _rwkv_prep_kernel,
        grid=(b, nt),
        in_specs=[
            pl.BlockSpec((1, tl, c), lambda bi, i: (bi, i, 0)),
            pl.BlockSpec((1, SUBLANES, c), lambda bi, i: (bi, jnp.maximum(i * hb - 1, 0), 0)),
            pl.BlockSpec((1, SUBLANES, c), lambda bi, i: (bi, jnp.minimum((i + 1) * hb, nhb - 1), 0)),
            full((1, c)),
            full(w2cat.shape), full(w0cat.shape), full(a2cat.shape), full(a0cat.shape), full(g2.shape),
            full((1, RWKV_DIM)), full((1, RWKV_DIM)), full((1, RWKV_DIM)), full(ones_bd.shape),
            full(tri.shape),
        ],
        out_specs=[tok2, tok2, tok2, tok2, tokw, tok, tok, tok],
        out_shape=[s2h, s2h, s2h, s2h, sw, s1h, s1, s1],
        compiler_params=_cparams(("parallel", "arbitrary")),
        name="rwkv_prep",
    )(pshift, pshift, pshift, mu, w2cat, w0cat, a2cat, a0cat, g2, k_k, k_a, r_k, ones_bd, tri)


def _stack_heads(x, head_masks):
    return jnp.concatenate([jnp.where(m, x, 0.0) for m in head_masks], axis=0)


def _rwkv_scan_kernel(atf_ref, rtf_ref, btf_ref, ktf_ref, wtf_ref, vf_ref,
                      atb_ref, rtb_ref, btb_ref, ktb_ref, wtb_ref, vb_ref,
                      of_ref, ob_ref, s_ref):
    t = CHUNK

    @pl.when(pl.program_id(1) == 0)
    def _():
        s_ref[...] = jnp.zeros_like(s_ref)

    row = lax.broadcasted_iota(jnp.int32, (t, QUAD), 0)
    lane = lax.broadcasted_iota(jnp.int32, (t, QUAD), 1)
    col = lane & (HEAD_DIM - 1)
    same16 = (row >> 4) == (col >> 4)
    same32 = (row >> 5) == (col >> 5)
    in32_off16 = same32 & jnp.logical_not(same16)
    eye = jnp.where(col == row, 1.0, 0.0)
    head_masks = [(lane >> 6) == h for h in range(4)]
    strict = (col < row, col > row)
    incl = (col <= row, col >= row)
    qrow = lax.broadcasted_iota(jnp.int32, (QUAD, QUAD), 0) >> 6
    qcol = lax.broadcasted_iota(jnp.int32, (QUAD, QUAD), 1) >> 6
    same_head = qrow == qcol
    stack = lambda x: _stack_heads(x, head_masks)
    cat = lambda a, b: jnp.concatenate([a, b], axis=0)

    refs = ((atf_ref, rtf_ref, btf_ref, ktf_ref, wtf_ref, vf_ref, of_ref),
            (atb_ref, rtb_ref, btb_ref, ktb_ref, wtb_ref, vb_ref, ob_ref))
    n_sub = vf_ref.shape[1] // t
    chains = [(d, q, c) for c in range(n_sub) for q in range(RWKV_DIM // QUAD) for d in range(2)]
    each = lambda fn, *lists: [fn(*args) for args in zip(*lists)]
    sl = lambda q: slice(q * QUAD, (q + 1) * QUAD)

    def chunk(d, c):
        return c if d == 0 else n_sub - 1 - c

    def rows(d, c):
        return slice(chunk(d, c) * t, (chunk(d, c) + 1) * t)

    load = lambda j: [refs[d][j][0, 0, rows(d, c), sl(q)].astype(F32) for d, q, c in chains]
    at, rt, bt, kt = load(0), load(1), load(2), load(3)
    v = [refs[d][5][0, rows(d, c), sl(q)].astype(F32) for d, q, c in chains]
    w_total = [refs[d][4][0, 0, chunk(d, c) * SUBLANES:chunk(d, c) * SUBLANES + 1, sl(q)]
               for d, q, c in chains]
    lhs = each(cat, at, rt)
    bk = each(cat, bt, kt)

    gram = each(lambda l_, b_, k_: _dot_nt(l_, cat(stack(b_), stack(k_))), lhs, bt, kt)
    a_ab = [jnp.where(strict[d], g[:t, :QUAD], 0.0) for (d, _, _), g in zip(chains, gram)]
    a_ak = [jnp.where(strict[d], g[:t, QUAD:], 0.0) for (d, _, _), g in zip(chains, gram)]
    a_rb = [jnp.where(incl[d], g[t:, :QUAD], 0.0) for (d, _, _), g in zip(chains, gram)]
    a_rk = [jnp.where(incl[d], g[t:, QUAD:], 0.0) for (d, _, _), g in zip(chains, gram)]
    from_v = each(lambda p, q_, v_: _dot(cat(p, q_), stack(v_)), a_ak, a_rk, v)

    a_d = each(lambda a: jnp.where(same16, a, 0.0), a_ab)
    pw = each(lambda a: eye + a, a_d)
    sq = each(lambda a: _dot(a, stack(a)), a_d)
    for _ in range(2):
        both = each(lambda p, s: _dot(cat(p, s), stack(s)), pw, sq)
        pw = each(lambda p, bo: p + bo[:t], pw, both)
        sq = each(lambda bo: bo[t:], both)
    x0 = each(lambda p, s: p + _dot(p, stack(s)), pw, sq)
    m1 = each(lambda x, a: _dot(x, stack(jnp.where(in32_off16, a, 0.0))), x0, a_ab)
    x1 = each(lambda x, m: x + _dot(m, stack(x)), x0, m1)
    m2 = each(lambda x, a: _dot(x, stack(jnp.where(same32, 0.0, a))), x1, a_ab)
    x2 = each(lambda x, m: x + _dot(m, stack(x)), x1, m2)

    per_chunk = len(chains) // n_sub
    state = [s_ref[d, q] for d, q, _ in chains[:per_chunk]]
    for c in range(n_sub):
        pick = lambda xs: xs[c * per_chunk:(c + 1) * per_chunk]
        from_state = each(_dot_nt, pick(lhs), state)
        u = each(lambda x, fs, fv: _dot(x, stack(fs[:t] + fv[:t])), pick(x2), from_state, pick(from_v))
        y = each(lambda fs, fv, a, u_: fs[t:] + fv[t:] + _dot(a, stack(u_)),
                 from_state, pick(from_v), pick(a_rb), u)
        upd = each(lambda u_, v_, bk_: _dot_tn(cat(u_, v_), bk_), u, pick(v), pick(bk))
        for (d, q, _), y_ in zip(pick(chains), y):
            refs[d][6][0, rows(d, c), sl(q)] = y_
        state = each(lambda s_, up, w: (s_ + jnp.where(same_head, up, 0.0)) * w, state, upd, pick(w_total))
    for (d, q, _), s_ in zip(chains[:per_chunk], state):
        s_ref[d, q] = s_


def _cumsum(tri, lw):
    acc = None
    rem = lw
    for _ in range(3):
        part = rem.astype(BF16)
        term = jnp.dot(tri, part, preferred_element_type=F32)
        acc = term if acc is None else acc + term
        rem = rem - part.astype(F32)
    return acc


def _rwkv_scan(at, rt, bt, kt, wt, v):
    b, l, c = v.shape
    tl = SCAN_CHUNKS * CHUNK
    nc = l // tl
    wrows = SCAN_CHUNKS * SUBLANES
    fwd = pl.BlockSpec((1, tl, c), lambda bi, i: (bi, i, 0))
    bwd = pl.BlockSpec((1, tl, c), lambda bi, i: (bi, nc - 1 - i, 0))
    fwd_d = pl.BlockSpec((1, 1, tl, c), lambda bi, i: (0, bi, i, 0))
    bwd_d = pl.BlockSpec((1, 1, tl, c), lambda bi, i: (1, bi, nc - 1 - i, 0))
    fwd_w = pl.BlockSpec((1, 1, wrows, c), lambda bi, i: (0, bi, i, 0))
    bwd_w = pl.BlockSpec((1, 1, wrows, c), lambda bi, i: (1, bi, nc - 1 - i, 0))
    return pl.pallas_call(
        _rwkv_scan_kernel,
        grid=(b, nc),
        in_specs=[fwd_d, fwd_d, fwd_d, fwd_d, fwd_w, fwd, bwd_d, bwd_d, bwd_d, bwd_d, bwd_w, bwd],
        out_specs=[fwd, bwd],
        out_shape=[jax.ShapeDtypeStruct((b, l, c), F32)] * 2,
        scratch_shapes=[pltpu.VMEM((2, c // QUAD, QUAD, QUAD), F32)],
        compiler_params=_cparams(("parallel", "arbitrary")),
        name="rwkv_scan",
    )(at, rt, bt, kt, wt, v, at, rt, bt, kt, wt, v)


def _s5_scan_kernel(u_ref, bdense_ref, cdense_ref, lam_ref, o_ref, ubuf_ref, st_ref, s_ref):
    d = pl.program_id(0)
    n_seq, tl = u_ref.shape[0], u_ref.shape[1]
    ns = S5_STATES

    @pl.when(pl.program_id(1) == 0)
    def _():
        s_ref[...] = jnp.zeros_like(s_ref)
        ubuf_ref[...] = jnp.zeros_like(ubuf_ref)

    rows = tl * S5_SEQ_PAD
    hs = ns // S5_SPLIT
    hc = S5_DIM // S5_SPLIT
    for b in range(n_seq):
        ubuf_ref[:, b, :] = u_ref[b]
    u = ubuf_ref[...].reshape(rows, S5_DIM)
    for h in range(S5_SPLIT):
        bu = _dot(u[:, h * hc:(h + 1) * hc], bdense_ref[0, h])
        st_ref[:, :, h * hs:(h + 1) * hs] = bu[:, :hs].reshape(tl, S5_SEQ_PAD, hs)
        st_ref[:, :, ns + h * hs:ns + (h + 1) * hs] = bu[:, hs:].reshape(tl, S5_SEQ_PAD, hs)
    lam_re = lam_ref[0, 0]
    lam_im = lam_ref[0, 1]

    def step(j, carry):
        s_re, s_im = carry
        tt = jnp.where(d == 0, j, tl - 1 - j)
        n_re = lam_re * s_re - lam_im * s_im + st_ref[tt, :, :ns]
        n_im = lam_re * s_im + lam_im * s_re + st_ref[tt, :, ns:]
        st_ref[tt, :, :ns] = n_re
        st_ref[tt, :, ns:] = n_im
        return n_re, n_im

    s_re, s_im = lax.fori_loop(0, tl, step, (s_ref[0], s_ref[1]), unroll=2)
    s_ref[0] = s_re
    s_ref[1] = s_im
    for h in range(S5_SPLIT):
        st_re = st_ref[:, :, h * hs:(h + 1) * hs].reshape(rows, hs)
        st_im = st_ref[:, :, ns + h * hs:ns + (h + 1) * hs].reshape(rows, hs)
        y = _dot(st_re, cdense_ref[h, :hs]) + _dot(st_im, cdense_ref[h, hs:])
        o_ref[0, :, :, h * hc:(h + 1) * hc] = y.reshape(tl, S5_SEQ_PAD, hc)


def _s5_scan(proj, bdense, cdense, lam):
    b, l, _ = proj.shape
    assert b <= S5_SEQ_PAD and PROJ_U_OFF % S5_DIM == 0
    tl = S5_TILE
    nt = l // tl
    pos = lambda dd, i: i + dd * (nt - 1 - 2 * i)
    return pl.pallas_call(
        _s5_scan_kernel,
        grid=(2, nt),
        in_specs=[
            pl.BlockSpec((b, tl, S5_DIM), lambda dd, i: (0, pos(dd, i), PROJ_U_OFF // S5_DIM)),
            pl.BlockSpec((1,) + bdense.shape[1:], lambda dd, i: (dd, 0, 0, 0)),
            pl.BlockSpec(cdense.shape, lambda dd, i: (0, 0, 0)),
            pl.BlockSpec((1, 2, S5_SEQ_PAD, S5_STATES), lambda dd, i: (dd, 0, 0, 0)),
        ],
        out_specs=pl.BlockSpec((1, tl, S5_SEQ_PAD, S5_DIM), lambda dd, i: (dd, pos(dd, i), 0, 0)),
        out_shape=jax.ShapeDtypeStruct((2, l, S5_SEQ_PAD, S5_DIM), F32),
        scratch_shapes=[pltpu.VMEM((tl, S5_SEQ_PAD, S5_DIM), F32),
                        pltpu.VMEM((tl, S5_SEQ_PAD, 2 * S5_STATES), F32),
                        pltpu.VMEM((2, S5_SEQ_PAD, S5_STATES), F32)],
        compiler_params=_cparams(("parallel", "arbitrary")),
        name="s5_scan",
    )(proj, bdense, cdense, lam)


def _s5_post_kernel(yf_ref, yb_ref, u_ref, dskip_ref, wglu_ref, bglu_ref, o_ref):
    n_seq, tl = u_ref.shape[0], u_ref.shape[1]
    per_seq = lambda ref, b: ref[0, :, b, :]
    y = jnp.concatenate([per_seq(yf_ref, b) + per_seq(yb_ref, b) + dskip_ref[...] * u_ref[b]
                         for b in range(n_seq)], axis=0)
    y = 0.5 * y * (1.0 + jnp.tanh(math.sqrt(2.0 / math.pi) * (y + 0.044715 * (y * y * y))))
    gate = _sigmoid(_dot(y, wglu_ref[...]) + bglu_ref[...])
    o_ref[...] = (y * gate).astype(BF16).reshape(n_seq, tl, S5_DIM)


def _s5_post(y2, proj, d_skip, w_glu, b_glu):
    b, l, _ = proj.shape
    tl = S5_TILE
    row = pl.BlockSpec((1, S5_DIM), lambda i: (0, 0))
    return pl.pallas_call(
        _s5_post_kernel,
        grid=(l // tl,),
        in_specs=[
            pl.BlockSpec((1, tl, S5_SEQ_PAD, S5_DIM), lambda i: (0, i, 0, 0)),
            pl.BlockSpec((1, tl, S5_SEQ_PAD, S5_DIM), lambda i: (1, i, 0, 0)),
            pl.BlockSpec((b, tl, S5_DIM), lambda i: (0, i, PROJ_U_OFF // S5_DIM)),
            row,
            pl.BlockSpec((S5_DIM, S5_DIM), lambda i: (0, 0)),
            row,
        ],
        out_specs=pl.BlockSpec((b, tl, S5_DIM), lambda i: (0, i, 0)),
        out_shape=jax.ShapeDtypeStruct((b, l, S5_DIM), BF16),
        compiler_params=_cparams(("parallel",)),
        name="s5_post",
    )(y2, y2, proj, d_skip, w_glu, b_glu)


def _s5_params(a_re, a_im, log_step, b_re, b_im, c_re, c_im):
    dt = jnp.exp(log_step)[..., None]
    z_re, z_im = a_re * dt, a_im * dt
    mag = jnp.exp(z_re)
    lam_re, lam_im = mag * jnp.cos(z_im), mag * jnp.sin(z_im)
    den = a_re * a_re + a_im * a_im
    q_re = ((lam_re - 1.0) * a_re + lam_im * a_im) / den
    q_im = (lam_im * a_re - (lam_re - 1.0) * a_im) / den
    bb_re = q_re[..., None] * b_re - q_im[..., None] * b_im
    bb_im = q_re[..., None] * b_im + q_im[..., None] * b_re
    gb = S5_GROUPS // S5_SPLIT
    eye = jnp.eye(gb, dtype=F32)

    def dense_b(x):
        x = x.reshape(2, S5_SPLIT, gb, S5_STATE, S5_CH)
        return jnp.einsum('dhgpc,gk->dhgckp', x, eye).reshape(2, S5_SPLIT, gb * S5_CH, gb * S5_STATE)

    def dense_c(x):
        x = x.reshape(S5_SPLIT, gb, S5_CH, S5_STATE)
        return jnp.einsum('hgcp,gk->hgpkc', x, eye).reshape(S5_SPLIT, gb * S5_STATE, gb * S5_CH)

    bdense = jnp.concatenate([dense_b(bb_re), dense_b(bb_im)], axis=-1).astype(BF16)
    cdense = jnp.concatenate([dense_c(c_re), -dense_c(c_im)], axis=1).astype(BF16)
    lam = jnp.stack([lam_re.reshape(2, S5_STATES), lam_im.reshape(2, S5_STATES)], axis=1)
    lam = jnp.broadcast_to(lam[:, :, None, :], (2, 2, S5_SEQ_PAD, S5_STATES))
    return bdense, cdense, lam


MERGE_TILE = 256


def _merge_kernel(x_ref, yf_ref, yb_ref, bonus_ref, g_ref, ys_ref, gate_ref,
                  lnw_ref, lnb_ref, ones_ref, pr_ref, ps_ref, wo_ref, o_ref):
    ones_bd = ones_ref[...]
    y = yf_ref[...] + yb_ref[...]
    mean = _head_sum(y, ones_bd) * (1.0 / HEAD_DIM)
    yc = y - mean
    var = _head_sum(yc * yc, ones_bd) * (1.0 / HEAD_DIM)
    yn = yc * lax.rsqrt(var + GN_EPS) * lnw_ref[...] + lnb_ref[...]
    yr = (yn + bonus_ref[...]) * g_ref[...]
    y_rwkv = _dot(yr, pr_ref[...])
    y_s5 = jnp.dot(ys_ref[...], ps_ref[...], preferred_element_type=F32)
    gates = _sigmoid(gate_ref[...].astype(F32))
    merged = gates[:, :D_MODEL] * y_rwkv + gates[:, D_MODEL:] * y_s5
    o_ref[...] = x_ref[...] + _dot(merged, wo_ref[...])


def _merge(x, yf, yb, bonus, g, ys, gates, ln_w, ln_b, ones_bd, proj_rwkv, proj_s5, w_out):
    n, d = x.shape
    tm = MERGE_TILE
    assert gates.shape[1] == 2 * d
    tok = lambda c: pl.BlockSpec((tm, c), lambda i: (i, 0))
    const = lambda a: pl.BlockSpec(a.shape, lambda i: (0, 0), pipeline_mode=pl.Buffered(1))
    return pl.pallas_call(
        _merge_kernel,
        grid=(n // tm,),
        in_specs=[tok(d), tok(RWKV_DIM), tok(RWKV_DIM), tok(RWKV_DIM), tok(RWKV_DIM), tok(S5_DIM),
                  tok(2 * d),
                  const(ln_w), const(ln_b), const(ones_bd), const(proj_rwkv), const(proj_s5), const(w_out)],
        out_specs=tok(d),
        out_shape=jax.ShapeDtypeStruct((n, d), F32),
        compiler_params=_cparams(("parallel",)),
        name="merge",
    )(x, yf, yb, bonus, g, ys, gates, ln_w, ln_b, ones_bd, proj_rwkv, proj_s5, w_out)


def _cast_pad_kernel(x_ref, o_ref, *, valid_tiles):
    cols = x_ref.shape[1]

    @pl.when(pl.program_id(0) < valid_tiles)
    def _():
        o_ref[:, :cols] = x_ref[...].astype(BF16)
        if o_ref.shape[1] > cols:
            o_ref[:, cols:] = jnp.zeros((o_ref.shape[0], o_ref.shape[1] - cols), BF16)

    @pl.when(pl.program_id(0) >= valid_tiles)
    def _():
        o_ref[...] = jnp.zeros_like(o_ref)


def _pad_ff(w, axis):
    rows, cols = w.shape
    pad = (-w.shape[axis]) % FF_TILE
    out_rows, out_cols = (rows + pad, cols) if axis == 0 else (rows, cols + pad)
    tr = LANES if axis == 0 else 2 * LANES
    assert rows % tr == 0 and out_rows % tr == 0 and cols % LANES == 0
    valid_tiles = rows // tr
    return pl.pallas_call(
        functools.partial(_cast_pad_kernel, valid_tiles=valid_tiles),
        grid=(out_rows // tr,),
        in_specs=[pl.BlockSpec((tr, cols), lambda i: (jnp.minimum(i, valid_tiles - 1), 0))],
        out_specs=pl.BlockSpec((tr, out_cols), lambda i: (i, 0)),
        out_shape=jax.ShapeDtypeStruct((out_rows, out_cols), BF16),
        compiler_params=_cparams(("parallel",)),
        name="cast_pad",
    )(w)


def _proj_weight_kernel(w_ref, o_ref):
    o_ref[:, :SHIFT_COLS] = w_ref[:, :SHIFT_COLS].astype(BF16)
    o_ref[:, SHIFT_COLS:PROJ_U_OFF] = jnp.zeros((o_ref.shape[0], PROJ_U_OFF - SHIFT_COLS), BF16)
    o_ref[:, PROJ_U_OFF:] = w_ref[:, SHIFT_COLS:].astype(BF16)


def _proj_weight(w_in):
    d, cols = w_in.shape
    assert cols + PROJ_U_OFF - SHIFT_COLS == PROJ_COLS
    tr = 2 * LANES
    return pl.pallas_call(
        _proj_weight_kernel,
        grid=(d // tr,),
        in_specs=[pl.BlockSpec((tr, cols), lambda i: (i, 0))],
        out_specs=pl.BlockSpec((tr, PROJ_COLS), lambda i: (i, 0)),
        out_shape=jax.ShapeDtypeStruct((d, PROJ_COLS), BF16),
        compiler_params=_cparams(("parallel",)),
        name="proj_weight",
    )(w_in)


def _lora_cat(w):
    z = jnp.zeros_like(w[0])
    return jnp.concatenate([jnp.concatenate([w[0], z], axis=1),
                            jnp.concatenate([z, w[1]], axis=1)], axis=0).astype(BF16)


def _forward(x_a, x_b, norm_ffn1, ffn1_w_gate, ffn1_w_up, ffn1_w_down, norm_mix, w_in, shift_mu,
             rwkv_w0, rwkv_w2, rwkv_a0, rwkv_a2, rwkv_g2, rwkv_k_k, rwkv_k_a, rwkv_r_k,
             rwkv_ln_w, rwkv_ln_b, s5_a_re, s5_a_im, s5_log_step, s5_b_re, s5_b_im,
             s5_c_re, s5_c_im, s5_d, s5_w_glu, s5_b_glu, proj_rwkv, proj_s5, w_out,
             norm_ffn2, ffn2_w_gate, ffn2_w_up, ffn2_w_down, norm_final):
    l, d = x_a.shape[1:]
    b = x_a.shape[0] + x_b.shape[0]
    n_a, n_b = x_a.shape[0] * l, x_b.shape[0] * l
    n = n_a + n_b
    row = lambda p: p.reshape(1, -1)

    (x1,) = _ffn([x_a.reshape(n_a, d), x_b.reshape(n_b, d)], row(norm_ffn1), _pad_ff(ffn1_w_gate, 1),
                 _pad_ff(ffn1_w_up, 1), _pad_ff(ffn1_w_down, 0), row(norm_final),
                 out_rows=[n], final_norm=False)

    assert w_in.shape[1] == SHIFT_COLS + S5_DIM + 2 * d
    w_proj = _proj_weight(w_in)
    proj, gates = _in_proj(x1, row(norm_mix), w_proj)
    proj3 = proj.reshape(b, l, PROJ_GATE_OFF)

    head_id = jnp.arange(QUAD) // HEAD_DIM
    ones_bd = (head_id[:, None] == head_id[None, :]).astype(BF16)
    at, rt, bt, kt, wt, v, g, bonus = _rwkv_prep(
        proj3, row(shift_mu),
        _lora_cat(rwkv_w2), rwkv_w0.reshape(1, -1), _lora_cat(rwkv_a2), rwkv_a0.reshape(1, -1),
        rwkv_g2.astype(BF16), row(rwkv_k_k), row(rwkv_k_a), row(rwkv_r_k), ones_bd)
    yf, yb = _rwkv_scan(at, rt, bt, kt, wt, v)

    bdense, cdense, lam = _s5_params(s5_a_re, s5_a_im, s5_log_step, s5_b_re, s5_b_im, s5_c_re, s5_c_im)
    ys2 = _s5_scan(proj3, bdense, cdense, lam)
    ys = _s5_post(ys2, proj3, row(s5_d), s5_w_glu.astype(BF16), row(s5_b_glu)).reshape(n, S5_DIM)

    flat = lambda a: a.reshape(n, RWKV_DIM)
    x2 = _merge(x1, flat(yf), flat(yb), flat(bonus), flat(g), ys, gates, row(rwkv_ln_w), row(rwkv_ln_b),
                ones_bd, proj_rwkv.astype(BF16), proj_s5.astype(BF16), w_out.astype(BF16))
    y_a, y_b = _ffn([x2], row(norm_ffn2), _pad_ff(ffn2_w_gate, 1), _pad_ff(ffn2_w_up, 1),
                    _pad_ff(ffn2_w_down, 0), row(norm_final), out_rows=[n_a, n_b], final_norm=True)
    return y_a.reshape(x_a.shape), y_b.reshape(x_b.shape)


def kernel(x_prompt, x_sample, norm_ffn1, ffn1_w_gate, ffn1_w_up, ffn1_w_down, norm_mix, w_in, shift_mu, rwkv_w0, rwkv_w2, rwkv_a0, rwkv_a2, rwkv_g2, rwkv_k_k, rwkv_k_a, rwkv_r_k, rwkv_ln_w, rwkv_ln_b, s5_a_re, s5_a_im, s5_log_step, s5_b_re, s5_b_im, s5_c_re, s5_c_im, s5_d, s5_w_glu, s5_b_glu, proj_rwkv, proj_s5, w_out, norm_ffn2, ffn2_w_gate, ffn2_w_up, ffn2_w_down, norm_final):
    layer = (norm_ffn1, ffn1_w_gate, ffn1_w_up, ffn1_w_down, norm_mix, w_in, shift_mu,
             rwkv_w0, rwkv_w2, rwkv_a0, rwkv_a2, rwkv_g2, rwkv_k_k, rwkv_k_a, rwkv_r_k,
             rwkv_ln_w, rwkv_ln_b, s5_a_re, s5_a_im, s5_log_step, s5_b_re, s5_b_im,
             s5_c_re, s5_c_im, s5_d, s5_w_glu, s5_b_glu, proj_rwkv, proj_s5, w_out,
             norm_ffn2, ffn2_w_gate, ffn2_w_up, ffn2_w_down)
    assert all(p.shape[0] == 1 for p in layer), "single-layer block"
    assert x_prompt.shape[1:] == x_sample.shape[1:]
    return _forward(x_prompt, x_sample, *[p[0] for p in layer], norm_final)
```

```python
import functools
import math

import jax
import jax.numpy as jnp
from jax import lax
from jax.experimental import pallas as pl
from jax.experimental.pallas import tpu as pltpu

F32 = jnp.float32
BF16 = jnp.bfloat16

D_MODEL = 2048
D_FF = 5504
HEADS = 16
HEAD_DIM = 64
RWKV_DIM = HEADS * HEAD_DIM
LORA = 64
GATE_LORA = 128
S5_GROUPS = 32
S5_CH = 16
S5_DIM = S5_GROUPS * S5_CH
S5_STATE = 64
S5_STATES = S5_GROUPS * S5_STATE
SHIFT_COLS = 3 * RWKV_DIM + 4 * LORA + GATE_LORA
RMS_EPS = 1e-6
GN_EPS = 64e-5
KK_EPS = 1e-12

LANES = 128
SUBLANES = 8
VMEM_LIMIT = 56 * 1024 * 1024

FF_TILE = 512
TOK_TILE = 512
CHUNK = 64
QUAD = 4 * HEAD_DIM
SCAN_CHUNKS = 4
S5_SEQ_PAD = SUBLANES
S5_TILE = 64
S5_SPLIT = 2
PROJ_TOK_TILE = 1024
PROJ_COL_TILE = 1024
PROJ_U_OFF = 3584
PROJ_GATE_OFF = 4096
PROJ_COLS = 8192


def _cparams(sem):
    return pltpu.CompilerParams(dimension_semantics=sem, vmem_limit_bytes=VMEM_LIMIT)


def _dot(a, b):
    return jnp.dot(a.astype(BF16), b.astype(BF16), preferred_element_type=F32)


def _dot_nt(a, b):
    return lax.dot_general(a.astype(BF16), b.astype(BF16), (((1,), (1,)), ((), ())),
                           preferred_element_type=F32)


def _dot_tn(a, b):
    return lax.dot_general(a.astype(BF16), b.astype(BF16), (((0,), (0,)), ((), ())),
                           preferred_element_type=F32)


def _split_dot(x, w, pieces):
    acc = None
    rem = x
    for _ in range(pieces):
        part = rem.astype(BF16)
        term = jnp.dot(part, w, preferred_element_type=F32)
        acc = term if acc is None else acc + term
        rem = rem - part.astype(F32)
    return acc


def _sigmoid(x):
    return 1.0 / (1.0 + jnp.exp(-x))


def _rms(x, g):
    ms = jnp.mean(x * x, axis=-1, keepdims=True)
    return x * lax.rsqrt(ms + RMS_EPS) * g


def _ffn_kernel(*refs, n_x, n_out, first_tiles, final_norm):
    x_refs = refs[:n_x]
    g_ref, wg_ref, wu_ref, wd_ref, gf_ref = refs[n_x:n_x + 5]
    o_refs = refs[n_x + 5:n_x + 5 + n_out]
    xn_ref, acc_ref = refs[n_x + 5 + n_out:]
    i = pl.program_id(0)
    j = pl.program_id(1)
    in_first = i < first_tiles
    parts = [(in_first, x_refs[0], o_refs[0]), (jnp.logical_not(in_first), x_refs[-1], o_refs[-1])]
    if n_x == 1 and n_out == 1:
        parts = [(True, x_refs[0], o_refs[0])]

    for cond, x_ref, _ in parts:
        @pl.when((j == 0) & cond)
        def _(x_ref=x_ref):
            xn_ref[...] = _rms(x_ref[...], g_ref[...]).astype(BF16)
            acc_ref[...] = jnp.zeros_like(acc_ref)

    xn = xn_ref[...]
    hg = jnp.dot(xn, wg_ref[...], preferred_element_type=F32)
    hu = jnp.dot(xn, wu_ref[...], preferred_element_type=F32)
    act = (hg * _sigmoid(hg)) * hu
    acc_ref[...] += jnp.dot(act.astype(BF16), wd_ref[...], preferred_element_type=F32)

    for cond, x_ref, o_ref in parts:
        @pl.when((j == pl.num_programs(1) - 1) & cond)
        def _(x_ref=x_ref, o_ref=o_ref):
            y = x_ref[...] + 0.5 * acc_ref[...]
            o_ref[...] = _rms(y, gf_ref[...]) if final_norm else y


def _ffn(xs, g, wg, wu, wd, gf, *, out_rows, final_norm):
    d = xs[0].shape[1]
    f = wg.shape[1]
    tm, tf = TOK_TILE, FF_TILE
    n = sum(x.shape[0] for x in xs)
    assert n == sum(out_rows) and all(r % tm == 0 for r in out_rows)
    assert all(x.shape[0] % tm == 0 for x in xs)
    first_rows = xs[0].shape[0] if len(xs) == 2 else out_rows[0]
    if len(xs) == 2 and len(out_rows) == 2:
        assert xs[0].shape[0] == out_rows[0]
    first_tiles = first_rows // tm

    def split_specs(count):
        if count == 1:
            return [pl.BlockSpec((tm, d), lambda i, j: (i, 0))]
        return [pl.BlockSpec((tm, d), lambda i, j: (jnp.minimum(i, first_tiles - 1), 0)),
                pl.BlockSpec((tm, d), lambda i, j: (jnp.maximum(i - first_tiles, 0), 0))]

    outs = pl.pallas_call(
        functools.partial(_ffn_kernel, n_x=len(xs), n_out=len(out_rows), first_tiles=first_tiles,
                          final_norm=final_norm),
        grid=(n // tm, f // tf),
        in_specs=split_specs(len(xs)) + [
            pl.BlockSpec((1, d), lambda i, j: (0, 0)),
            pl.BlockSpec((d, tf), lambda i, j: (0, j)),
            pl.BlockSpec((d, tf), lambda i, j: (0, j)),
            pl.BlockSpec((tf, d), lambda i, j: (j, 0)),
            pl.BlockSpec((1, d), lambda i, j: (0, 0)),
        ],
        out_specs=split_specs(len(out_rows)),
        out_shape=[jax.ShapeDtypeStruct((r, d), F32) for r in out_rows],
        scratch_shapes=[pltpu.VMEM((tm, d), BF16), pltpu.VMEM((tm, d), F32)],
        compiler_params=_cparams(("arbitrary", "arbitrary")),
        name="ffn_final" if final_norm else "ffn",
    )(*xs, g, wg, wu, wd, gf)
    return outs


def _in_proj_kernel(x_ref, g_ref, w_ref, o_ref, xn_ref):
    @pl.when(pl.program_id(1) == 0)
    def _():
        xn_ref[...] = _rms(x_ref[...], g_ref[...]).astype(BF16)

    o_ref[...] = jnp.dot(xn_ref[...], w_ref[...], preferred_element_type=F32)


def _in_proj(x, g, w):
    n, d = x.shape
    c = w.shape[1]
    tm, tn = PROJ_TOK_TILE, PROJ_COL_TILE
    return pl.pallas_call(
        _in_proj_kernel,
        grid=(n // tm, c // tn),
        in_specs=[
            pl.BlockSpec((tm, d), lambda i, j: (i, 0)),
            pl.BlockSpec((1, d), lambda i, j: (0, 0)),
            pl.BlockSpec((d, tn), lambda i, j: (0, j)),
        ],
        out_specs=pl.BlockSpec((tm, tn), lambda i, j: (i, j)),
        out_shape=jax.ShapeDtypeStruct((n, c), F32),
        scratch_shapes=[pltpu.VMEM((tm, d), BF16)],
        compiler_params=_cparams(("parallel", "arbitrary")),
        name="proj_in",
    )(x, g, w)


PREP_TILE = 256


def _head_sum(x, ones_quad):
    return jnp.concatenate([_split_dot(x[:, c:c + QUAD], ones_quad, 2)
                            for c in range(0, x.shape[1], QUAD)], axis=1)


def _rwkv_prep_kernel(main_ref, prev_ref, next_ref, mu_ref, w2_ref, w0_ref, a2_ref, a0_ref, g2_ref,
                      kk_ref, ka_ref, rk_ref, ones_ref, tri_ref,
                      at_out, rt_out, bt_out, kt_out, wt_out, v_out, g_out, bonus_out):
    i = pl.program_id(1)
    tl = main_ref.shape[1]
    x = main_ref[0]
    row = lax.broadcasted_iota(jnp.int32, (tl, 1), 0)
    before = jnp.where(i == 0, 0.0, prev_ref[0, SUBLANES - 1:SUBLANES, :])
    after = jnp.where(i == pl.num_programs(1) - 1, 0.0, next_ref[0, 0:1, :])
    prev = jnp.where(row == 0, before, pltpu.roll(x, 1, 0))
    nxt = jnp.where(row == tl - 1, after, pltpu.roll(x, tl - 1, 0))
    p = x + (0.5 * (prev + nxt) - x) * mu_ref[...]

    c0 = 3 * RWKV_DIM
    r = p[:, 0:RWKV_DIM]
    k = p[:, RWKV_DIM:2 * RWKV_DIM]
    v = p[:, 2 * RWKV_DIM:c0]
    wlow = jnp.tanh(p[:, c0:c0 + 2 * LORA])
    alow = p[:, c0 + 2 * LORA:c0 + 4 * LORA]
    glow = p[:, c0 + 4 * LORA:c0 + 4 * LORA + GATE_LORA]
    ones_bd = ones_ref[...]

    wpre = _dot(wlow, w2_ref[...]) + w0_ref[...]
    lw = (-math.exp(-0.5) * math.log2(math.e)) * _sigmoid(wpre)
    a = _sigmoid(_dot(alow, a2_ref[...]) + a0_ref[...])

    kk = k * kk_ref[...]
    kkn = kk * lax.rsqrt(_head_sum(kk * kk, ones_bd) + KK_EPS)
    ka = ka_ref[...]
    ksum = None
    n_chunks = tl // CHUNK
    for d in range(2):
        a_d = a[:, d * RWKV_DIM:(d + 1) * RWKV_DIM]
        lw_d = lw[:, d * RWKV_DIM:(d + 1) * RWKV_DIM]
        kd = k * (1.0 + (a_d - 1.0) * ka)
        cum = _cumsum(tri_ref[d], lw_d)
        e_out = jnp.exp2(-cum)
        at_out[d, 0] = (-kkn * jnp.exp2(cum - lw_d)).astype(BF16)
        rt_out[d, 0] = (r * jnp.exp2(cum)).astype(BF16)
        bt_out[d, 0] = (kkn * a_d * e_out).astype(BF16)
        kt_out[d, 0] = (kd * e_out).astype(BF16)
        total = jnp.exp2(jnp.sum(lw_d.reshape(n_chunks, CHUNK, RWKV_DIM), axis=1))
        wt_out[d, 0] = jnp.broadcast_to(total[:, None, :], (n_chunks, SUBLANES, RWKV_DIM)).reshape(
            n_chunks * SUBLANES, RWKV_DIM)
        ksum = kd if ksum is None else ksum + kd
    v_out[0] = v.astype(BF16)
    g_out[0] = _dot(_sigmoid(glow), g2_ref[...])
    bonus_out[0] = _head_sum(r * ksum * rk_ref[...], ones_bd) * v


def _rwkv_prep(pshift, mu, w2cat, w0cat, a2cat, a0cat, g2, k_k, k_a, r_k, ones_bd):
    b, l, _ = pshift.shape
    c = SHIFT_COLS
    tl = PREP_TILE
    nt = l // tl
    hb = tl // SUBLANES
    nhb = l // SUBLANES
    full = lambda shape: pl.BlockSpec(shape, lambda bi, i: (0,) * len(shape))
    tok = pl.BlockSpec((1, tl, RWKV_DIM), lambda bi, i: (bi, i, 0))
    tok2 = pl.BlockSpec((2, 1, tl, RWKV_DIM), lambda bi, i: (0, bi, i, 0))
    wrows = tl // CHUNK * SUBLANES
    tokw = pl.BlockSpec((2, 1, wrows, RWKV_DIM), lambda bi, i: (0, bi, i, 0))
    s1 = jax.ShapeDtypeStruct((b, l, RWKV_DIM), F32)
    s1h = jax.ShapeDtypeStruct((b, l, RWKV_DIM), BF16)
    s2h = jax.ShapeDtypeStruct((2, b, l, RWKV_DIM), BF16)
    sw = jax.ShapeDtypeStruct((2, b, l // CHUNK * SUBLANES, RWKV_DIM), F32)
    pos = jnp.arange(tl)
    same_chunk = (pos[:, None] // CHUNK) == (pos[None, :] // CHUNK)
    tri = jnp.stack([same_chunk & (pos[None, :] <= pos[:, None]),
                     same_chunk & (pos[None, :] >= pos[:, None])]).astype(BF16)
    return pl.pallas_call(
        _rwkv_prep_kernel,
        grid=(b, nt),
        in_specs=[
            pl.BlockSpec((1, tl, c), lambda bi, i: (bi, i, 0)),
            pl.BlockSpec((1, SUBLANES, c), lambda bi, i: (bi, jnp.maximum(i * hb - 1, 0), 0)),
            pl.BlockSpec((1, SUBLANES, c), lambda bi, i: (bi, jnp.minimum((i + 1) * hb, nhb - 1), 0)),
            full((1, c)),
            full(w2cat.shape), full(w0cat.shape), full(a2cat.shape), full(a0cat.shape), full(g2.shape),
            full((1, RWKV_DIM)), full((1, RWKV_DIM)), full((1, RWKV_DIM)), full(ones_bd.shape),
            full(tri.shape),
        ],
        out_specs=[tok2, tok2, tok2, tok2, tokw, tok, tok, tok],
        out_shape=[s2h, s2h, s2h, s2h, sw, s1h, s1, s1],
        compiler_params=_cparams(("parallel", "arbitrary")),
        name="rwkv_prep",
    )(pshift, pshift, pshift, mu, w2cat, w0cat, a2cat, a0cat, g2, k_k, k_a, r_k, ones_bd, tri)


def _stack_heads(x, head_masks):
    return jnp.concatenate([jnp.where(m, x, 0.0) for m in head_masks], axis=0)


def _rwkv_scan_kernel(atf_ref, rtf_ref, btf_ref, ktf_ref, wtf_ref, vf_ref,
                      atb_ref, rtb_ref, btb_ref, ktb_ref, wtb_ref, vb_ref,
                      of_ref, ob_ref, s_ref):
    t = CHUNK

    @pl.when(pl.program_id(1) == 0)
    def _():
        s_ref[...] = jnp.zeros_like(s_ref)

    row = lax.broadcasted_iota(jnp.int32, (t, QUAD), 0)
    lane = lax.broadcasted_iota(jnp.int32, (t, QUAD), 1)
    col = lane & (HEAD_DIM - 1)
    same16 = (row >> 4) == (col >> 4)
    same32 = (row >> 5) == (col >> 5)
    in32_off16 = same32 & jnp.logical_not(same16)
    eye = jnp.where(col == row, 1.0, 0.0)
    head_masks = [(lane >> 6) == h for h in range(4)]
    strict = (col < row, col > row)
    incl = (col <= row, col >= row)
    qrow = lax.broadcasted_iota(jnp.int32, (QUAD, QUAD), 0) >> 6
    qcol = lax.broadcasted_iota(jnp.int32, (QUAD, QUAD), 1) >> 6
    same_head = qrow == qcol
    stack = lambda x: _stack_heads(x, head_masks)
    cat = lambda a, b: jnp.concatenate([a, b], axis=0)

    refs = ((atf_ref, rtf_ref, btf_ref, ktf_ref, wtf_ref, vf_ref, of_ref),
            (atb_ref, rtb_ref, btb_ref, ktb_ref, wtb_ref, vb_ref, ob_ref))
    n_sub = vf_ref.shape[1] // t
    chains = [(d, q, c) for c in range(n_sub) for q in range(RWKV_DIM // QUAD) for d in range(2)]
    each = lambda fn, *lists: [fn(*args) for args in zip(*lists)]
    sl = lambda q: slice(q * QUAD, (q + 1) * QUAD)

    def chunk(d, c):
        return c if d == 0 else n_sub - 1 - c

    def rows(d, c):
        return slice(chunk(d, c) * t, (chunk(d, c) + 1) * t)

    load = lambda j: [refs[d][j][0, 0, rows(d, c), sl(q)].astype(F32) for d, q, c in chains]
    at, rt, bt, kt = load(0), load(1), load(2), load(3)
    v = [refs[d][5][0, rows(d, c), sl(q)].astype(F32) for d, q, c in chains]
    w_total = [refs[d][4][0, 0, chunk(d, c) * SUBLANES:chunk(d, c) * SUBLANES + 1, sl(q)]
               for d, q, c in chains]
    lhs = each(cat, at, rt)
    bk = each(cat, bt, kt)

    gram = each(lambda l_, b_, k_: _dot_nt(l_, cat(stack(b_), stack(k_))), lhs, bt, kt)
    a_ab = [jnp.where(strict[d], g[:t, :QUAD], 0.0) for (d, _, _), g in zip(chains, gram)]
    a_ak = [jnp.where(strict[d], g[:t, QUAD:], 0.0) for (d, _, _), g in zip(chains, gram)]
    a_rb = [jnp.where(incl[d], g[t:, :QUAD], 0.0) for (d, _, _), g in zip(chains, gram)]
    a_rk = [jnp.where(incl[d], g[t:, QUAD:], 0.0) for (d, _, _), g in zip(chains, gram)]
    from_v = each(lambda p, q_, v_: _dot(cat(p, q_), stack(v_)), a_ak, a_rk, v)

    a_d = each(lambda a: jnp.where(same16, a, 0.0), a_ab)
    pw = each(lambda a: eye + a, a_d)
    sq = each(lambda a: _dot(a, stack(a)), a_d)
    for _ in range(2):
        both = each(lambda p, s: _dot(cat(p, s), stack(s)), pw, sq)
        pw = each(lambda p, bo: p + bo[:t], pw, both)
        sq = each(lambda bo: bo[t:], both)
    x0 = each(lambda p, s: p + _dot(p, stack(s)), pw, sq)
    m1 = each(lambda x, a: _dot(x, stack(jnp.where(in32_off16, a, 0.0))), x0, a_ab)
    x1 = each(lambda x, m: x + _dot(m, stack(x)), x0, m1)
    m2 = each(lambda x, a: _dot(x, stack(jnp.where(same32, 0.0, a))), x1, a_ab)
    x2 = each(lambda x, m: x + _dot(m, stack(x)), x1, m2)

    per_chunk = len(chains) // n_sub
    state = [s_ref[d, q] for d, q, _ in chains[:per_chunk]]
    for c in range(n_sub):
        pick = lambda xs: xs[c * per_chunk:(c + 1) * per_chunk]
        from_state = each(_dot_nt, pick(lhs), state)
        u = each(lambda x, fs, fv: _dot(x, stack(fs[:t] + fv[:t])), pick(x2), from_state, pick(from_v))
        y = each(lambda fs, fv, a, u_: fs[t:] + fv[t:] + _dot(a, stack(u_)),
                 from_state, pick(from_v), pick(a_rb), u)
        upd = each(lambda u_, v_, bk_: _dot_tn(cat(u_, v_), bk_), u, pick(v), pick(bk))
        for (d, q, _), y_ in zip(pick(chains), y):
            refs[d][6][0, rows(d, c), sl(q)] = y_
        state = each(lambda s_, up, w: (s_ + jnp.where(same_head, up, 0.0)) * w, state, upd, pick(w_total))
    for (d, q, _), s_ in zip(chains[:per_chunk], state):
        s_ref[d, q] = s_


def _cumsum(tri, lw):
    acc = None
    rem = lw
    for _ in range(3):
        part = rem.astype(BF16)
        term = jnp.dot(tri, part, preferred_element_type=F32)
        acc = term if acc is None else acc + term
        rem = rem - part.astype(F32)
    return acc


def _rwkv_scan(at, rt, bt, kt, wt, v):
    b, l, c = v.shape
    tl = SCAN_CHUNKS * CHUNK
    nc = l // tl
    wrows = SCAN_CHUNKS * SUBLANES
    fwd = pl.BlockSpec((1, tl, c), lambda bi, i: (bi, i, 0))
    bwd = pl.BlockSpec((1, tl, c), lambda bi, i: (bi, nc - 1 - i, 0))
    fwd_d = pl.BlockSpec((1, 1, tl, c), lambda bi, i: (0, bi, i, 0))
    bwd_d = pl.BlockSpec((1, 1, tl, c), lambda bi, i: (1, bi, nc - 1 - i, 0))
    fwd_w = pl.BlockSpec((1, 1, wrows, c), lambda bi, i: (0, bi, i, 0))
    bwd_w = pl.BlockSpec((1, 1, wrows, c), lambda bi, i: (1, bi, nc - 1 - i, 0))
    return pl.pallas_call(
        _rwkv_scan_kernel,
        grid=(b, nc),
        in_specs=[fwd_d, fwd_d, fwd_d, fwd_d, fwd_w, fwd, bwd_d, bwd_d, bwd_d, bwd_d, bwd_w, bwd],
        out_specs=[fwd, bwd],
        out_shape=[jax.ShapeDtypeStruct((b, l, c), F32)] * 2,
        scratch_shapes=[pltpu.VMEM((2, c // QUAD, QUAD, QUAD), F32)],
        compiler_params=_cparams(("parallel", "arbitrary")),
        name="rwkv_scan",
    )(at, rt, bt, kt, wt, v, at, rt, bt, kt, wt, v)


def _s5_scan_kernel(u_ref, bdense_ref, cdense_ref, lam_ref, o_ref, ubuf_ref, st_ref, s_ref):
    d = pl.program_id(0)
    n_seq, tl = u_ref.shape[0], u_ref.shape[1]
    ns = S5_STATES

    @pl.when(pl.program_id(1) == 0)
    def _():
        s_ref[...] = jnp.zeros_like(s_ref)
        ubuf_ref[...] = jnp.zeros_like(ubuf_ref)

    rows = tl * S5_SEQ_PAD
    hs = ns // S5_SPLIT
    hc = S5_DIM // S5_SPLIT
    for b in range(n_seq):
        ubuf_ref[:, b, :] = u_ref[b]
    u = ubuf_ref[...].reshape(rows, S5_DIM)
    for h in range(S5_SPLIT):
        bu = _dot(u[:, h * hc:(h + 1) * hc], bdense_ref[0, h])
        st_ref[:, :, h * hs:(h + 1) * hs] = bu[:, :hs].reshape(tl, S5_SEQ_PAD, hs)
        st_ref[:, :, ns + h * hs:ns + (h + 1) * hs] = bu[:, hs:].reshape(tl, S5_SEQ_PAD, hs)
    lam_re = lam_ref[0, 0]
    lam_im = lam_ref[0, 1]

    def step(j, carry):
        s_re, s_im = carry
        tt = jnp.where(d == 0, j, tl - 1 - j)
        n_re = lam_re * s_re - lam_im * s_im + st_ref[tt, :, :ns]
        n_im = lam_re * s_im + lam_im * s_re + st_ref[tt, :, ns:]
        st_ref[tt, :, :ns] = n_re
        st_ref[tt, :, ns:] = n_im
        return n_re, n_im

    s_re, s_im = lax.fori_loop(0, tl, step, (s_ref[0], s_ref[1]), unroll=2)
    s_ref[0] = s_re
    s_ref[1] = s_im
    for h in range(S5_SPLIT):
        st_re = st_ref[:, :, h * hs:(h + 1) * hs].reshape(rows, hs)
        st_im = st_ref[:, :, ns + h * hs:ns + (h + 1) * hs].reshape(rows, hs)
        y = _dot(st_re, cdense_ref[h, :hs]) + _dot(st_im, cdense_ref[h, hs:])
        o_ref[0, :, :, h * hc:(h + 1) * hc] = y.reshape(tl, S5_SEQ_PAD, hc)


def _s5_scan(proj, bdense, cdense, lam):
    b, l, _ = proj.shape
    assert b <= S5_SEQ_PAD and PROJ_U_OFF % S5_DIM == 0
    tl = S5_TILE
    nt = l // tl
    pos = lambda dd, i: i + dd * (nt - 1 - 2 * i)
    return pl.pallas_call(
        _s5_scan_kernel,
        grid=(2, nt),
        in_specs=[
            pl.BlockSpec((b, tl, S5_DIM), lambda dd, i: (0, pos(dd, i), PROJ_U_OFF // S5_DIM)),
            pl.BlockSpec((1,) + bdense.shape[1:], lambda dd, i: (dd, 0, 0, 0)),
            pl.BlockSpec(cdense.shape, lambda dd, i: (0, 0, 0)),
            pl.BlockSpec((1, 2, S5_SEQ_PAD, S5_STATES), lambda dd, i: (dd, 0, 0, 0)),
        ],
        out_specs=pl.BlockSpec((1, tl, S5_SEQ_PAD, S5_DIM), lambda dd, i: (dd, pos(dd, i), 0, 0)),
        out_shape=jax.ShapeDtypeStruct((2, l, S5_SEQ_PAD, S5_DIM), F32),
        scratch_shapes=[pltpu.VMEM((tl, S5_SEQ_PAD, S5_DIM), F32),
                        pltpu.VMEM((tl, S5_SEQ_PAD, 2 * S5_STATES), F32),
                        pltpu.VMEM((2, S5_SEQ_PAD, S5_STATES), F32)],
        compiler_params=_cparams(("parallel", "arbitrary")),
        name="s5_scan",
    )(proj, bdense, cdense, lam)


def _s5_post_kernel(yf_ref, yb_ref, u_ref, dskip_ref, wglu_ref, bglu_ref, o_ref):
    n_seq, tl = u_ref.shape[0], u_ref.shape[1]
    per_seq = lambda ref, b: ref[0, :, b, :]
    y = jnp.concatenate([per_seq(yf_ref, b) + per_seq(yb_ref, b) + dskip_ref[...] * u_ref[b]
                         for b in range(n_seq)], axis=0)
    y = 0.5 * y * (1.0 + jnp.tanh(math.sqrt(2.0 / math.pi) * (y + 0.044715 * (y * y * y))))
    gate = _sigmoid(_dot(y, wglu_ref[...]) + bglu_ref[...])
    o_ref[...] = (y * gate).astype(BF16).reshape(n_seq, tl, S5_DIM)


def _s5_post(y2, proj, d_skip, w_glu, b_glu):
    b, l, _ = proj.shape
    tl = S5_TILE
    row = pl.BlockSpec((1, S5_DIM), lambda i: (0, 0))
    return pl.pallas_call(
        _s5_post_kernel,
        grid=(l // tl,),
        in_specs=[
            pl.BlockSpec((1, tl, S5_SEQ_PAD, S5_DIM), lambda i: (0, i, 0, 0)),
            pl.BlockSpec((1, tl, S5_SEQ_PAD, S5_DIM), lambda i: (1, i, 0, 0)),
            pl.BlockSpec((b, tl, S5_DIM), lambda i: (0, i, PROJ_U_OFF // S5_DIM)),
            row,
            pl.BlockSpec((S5_DIM, S5_DIM), lambda i: (0, 0)),
            row,
        ],
        out_specs=pl.BlockSpec((b, tl, S5_DIM), lambda i: (0, i, 0)),
        out_shape=jax.ShapeDtypeStruct((b, l, S5_DIM), BF16),
        compiler_params=_cparams(("parallel",)),
        name="s5_post",
    )(y2, y2, proj, d_skip, w_glu, b_glu)


def _s5_params(a_re, a_im, log_step, b_re, b_im, c_re, c_im):
    dt = jnp.exp(log_step)[..., None]
    z_re, z_im = a_re * dt, a_im * dt
    mag = jnp.exp(z_re)
    lam_re, lam_im = mag * jnp.cos(z_im), mag * jnp.sin(z_im)
    den = a_re * a_re + a_im * a_im
    q_re = ((lam_re - 1.0) * a_re + lam_im * a_im) / den
    q_im = (lam_im * a_re - (lam_re - 1.0) * a_im) / den
    bb_re = q_re[..., None] * b_re - q_im[..., None] * b_im
    bb_im = q_re[..., None] * b_im + q_im[..., None] * b_re
    gb = S5_GROUPS // S5_SPLIT
    eye = jnp.eye(gb, dtype=F32)

    def dense_b(x):
        x = x.reshape(2, S5_SPLIT, gb, S5_STATE, S5_CH)
        return jnp.einsum('dhgpc,gk->dhgckp', x, eye).reshape(2, S5_SPLIT, gb * S5_CH, gb * S5_STATE)

    def dense_c(x):
        x = x.reshape(S5_SPLIT, gb, S5_CH, S5_STATE)
        return jnp.einsum('hgcp,gk->hgpkc', x, eye).reshape(S5_SPLIT, gb * S5_STATE, gb * S5_CH)

    bdense = jnp.concatenate([dense_b(bb_re), dense_b(bb_im)], axis=-1).astype(BF16)
    cdense = jnp.concatenate([dense_c(c_re), -dense_c(c_im)], axis=1).astype(BF16)
    lam = jnp.stack([lam_re.reshape(2, S5_STATES), lam_im.reshape(2, S5_STATES)], axis=1)
    lam = jnp.broadcast_to(lam[:, :, None, :], (2, 2, S5_SEQ_PAD, S5_STATES))
    return bdense, cdense, lam


MERGE_TILE = 256


def _merge_kernel(x_ref, yf_ref, yb_ref, bonus_ref, g_ref, ys_ref, gate_ref,
                  lnw_ref, lnb_ref, ones_ref, pr_ref, ps_ref, wo_ref, o_ref):
    ones_bd = ones_ref[...]
    y = yf_ref[...] + yb_ref[...]
    mean = _head_sum(y, ones_bd) * (1.0 / HEAD_DIM)
    yc = y - mean
    var = _head_sum(yc * yc, ones_bd) * (1.0 / HEAD_DIM)
    yn = yc * lax.rsqrt(var + GN_EPS) * lnw_ref[...] + lnb_ref[...]
    yr = (yn + bonus_ref[...]) * g_ref[...]
    y_rwkv = _dot(yr, pr_ref[...])
    y_s5 = jnp.dot(ys_ref[...], ps_ref[...], preferred_element_type=F32)
    gates = _sigmoid(gate_ref[...])
    merged = gates[:, :D_MODEL] * y_rwkv + gates[:, D_MODEL:] * y_s5
    o_ref[...] = x_ref[...] + _dot(merged, wo_ref[...])


def _merge(x, yf, yb, bonus, g, ys, proj, ln_w, ln_b, ones_bd, proj_rwkv, proj_s5, w_out):
    n, d = x.shape
    tm = MERGE_TILE
    assert PROJ_GATE_OFF == 2 * d and proj.shape[1] == PROJ_GATE_OFF + 2 * d
    tok = lambda c: pl.BlockSpec((tm, c), lambda i: (i, 0))
    const = lambda a: pl.BlockSpec(a.shape, lambda i: (0, 0), pipeline_mode=pl.Buffered(1))
    return pl.pallas_call(
        _merge_kernel,
        grid=(n // tm,),
        in_specs=[tok(d), tok(RWKV_DIM), tok(RWKV_DIM), tok(RWKV_DIM), tok(RWKV_DIM), tok(S5_DIM),
                  pl.BlockSpec((tm, 2 * d), lambda i: (i, 1)),
                  const(ln_w), const(ln_b), const(ones_bd), const(proj_rwkv), const(proj_s5), const(w_out)],
        out_specs=tok(d),
        out_shape=jax.ShapeDtypeStruct((n, d), F32),
        compiler_params=_cparams(("parallel",)),
        name="merge",
    )(x, yf, yb, bonus, g, ys, proj, ln_w, ln_b, ones_bd, proj_rwkv, proj_s5, w_out)


def _cast_pad_kernel(x_ref, o_ref, *, valid_tiles):
    cols = x_ref.shape[1]

    @pl.when(pl.program_id(0) < valid_tiles)
    def _():
        o_ref[:, :cols] = x_ref[...].astype(BF16)
        if o_ref.shape[1] > cols:
            o_ref[:, cols:] = jnp.zeros((o_ref.shape[0], o_ref.shape[1] - cols), BF16)

    @pl.when(pl.program_id(0) >= valid_tiles)
    def _():
        o_ref[...] = jnp.zeros_like(o_ref)


def _pad_ff(w, axis):
    rows, cols = w.shape
    pad = (-w.shape[axis]) % FF_TILE
    out_rows, out_cols = (rows + pad, cols) if axis == 0 else (rows, cols + pad)
    tr = LANES if axis == 0 else 2 * LANES
    assert rows % tr == 0 and out_rows % tr == 0 and cols % LANES == 0
    valid_tiles = rows // tr
    return pl.pallas_call(
        functools.partial(_cast_pad_kernel, valid_tiles=valid_tiles),
        grid=(out_rows // tr,),
        in_specs=[pl.BlockSpec((tr, cols), lambda i: (jnp.minimum(i, valid_tiles - 1), 0))],
        out_specs=pl.BlockSpec((tr, out_cols), lambda i: (i, 0)),
        out_shape=jax.ShapeDtypeStruct((out_rows, out_cols), BF16),
        compiler_params=_cparams(("parallel",)),
        name="cast_pad",
    )(w)


def _proj_weight_kernel(w_ref, o_ref):
    o_ref[:, :SHIFT_COLS] = w_ref[:, :SHIFT_COLS].astype(BF16)
    o_ref[:, SHIFT_COLS:PROJ_U_OFF] = jnp.zeros((o_ref.shape[0], PROJ_U_OFF - SHIFT_COLS), BF16)
    o_ref[:, PROJ_U_OFF:] = w_ref[:, SHIFT_COLS:].astype(BF16)


def _proj_weight(w_in):
    d, cols = w_in.shape
    assert cols + PROJ_U_OFF - SHIFT_COLS == PROJ_COLS
    tr = 2 * LANES
    return pl.pallas_call(
        _proj_weight_kernel,
        grid=(d // tr,),
        in_specs=[pl.BlockSpec((tr, cols), lambda i: (i, 0))],
        out_specs=pl.BlockSpec((tr, PROJ_COLS), lambda i: (i, 0)),
        out_shape=jax.ShapeDtypeStruct((d, PROJ_COLS), BF16),
        compiler_params=_cparams(("parallel",)),
        name="proj_weight",
    )(w_in)


def _lora_cat(w):
    z = jnp.zeros_like(w[0])
    return jnp.concatenate([jnp.concatenate([w[0], z], axis=1),
                            jnp.concatenate([z, w[1]], axis=1)], axis=0).astype(BF16)


def _forward(x_a, x_b, norm_ffn1, ffn1_w_gate, ffn1_w_up, ffn1_w_down, norm_mix, w_in, shift_mu,
             rwkv_w0, rwkv_w2, rwkv_a0, rwkv_a2, rwkv_g2, rwkv_k_k, rwkv_k_a, rwkv_r_k,
             rwkv_ln_w, rwkv_ln_b, s5_a_re, s5_a_im, s5_log_step, s5_b_re, s5_b_im,
             s5_c_re, s5_c_im, s5_d, s5_w_glu, s5_b_glu, proj_rwkv, proj_s5, w_out,
             norm_ffn2, ffn2_w_gate, ffn2_w_up, ffn2_w_down, norm_final):
    l, d = x_a.shape[1:]
    b = x_a.shape[0] + x_b.shape[0]
    n_a, n_b = x_a.shape[0] * l, x_b.shape[0] * l
    n = n_a + n_b
    row = lambda p: p.reshape(1, -1)

    (x1,) = _ffn([x_a.reshape(n_a, d), x_b.reshape(n_b, d)], row(norm_ffn1), _pad_ff(ffn1_w_gate, 1),
                 _pad_ff(ffn1_w_up, 1), _pad_ff(ffn1_w_down, 0), row(norm_final),
                 out_rows=[n], final_norm=False)

    assert w_in.shape[1] == SHIFT_COLS + S5_DIM + 2 * d
    w_proj = _proj_weight(w_in)
    proj = _in_proj(x1, row(norm_mix), w_proj)
    proj3 = proj.reshape(b, l, PROJ_COLS)

    head_id = jnp.arange(QUAD) // HEAD_DIM
    ones_bd = (head_id[:, None] == head_id[None, :]).astype(BF16)
    at, rt, bt, kt, wt, v, g, bonus = _rwkv_prep(
        proj3, row(shift_mu),
        _lora_cat(rwkv_w2), rwkv_w0.reshape(1, -1), _lora_cat(rwkv_a2), rwkv_a0.reshape(1, -1),
        rwkv_g2.astype(BF16), row(rwkv_k_k), row(rwkv_k_a), row(rwkv_r_k), ones_bd)
    yf, yb = _rwkv_scan(at, rt, bt, kt, wt, v)

    bdense, cdense, lam = _s5_params(s5_a_re, s5_a_im, s5_log_step, s5_b_re, s5_b_im, s5_c_re, s5_c_im)
    ys2 = _s5_scan(proj3, bdense, cdense, lam)
    ys = _s5_post(ys2, proj3, row(s5_d), s5_w_glu.astype(BF16), row(s5_b_glu)).reshape(n, S5_DIM)

    flat = lambda a: a.reshape(n, RWKV_DIM)
    x2 = _merge(x1, flat(yf), flat(yb), flat(bonus), flat(g), ys, proj, row(rwkv_ln_w), row(rwkv_ln_b),
                ones_bd, proj_rwkv.astype(BF16), proj_s5.astype(BF16), w_out.astype(BF16))
    y_a, y_b = _ffn([x2], row(norm_ffn2), _pad_ff(ffn2_w_gate, 1), _pad_ff(ffn2_w_up, 1),
                    _pad_ff(ffn2_w_down, 0), row(norm_final), out_rows=[n_a, n_b], final_norm=True)
    return y_a.reshape(x_a.shape), y_b.reshape(x_b.shape)


def kernel(x_prompt, x_sample, norm_ffn1, ffn1_w_gate, ffn1_w_up, ffn1_w_down, norm_mix, w_in, shift_mu, rwkv_w0, rwkv_w2, rwkv_a0, rwkv_a2, rwkv_g2, rwkv_k_k, rwkv_k_a, rwkv_r_k, rwkv_ln_w, rwkv_ln_b, s5_a_re, s5_a_im, s5_log_step, s5_b_re, s5_b_im, s5_c_re, s5_c_im, s5_d, s5_w_glu, s5_b_glu, proj_rwkv, proj_s5, w_out, norm_ffn2, ffn2_w_gate, ffn2_w_up, ffn2_w_down, norm_final):
    layer = (norm_ffn1, ffn1_w_gate, ffn1_w_up, ffn1_w_down, norm_mix, w_in, shift_mu,
             rwkv_w0, rwkv_w2, rwkv_a0, rwkv_a2, rwkv_g2, rwkv_k_k, rwkv_k_a, rwkv_r_k,
             rwkv_ln_w, rwkv_ln_b, s5_a_re, s5_a_im, s5_log_step, s5_b_re, s5_b_im,
             s5_c_re, s5_c_im, s5_d, s5_w_glu, s5_b_glu, proj_rwkv, proj_s5, w_out,
             norm_ffn2, ffn2_w_gate, ffn2_w_up, ffn2_w_down)
    assert all(p.shape[0] == 1 for p in layer), "single-layer block"
    assert x_prompt.shape[1:] == x_sample.shape[1:]
    return _forward(x_prompt, x_sample, *[p[0] for p in layer], norm_final)
```

```python
import functools
import math

import jax
import jax.numpy as jnp
from jax import lax
from jax.experimental import pallas as pl
from jax.experimental.pallas import tpu as pltpu

F32 = jnp.float32
BF16 = jnp.bfloat16

D_MODEL = 2048
D_FF = 5504
HEADS = 16
HEAD_DIM = 64
RWKV_DIM = HEADS * HEAD_DIM
LORA = 64
GATE_LORA = 128
S5_GROUPS = 32
S5_CH = 16
S5_DIM = S5_GROUPS * S5_CH
S5_STATE = 64
S5_STATES = S5_GROUPS * S5_STATE
SHIFT_COLS = 3 * RWKV_DIM + 4 * LORA + GATE_LORA
RMS_EPS = 1e-6
GN_EPS = 64e-5
KK_EPS = 1e-12

LANES = 128
SUBLANES = 8
VMEM_LIMIT = 56 * 1024 * 1024

FF_TILE = 512
TOK_TILE = 512
CHUNK = 64
QUAD = 4 * HEAD_DIM
SCAN_CHUNKS = 4
S5_SEQ_PAD = SUBLANES
S5_TILE = 64
S5_SPLIT = 2
PROJ_TOK_TILE = 1024
PROJ_COL_TILE = 1024
PROJ_U_OFF = 3584
PROJ_GATE_OFF = 4096
PROJ_COLS = 8192


def _cparams(sem):
    return pltpu.CompilerParams(dimension_semantics=sem, vmem_limit_bytes=VMEM_LIMIT)


def _dot(a, b):
    return jnp.dot(a.astype(BF16), b.astype(BF16), preferred_element_type=F32)


def _dot_nt(a, b):
    return lax.dot_general(a.astype(BF16), b.astype(BF16), (((1,), (1,)), ((), ())),
                           preferred_element_type=F32)


def _dot_tn(a, b):
    return lax.dot_general(a.astype(BF16), b.astype(BF16), (((0,), (0,)), ((), ())),
                           preferred_element_type=F32)


def _split_dot(x, w, pieces):
    acc = None
    rem = x
    for _ in range(pieces):
        part = rem.astype(BF16)
        term = jnp.dot(part, w, preferred_element_type=F32)
        acc = term if acc is None else acc + term
        rem = rem - part.astype(F32)
    return acc


def _sigmoid(x):
    return 1.0 / (1.0 + jnp.exp(-x))


def _rms(x, g):
    ms = jnp.mean(x * x, axis=-1, keepdims=True)
    return x * lax.rsqrt(ms + RMS_EPS) * g


def _ffn_kernel(*refs, n_x, n_out, first_tiles, final_norm):
    x_refs = refs[:n_x]
    g_ref, wg_ref, wu_ref, wd_ref, gf_ref = refs[n_x:n_x + 5]
    o_refs = refs[n_x + 5:n_x + 5 + n_out]
    xn_ref, acc_ref = refs[n_x + 5 + n_out:]
    i = pl.program_id(0)
    j = pl.program_id(1)
    in_first = i < first_tiles
    parts = [(in_first, x_refs[0], o_refs[0]), (jnp.logical_not(in_first), x_refs[-1], o_refs[-1])]
    if n_x == 1 and n_out == 1:
        parts = [(True, x_refs[0], o_refs[0])]

    for cond, x_ref, _ in parts:
        @pl.when((j == 0) & cond)
        def _(x_ref=x_ref):
            xn_ref[...] = _rms(x_ref[...], g_ref[...]).astype(BF16)
            acc_ref[...] = jnp.zeros_like(acc_ref)

    xn = xn_ref[...]
    hg = jnp.dot(xn, wg_ref[...], preferred_element_type=F32)
    hu = jnp.dot(xn, wu_ref[...], preferred_element_type=F32)
    act = (hg * _sigmoid(hg)) * hu
    acc_ref[...] += jnp.dot(act.astype(BF16), wd_ref[...], preferred_element_type=F32)

    for cond, x_ref, o_ref in parts:
        @pl.when((j == pl.num_programs(1) - 1) & cond)
        def _(x_ref=x_ref, o_ref=o_ref):
            y = x_ref[...] + 0.5 * acc_ref[...]
            o_ref[...] = _rms(y, gf_ref[...]) if final_norm else y


def _ffn(xs, g, wg, wu, wd, gf, *, out_rows, final_norm):
    d = xs[0].shape[1]
    f = wg.shape[1]
    tm, tf = TOK_TILE, FF_TILE
    n = sum(x.shape[0] for x in xs)
    assert n == sum(out_rows) and all(r % tm == 0 for r in out_rows)
    assert all(x.shape[0] % tm == 0 for x in xs)
    first_rows = xs[0].shape[0] if len(xs) == 2 else out_rows[0]
    if len(xs) == 2 and len(out_rows) == 2:
        assert xs[0].shape[0] == out_rows[0]
    first_tiles = first_rows // tm

    def split_specs(count):
        if count == 1:
            return [pl.BlockSpec((tm, d), lambda i, j: (i, 0))]
        return [pl.BlockSpec((tm, d), lambda i, j: (jnp.minimum(i, first_tiles - 1), 0)),
                pl.BlockSpec((tm, d), lambda i, j: (jnp.maximum(i - first_tiles, 0), 0))]

    outs = pl.pallas_call(
        functools.partial(_ffn_kernel, n_x=len(xs), n_out=len(out_rows), first_tiles=first_tiles,
                          final_norm=final_norm),
        grid=(n // tm, f // tf),
        in_specs=split_specs(len(xs)) + [
            pl.BlockSpec((1, d), lambda i, j: (0, 0)),
            pl.BlockSpec((d, tf), lambda i, j: (0, j)),
            pl.BlockSpec((d, tf), lambda i, j: (0, j)),
            pl.BlockSpec((tf, d), lambda i, j: (j, 0)),
            pl.BlockSpec((1, d), lambda i, j: (0, 0)),
        ],
        out_specs=split_specs(len(out_rows)),
        out_shape=[jax.ShapeDtypeStruct((r, d), F32) for r in out_rows],
        scratch_shapes=[pltpu.VMEM((tm, d), BF16), pltpu.VMEM((tm, d), F32)],
        compiler_params=_cparams(("arbitrary", "arbitrary")),
        name="ffn_final" if final_norm else "ffn",
    )(*xs, g, wg, wu, wd, gf)
    return outs


def _in_proj_kernel(x_ref, g_ref, w_ref, o_ref, xn_ref):
    @pl.when(pl.program_id(1) == 0)
    def _():
        xn_ref[...] = _rms(x_ref[...], g_ref[...]).astype(BF16)

    o_ref[...] = jnp.dot(xn_ref[...], w_ref[...], preferred_element_type=F32)


def _in_proj(x, g, w):
    n, d = x.shape
    c = w.shape[1]
    tm, tn = PROJ_TOK_TILE, PROJ_COL_TILE
    return pl.pallas_call(
        _in_proj_kernel,
        grid=(n // tm, c // tn),
        in_specs=[
            pl.BlockSpec((tm, d), lambda i, j: (i, 0)),
            pl.BlockSpec((1, d), lambda i, j: (0, 0)),
            pl.BlockSpec((d, tn), lambda i, j: (0, j)),
        ],
        out_specs=pl.BlockSpec((tm, tn), lambda i, j: (i, j)),
        out_shape=jax.ShapeDtypeStruct((n, c), F32),
        scratch_shapes=[pltpu.VMEM((tm, d), BF16)],
        compiler_params=_cparams(("parallel", "arbitrary")),
        name="proj_in",
    )(x, g, w)


PREP_TILE = 256


def _head_sum(x, ones_quad):
    return jnp.concatenate([_split_dot(x[:, c:c + QUAD], ones_quad, 2)
                            for c in range(0, x.shape[1], QUAD)], axis=1)


def _rwkv_prep_kernel(main_ref, prev_ref, next_ref, mu_ref, w2_ref, w0_ref, a2_ref, a0_ref, g2_ref,
                      kk_ref, ka_ref, rk_ref, ones_ref, tri_ref,
                      at_out, rt_out, bt_out, kt_out, wt_out, v_out, g_out, bonus_out):
    i = pl.program_id(1)
    tl = main_ref.shape[1]
    x = main_ref[0]
    row = lax.broadcasted_iota(jnp.int32, (tl, 1), 0)
    before = jnp.where(i == 0, 0.0, prev_ref[0, SUBLANES - 1:SUBLANES, :])
    after = jnp.where(i == pl.num_programs(1) - 1, 0.0, next_ref[0, 0:1, :])
    prev = jnp.where(row == 0, before, pltpu.roll(x, 1, 0))
    nxt = jnp.where(row == tl - 1, after, pltpu.roll(x, tl - 1, 0))
    p = x + (0.5 * (prev + nxt) - x) * mu_ref[...]

    c0 = 3 * RWKV_DIM
    r = p[:, 0:RWKV_DIM]
    k = p[:, RWKV_DIM:2 * RWKV_DIM]
    v = p[:, 2 * RWKV_DIM:c0]
    wlow = jnp.tanh(p[:, c0:c0 + 2 * LORA])
    alow = p[:, c0 + 2 * LORA:c0 + 4 * LORA]
    glow = p[:, c0 + 4 * LORA:c0 + 4 * LORA + GATE_LORA]
    ones_bd = ones_ref[...]

    wpre = _dot(wlow, w2_ref[...]) + w0_ref[...]
    lw = (-math.exp(-0.5) * math.log2(math.e)) * _sigmoid(wpre)
    a = _sigmoid(_dot(alow, a2_ref[...]) + a0_ref[...])

    kk = k * kk_ref[...]
    kkn = kk * lax.rsqrt(_head_sum(kk * kk, ones_bd) + KK_EPS)
    ka = ka_ref[...]
    ksum = None
    n_chunks = tl // CHUNK
    for d in range(2):
        a_d = a[:, d * RWKV_DIM:(d + 1) * RWKV_DIM]
        lw_d = lw[:, d * RWKV_DIM:(d + 1) * RWKV_DIM]
        kd = k * (1.0 + (a_d - 1.0) * ka)
        cum = _cumsum(tri_ref[d], lw_d)
        e_out = jnp.exp2(-cum)
        at_out[d, 0] = (-kkn * jnp.exp2(cum - lw_d)).astype(BF16)
        rt_out[d, 0] = (r * jnp.exp2(cum)).astype(BF16)
        bt_out[d, 0] = (kkn * a_d * e_out).astype(BF16)
        kt_out[d, 0] = (kd * e_out).astype(BF16)
        total = jnp.exp2(jnp.sum(lw_d.reshape(n_chunks, CHUNK, RWKV_DIM), axis=1))
        wt_out[d, 0] = jnp.broadcast_to(total[:, None, :], (n_chunks, SUBLANES, RWKV_DIM)).reshape(
            n_chunks * SUBLANES, RWKV_DIM)
        ksum = kd if ksum is None else ksum + kd
    v_out[0] = v.astype(BF16)
    g_out[0] = _dot(_sigmoid(glow), g2_ref[...])
    bonus_out[0] = _head_sum(r * ksum * rk_ref[...], ones_bd) * v


def _rwkv_prep(pshift, mu, w2cat, w0cat, a2cat, a0cat, g2, k_k, k_a, r_k, ones_bd):
    b, l, _ = pshift.shape
    c = SHIFT_COLS
    tl = PREP_TILE
    nt = l // tl
    hb = tl // SUBLANES
    nhb = l // SUBLANES
    full = lambda shape: pl.BlockSpec(shape, lambda bi, i: (0,) * len(shape))
    tok = pl.BlockSpec((1, tl, RWKV_DIM), lambda bi, i: (bi, i, 0))
    tok2 = pl.BlockSpec((2, 1, tl, RWKV_DIM), lambda bi, i: (0, bi, i, 0))
    wrows = tl // CHUNK * SUBLANES
    tokw = pl.BlockSpec((2, 1, wrows, RWKV_DIM), lambda bi, i: (0, bi, i, 0))
    s1 = jax.ShapeDtypeStruct((b, l, RWKV_DIM), F32)
    s1h = jax.ShapeDtypeStruct((b, l, RWKV_DIM), BF16)
    s2h = jax.ShapeDtypeStruct((2, b, l, RWKV_DIM), BF16)
    sw = jax.ShapeDtypeStruct((2, b, l // CHUNK * SUBLANES, RWKV_DIM), F32)
    pos = jnp.arange(tl)
    same_chunk = (pos[:, None] // CHUNK) == (pos[None, :] // CHUNK)
    tri = jnp.stack([same_chunk & (pos[None, :] <= pos[:, None]),
                     same_chunk & (pos[None, :] >= pos[:, None])]).astype(BF16)
    return pl.pallas_call(
        _rwkv_prep_kernel,
        grid=(b, nt),
        in_specs=[
            pl.BlockSpec((1, tl, c), lambda bi, i: (bi, i, 0)),
            pl.BlockSpec((1, SUBLANES, c), lambda bi, i: (bi, jnp.maximum(i * hb - 1, 0), 0)),
            pl.BlockSpec((1, SUBLANES, c), lambda bi, i: (bi, jnp.minimum((i + 1) * hb, nhb - 1), 0)),
            full((1, c)),
            full(w2cat.shape), full(w0cat.shape), full(a2cat.shape), full(a0cat.shape), full(g2.shape),
            full((1, RWKV_DIM)), full((1, RWKV_DIM)), full((1, RWKV_DIM)), full(ones_bd.shape),
            full(tri.shape),
        ],
        out_specs=[tok2, tok2, tok2, tok2, tokw, tok, tok, tok],
        out_shape=[s2h, s2h, s2h, s2h, sw, s1h, s1, s1],
        compiler_params=_cparams(("parallel", "arbitrary")),
        name="rwkv_prep",
    )(pshift, pshift, pshift, mu, w2cat, w0cat, a2cat, a0cat, g2, k_k, k_a, r_k, ones_bd, tri)


def _stack_heads(x, head_masks):
    return jnp.concatenate([jnp.where(m, x, 0.0) for m in head_masks], axis=0)


def _rwkv_scan_kernel(atf_ref, rtf_ref, btf_ref, ktf_ref, wtf_ref, vf_ref,
                      atb_ref, rtb_ref, btb_ref, ktb_ref, wtb_ref, vb_ref,
                      of_ref, ob_ref, s_ref):
    t = CHUNK

    @pl.when(pl.program_id(1) == 0)
    def _():
        s_ref[...] = jnp.zeros_like(s_ref)

    row = lax.broadcasted_iota(jnp.int32, (t, QUAD), 0)
    lane = lax.broadcasted_iota(jnp.int32, (t, QUAD), 1)
    col = lane & (HEAD_DIM - 1)
    same16 = (row >> 4) == (col >> 4)
    same32 = (row >> 5) == (col >> 5)
    in32_off16 = same32 & jnp.logical_not(same16)
    eye = jnp.where(col == row, 1.0, 0.0)
    head_masks = [(lane >> 6) == h for h in range(4)]
    strict = (col < row, col > row)
    incl = (col <= row, col >= row)
    qrow = lax.broadcasted_iota(jnp.int32, (QUAD, QUAD), 0) >> 6
    qcol = lax.broadcasted_iota(jnp.int32, (QUAD, QUAD), 1) >> 6
    same_head = qrow == qcol
    stack = lambda x: _stack_heads(x, head_masks)
    cat = lambda a, b: jnp.concatenate([a, b], axis=0)

    refs = ((atf_ref, rtf_ref, btf_ref, ktf_ref, wtf_ref, vf_ref, of_ref),
            (atb_ref, rtb_ref, btb_ref, ktb_ref, wtb_ref, vb_ref, ob_ref))
    n_sub = vf_ref.shape[1] // t
    chains = [(d, q, c) for c in range(n_sub) for q in range(RWKV_DIM // QUAD) for d in range(2)]
    each = lambda fn, *lists: [fn(*args) for args in zip(*lists)]
    sl = lambda q: slice(q * QUAD, (q + 1) * QUAD)

    def chunk(d, c):
        return c if d == 0 else n_sub - 1 - c

    def rows(d, c):
        return slice(chunk(d, c) * t, (chunk(d, c) + 1) * t)

    load = lambda j: [refs[d][j][0, 0, rows(d, c), sl(q)].astype(F32) for d, q, c in chains]
    at, rt, bt, kt = load(0), load(1), load(2), load(3)
    v = [refs[d][5][0, rows(d, c), sl(q)].astype(F32) for d, q, c in chains]
    w_total = [refs[d][4][0, 0, chunk(d, c) * SUBLANES:chunk(d, c) * SUBLANES + 1, sl(q)]
               for d, q, c in chains]
    lhs = each(cat, at, rt)
    bk = each(cat, bt, kt)

    gram = each(lambda l_, b_, k_: _dot_nt(l_, cat(stack(b_), stack(k_))), lhs, bt, kt)
    a_ab = [jnp.where(strict[d], g[:t, :QUAD], 0.0) for (d, _, _), g in zip(chains, gram)]
    a_ak = [jnp.where(strict[d], g[:t, QUAD:], 0.0) for (d, _, _), g in zip(chains, gram)]
    a_rb = [jnp.where(incl[d], g[t:, :QUAD], 0.0) for (d, _, _), g in zip(chains, gram)]
    a_rk = [jnp.where(incl[d], g[t:, QUAD:], 0.0) for (d, _, _), g in zip(chains, gram)]
    from_v = each(lambda p, q_, v_: _dot(cat(p, q_), stack(v_)), a_ak, a_rk, v)

    a_d = each(lambda a: jnp.where(same16, a, 0.0), a_ab)
    pw = each(lambda a: eye + a, a_d)
    sq = each(lambda a: _dot(a, stack(a)), a_d)
    for _ in range(2):
        both = each(lambda p, s: _dot(cat(p, s), stack(s)), pw, sq)
        pw = each(lambda p, bo: p + bo[:t], pw, both)
        sq = each(lambda bo: bo[t:], both)
    x0 = each(lambda p, s: p + _dot(p, stack(s)), pw, sq)
    m1 = each(lambda x, a: _dot(x, stack(jnp.where(in32_off16, a, 0.0))), x0, a_ab)
    x1 = each(lambda x, m: x + _dot(m, stack(x)), x0, m1)
    m2 = each(lambda x, a: _dot(x, stack(jnp.where(same32, 0.0, a))), x1, a_ab)
    x2 = each(lambda x, m: x + _dot(m, stack(x)), x1, m2)

    per_chunk = len(chains) // n_sub
    state = [s_ref[d, q] for d, q, _ in chains[:per_chunk]]
    for c in range(n_sub):
        pick = lambda xs: xs[c * per_chunk:(c + 1) * per_chunk]
        from_state = each(_dot_nt, pick(lhs), state)
        u = each(lambda x, fs, fv: _dot(x, stack(fs[:t] + fv[:t])), pick(x2), from_state, pick(from_v))
        y = each(lambda fs, fv, a, u_: fs[t:] + fv[t:] + _dot(a, stack(u_)),
                 from_state, pick(from_v), pick(a_rb), u)
        upd = each(lambda u_, v_, bk_: _dot_tn(cat(u_, v_), bk_), u, pick(v), pick(bk))
        for (d, q, _), y_ in zip(pick(chains), y):
            refs[d][6][0, rows(d, c), sl(q)] = y_
        state = each(lambda s_, up, w: (s_ + jnp.where(same_head, up, 0.0)) * w, state, upd, pick(w_total))
    for (d, q, _), s_ in zip(chains[:per_chunk], state):
        s_ref[d, q] = s_


def _cumsum(tri, lw):
    acc = None
    rem = lw
    for _ in range(3):
        part = rem.astype(BF16)
        term = jnp.dot(tri, part, preferred_element_type=F32)
        acc = term if acc is None else acc + term
        rem = rem - part.astype(F32)
    return acc


def _rwkv_scan(at, rt, bt, kt, wt, v):
    b, l, c = v.shape
    tl = SCAN_CHUNKS * CHUNK
    nc = l // tl
    wrows = SCAN_CHUNKS * SUBLANES
    fwd = pl.BlockSpec((1, tl, c), lambda bi, i: (bi, i, 0))
    bwd = pl.BlockSpec((1, tl, c), lambda bi, i: (bi, nc - 1 - i, 0))
    fwd_d = pl.BlockSpec((1, 1, tl, c), lambda bi, i: (0, bi, i, 0))
    bwd_d = pl.BlockSpec((1, 1, tl, c), lambda bi, i: (1, bi, nc - 1 - i, 0))
    fwd_w = pl.BlockSpec((1, 1, wrows, c), lambda bi, i: (0, bi, i, 0))
    bwd_w = pl.BlockSpec((1, 1, wrows, c), lambda bi, i: (1, bi, nc - 1 - i, 0))
    return pl.pallas_call(
        _rwkv_scan_kernel,
        grid=(b, nc),
        in_specs=[fwd_d, fwd_d, fwd_d, fwd_d, fwd_w, fwd, bwd_d, bwd_d, bwd_d, bwd_d, bwd_w, bwd],
        out_specs=[fwd, bwd],
        out_shape=[jax.ShapeDtypeStruct((b, l, c), F32)] * 2,
        scratch_shapes=[pltpu.VMEM((2, c // QUAD, QUAD, QUAD), F32)],
        compiler_params=_cparams(("parallel", "arbitrary")),
        name="rwkv_scan",
    )(at, rt, bt, kt, wt, v, at, rt, bt, kt, wt, v)


def _s5_scan_kernel(uf_ref, ub_ref, bdense_ref, cdense_ref, lam_ref, of_ref, ob_ref,
                    ubuf_ref, st_ref, s_ref):
    n_seq, tl = uf_ref.shape[0], uf_ref.shape[1]
    ns = S5_STATES
    u_refs = (uf_ref, ub_ref)
    o_refs = (of_ref, ob_ref)

    @pl.when(pl.program_id(0) == 0)
    def _():
        s_ref[...] = jnp.zeros_like(s_ref)
        ubuf_ref[...] = jnp.zeros_like(ubuf_ref)

    rows = tl * S5_SEQ_PAD
    hs = ns // S5_SPLIT
    hc = S5_DIM // S5_SPLIT

    def project_in(d):
        for b in range(n_seq):
            ubuf_ref[d, :, b, :] = u_refs[d][b]
        u = ubuf_ref[d].reshape(rows, S5_DIM)
        for h in range(S5_SPLIT):
            bu = _dot(u[:, h * hc:(h + 1) * hc], bdense_ref[d, h])
            st_ref[d, :, :, h * hs:(h + 1) * hs] = bu[:, :hs].reshape(tl, S5_SEQ_PAD, hs)
            st_ref[d, :, :, ns + h * hs:ns + (h + 1) * hs] = bu[:, hs:].reshape(tl, S5_SEQ_PAD, hs)

    def recur(d):
        lam_re = lam_ref[d, 0]
        lam_im = lam_ref[d, 1]
        s_re, s_im = s_ref[d, 0], s_ref[d, 1]
        for j in range(tl):
            tt = j if d == 0 else tl - 1 - j
            s_re, s_im = (lam_re * s_re - lam_im * s_im + st_ref[d, tt, :, :ns],
                          lam_re * s_im + lam_im * s_re + st_ref[d, tt, :, ns:])
            st_ref[d, tt, :, :ns] = s_re
            st_ref[d, tt, :, ns:] = s_im
        s_ref[d, 0] = s_re
        s_ref[d, 1] = s_im

    def project_out(d):
        for h in range(S5_SPLIT):
            st_re = st_ref[d, :, :, h * hs:(h + 1) * hs].reshape(rows, hs)
            st_im = st_ref[d, :, :, ns + h * hs:ns + (h + 1) * hs].reshape(rows, hs)
            y = _dot(st_re, cdense_ref[h, :hs]) + _dot(st_im, cdense_ref[h, hs:])
            o_refs[d][:, :, h * hc:(h + 1) * hc] = y.reshape(tl, S5_SEQ_PAD, hc)

    project_in(0)
    project_in(1)
    recur(0)
    project_out(0)
    recur(1)
    project_out(1)


def _s5_scan(proj, bdense, cdense, lam):
    b, l, _ = proj.shape
    assert b <= S5_SEQ_PAD and PROJ_U_OFF % S5_DIM == 0
    tl = S5_TILE
    nt = l // tl
    ucol = PROJ_U_OFF // S5_DIM
    const = lambda a: pl.BlockSpec(a.shape, lambda i: (0,) * a.ndim, pipeline_mode=pl.Buffered(1))
    out = jax.ShapeDtypeStruct((l, S5_SEQ_PAD, S5_DIM), F32)
    return pl.pallas_call(
        _s5_scan_kernel,
        grid=(nt,),
        in_specs=[
            pl.BlockSpec((b, tl, S5_DIM), lambda i: (0, i, ucol)),
            pl.BlockSpec((b, tl, S5_DIM), lambda i: (0, nt - 1 - i, ucol)),
            const(bdense), const(cdense), const(lam),
        ],
        out_specs=[pl.BlockSpec((tl, S5_SEQ_PAD, S5_DIM), lambda i: (i, 0, 0)),
                   pl.BlockSpec((tl, S5_SEQ_PAD, S5_DIM), lambda i: (nt - 1 - i, 0, 0))],
        out_shape=[out, out],
        scratch_shapes=[pltpu.VMEM((2, tl, S5_SEQ_PAD, S5_DIM), F32),
                        pltpu.VMEM((2, tl, S5_SEQ_PAD, 2 * S5_STATES), F32),
                        pltpu.VMEM((2, 2, S5_SEQ_PAD, S5_STATES), F32)],
        compiler_params=_cparams(("arbitrary",)),
        name="s5_scan",
    )(proj, proj, bdense, cdense, lam)


def _s5_post_kernel(yf_ref, yb_ref, u_ref, dskip_ref, wglu_ref, bglu_ref, o_ref):
    n_seq, tl = u_ref.shape[0], u_ref.shape[1]
    per_seq = lambda ref, b: ref[:, b, :]
    y = jnp.concatenate([per_seq(yf_ref, b) + per_seq(yb_ref, b) + dskip_ref[...] * u_ref[b]
                         for b in range(n_seq)], axis=0)
    y = 0.5 * y * (1.0 + jnp.tanh(math.sqrt(2.0 / math.pi) * (y + 0.044715 * (y * y * y))))
    gate = _sigmoid(_dot(y, wglu_ref[...]) + bglu_ref[...])
    o_ref[...] = (y * gate).astype(BF16).reshape(n_seq, tl, S5_DIM)


def _s5_post(yf, yb, proj, d_skip, w_glu, b_glu):
    b, l, _ = proj.shape
    tl = S5_TILE
    row = pl.BlockSpec((1, S5_DIM), lambda i: (0, 0))
    scan_out = pl.BlockSpec((tl, S5_SEQ_PAD, S5_DIM), lambda i: (i, 0, 0))
    return pl.pallas_call(
        _s5_post_kernel,
        grid=(l // tl,),
        in_specs=[
            scan_out, scan_out,
            pl.BlockSpec((b, tl, S5_DIM), lambda i: (0, i, PROJ_U_OFF // S5_DIM)),
            row,
            pl.BlockSpec((S5_DIM, S5_DIM), lambda i: (0, 0)),
            row,
        ],
        out_specs=pl.BlockSpec((b, tl, S5_DIM), lambda i: (0, i, 0)),
        out_shape=jax.ShapeDtypeStruct((b, l, S5_DIM), BF16),
        compiler_params=_cparams(("parallel",)),
        name="s5_post",
    )(yf, yb, proj, d_skip, w_glu, b_glu)


def _s5_params(a_re, a_im, log_step, b_re, b_im, c_re, c_im):
    dt = jnp.exp(log_step)[..., None]
    z_re, z_im = a_re * dt, a_im * dt
    mag = jnp.exp(z_re)
    lam_re, lam_im = mag * jnp.cos(z_im), mag * jnp.sin(z_im)
    den = a_re * a_re + a_im * a_im
    q_re = ((lam_re - 1.0) * a_re + lam_im * a_im) / den
    q_im = (lam_im * a_re - (lam_re - 1.0) * a_im) / den
    bb_re = q_re[..., None] * b_re - q_im[..., None] * b_im
    bb_im = q_re[..., None] * b_im + q_im[..., None] * b_re
    gb = S5_GROUPS // S5_SPLIT
    eye = jnp.eye(gb, dtype=F32)

    def dense_b(x):
        x = x.reshape(2, S5_SPLIT, gb, S5_STATE, S5_CH)
        return jnp.einsum('dhgpc,gk->dhgckp', x, eye).reshape(2, S5_SPLIT, gb * S5_CH, gb * S5_STATE)

    def dense_c(x):
        x = x.reshape(S5_SPLIT, gb, S5_CH, S5_STATE)
        return jnp.einsum('hgcp,gk->hgpkc', x, eye).reshape(S5_SPLIT, gb * S5_STATE, gb * S5_CH)

    bdense = jnp.concatenate([dense_b(bb_re), dense_b(bb_im)], axis=-1).astype(BF16)
    cdense = jnp.concatenate([dense_c(c_re), -dense_c(c_im)], axis=1).astype(BF16)
    lam = jnp.stack([lam_re.reshape(2, S5_STATES), lam_im.reshape(2, S5_STATES)], axis=1)
    lam = jnp.broadcast_to(lam[:, :, None, :], (2, 2, S5_SEQ_PAD, S5_STATES))
    return bdense, cdense, lam


MERGE_TILE = 256


def _merge_kernel(x_ref, yf_ref, yb_ref, bonus_ref, g_ref, ys_ref, gate_ref,
                  lnw_ref, lnb_ref, ones_ref, pr_ref, ps_ref, wo_ref, o_ref):
    ones_bd = ones_ref[...]
    y = yf_ref[...] + yb_ref[...]
    mean = _head_sum(y, ones_bd) * (1.0 / HEAD_DIM)
    yc = y - mean
    var = _head_sum(yc * yc, ones_bd) * (1.0 / HEAD_DIM)
    yn = yc * lax.rsqrt(var + GN_EPS) * lnw_ref[...] + lnb_ref[...]
    yr = (yn + bonus_ref[...]) * g_ref[...]
    y_rwkv = _dot(yr, pr_ref[...])
    y_s5 = jnp.dot(ys_ref[...], ps_ref[...], preferred_element_type=F32)
    gates = _sigmoid(gate_ref[...])
    merged = gates[:, :D_MODEL] * y_rwkv + gates[:, D_MODEL:] * y_s5
    o_ref[...] = x_ref[...] + _dot(merged, wo_ref[...])


def _merge(x, yf, yb, bonus, g, ys, proj, ln_w, ln_b, ones_bd, proj_rwkv, proj_s5, w_out):
    n, d = x.shape
    tm = MERGE_TILE
    assert PROJ_GATE_OFF == 2 * d and proj.shape[1] == PROJ_GATE_OFF + 2 * d
    tok = lambda c: pl.BlockSpec((tm, c), lambda i: (i, 0))
    const = lambda a: pl.BlockSpec(a.shape, lambda i: (0, 0), pipeline_mode=pl.Buffered(1))
    return pl.pallas_call(
        _merge_kernel,
        grid=(n // tm,),
        in_specs=[tok(d), tok(RWKV_DIM), tok(RWKV_DIM), tok(RWKV_DIM), tok(RWKV_DIM), tok(S5_DIM),
                  pl.BlockSpec((tm, 2 * d), lambda i: (i, 1)),
                  const(ln_w), const(ln_b), const(ones_bd), const(proj_rwkv), const(proj_s5), const(w_out)],
        out_specs=tok(d),
        out_shape=jax.ShapeDtypeStruct((n, d), F32),
        compiler_params=_cparams(("parallel",)),
        name="merge",
    )(x, yf, yb, bonus, g, ys, proj, ln_w, ln_b, ones_bd, proj_rwkv, proj_s5, w_out)


def _cast_pad_kernel(x_ref, o_ref, *, valid_tiles):
    cols = x_ref.shape[1]

    @pl.when(pl.program_id(0) < valid_tiles)
    def _():
        o_ref[:, :cols] = x_ref[...].astype(BF16)
        if o_ref.shape[1] > cols:
            o_ref[:, cols:] = jnp.zeros((o_ref.shape[0], o_ref.shape[1] - cols), BF16)

    @pl.when(pl.program_id(0) >= valid_tiles)
    def _():
        o_ref[...] = jnp.zeros_like(o_ref)


def _pad_ff(w, axis):
    rows, cols = w.shape
    pad = (-w.shape[axis]) % FF_TILE
    out_rows, out_cols = (rows + pad, cols) if axis == 0 else (rows, cols + pad)
    tr = LANES if axis == 0 else 2 * LANES
    assert rows % tr == 0 and out_rows % tr == 0 and cols % LANES == 0
    valid_tiles = rows // tr
    return pl.pallas_call(
        functools.partial(_cast_pad_kernel, valid_tiles=valid_tiles),
        grid=(out_rows // tr,),
        in_specs=[pl.BlockSpec((tr, cols), lambda i: (jnp.minimum(i, valid_tiles - 1), 0))],
        out_specs=pl.BlockSpec((tr, out_cols), lambda i: (i, 0)),
        out_shape=jax.ShapeDtypeStruct((out_rows, out_cols), BF16),
        compiler_params=_cparams(("parallel",)),
        name="cast_pad",
    )(w)


def _proj_weight_kernel(w_ref, o_ref):
    o_ref[:, :SHIFT_COLS] = w_ref[:, :SHIFT_COLS].astype(BF16)
    o_ref[:, SHIFT_COLS:PROJ_U_OFF] = jnp.zeros((o_ref.shape[0], PROJ_U_OFF - SHIFT_COLS), BF16)
    o_ref[:, PROJ_U_OFF:] = w_ref[:, SHIFT_COLS:].astype(BF16)


def _proj_weight(w_in):
    d, cols = w_in.shape
    assert cols + PROJ_U_OFF - SHIFT_COLS == PROJ_COLS
    tr = 2 * LANES
    return pl.pallas_call(
        _proj_weight_kernel,
        grid=(d // tr,),
        in_specs=[pl.BlockSpec((tr, cols), lambda i: (i, 0))],
        out_specs=pl.BlockSpec((tr, PROJ_COLS), lambda i: (i, 0)),
        out_shape=jax.ShapeDtypeStruct((d, PROJ_COLS), BF16),
        compiler_params=_cparams(("parallel",)),
        name="proj_weight",
    )(w_in)


def _lora_cat(w):
    z = jnp.zeros_like(w[0])
    return jnp.concatenate([jnp.concatenate([w[0], z], axis=1),
                            jnp.concatenate([z, w[1]], axis=1)], axis=0).astype(BF16)


def _forward(x_a, x_b, norm_ffn1, ffn1_w_gate, ffn1_w_up, ffn1_w_down, norm_mix, w_in, shift_mu,
             rwkv_w0, rwkv_w2, rwkv_a0, rwkv_a2, rwkv_g2, rwkv_k_k, rwkv_k_a, rwkv_r_k,
             rwkv_ln_w, rwkv_ln_b, s5_a_re, s5_a_im, s5_log_step, s5_b_re, s5_b_im,
             s5_c_re, s5_c_im, s5_d, s5_w_glu, s5_b_glu, proj_rwkv, proj_s5, w_out,
             norm_ffn2, ffn2_w_gate, ffn2_w_up, ffn2_w_down, norm_final):
    l, d = x_a.shape[1:]
    b = x_a.shape[0] + x_b.shape[0]
    n_a, n_b = x_a.shape[0] * l, x_b.shape[0] * l
    n = n_a + n_b
    row = lambda p: p.reshape(1, -1)

    (x1,) = _ffn([x_a.reshape(n_a, d), x_b.reshape(n_b, d)], row(norm_ffn1), _pad_ff(ffn1_w_gate, 1),
                 _pad_ff(ffn1_w_up, 1), _pad_ff(ffn1_w_down, 0), row(norm_final),
                 out_rows=[n], final_norm=False)

    assert w_in.shape[1] == SHIFT_COLS + S5_DIM + 2 * d
    w_proj = _proj_weight(w_in)
    proj = _in_proj(x1, row(norm_mix), w_proj)
    proj3 = proj.reshape(b, l, PROJ_COLS)

    head_id = jnp.arange(QUAD) // HEAD_DIM
    ones_bd = (head_id[:, None] == head_id[None, :]).astype(BF16)
    at, rt, bt, kt, wt, v, g, bonus = _rwkv_prep(
        proj3, row(shift_mu),
        _lora_cat(rwkv_w2), rwkv_w0.reshape(1, -1), _lora_cat(rwkv_a2), rwkv_a0.reshape(1, -1),
        rwkv_g2.astype(BF16), row(rwkv_k_k), row(rwkv_k_a), row(rwkv_r_k), ones_bd)
    yf, yb = _rwkv_scan(at, rt, bt, kt, wt, v)

    bdense, cdense, lam = _s5_params(s5_a_re, s5_a_im, s5_log_step, s5_b_re, s5_b_im, s5_c_re, s5_c_im)
    ysf, ysb = _s5_scan(proj3, bdense, cdense, lam)
    ys = _s5_post(ysf, ysb, proj3, row(s5_d), s5_w_glu.astype(BF16), row(s5_b_glu)).reshape(n, S5_DIM)

    flat = lambda a: a.reshape(n, RWKV_DIM)
    x2 = _merge(x1, flat(yf), flat(yb), flat(bonus), flat(g), ys, proj, row(rwkv_ln_w), row(rwkv_ln_b),
                ones_bd, proj_rwkv.astype(BF16), proj_s5.astype(BF16), w_out.astype(BF16))
    y_a, y_b = _ffn([x2], row(norm_ffn2), _pad_ff(ffn2_w_gate, 1), _pad_ff(ffn2_w_up, 1),
                    _pad_ff(ffn2_w_down, 0), row(norm_final), out_rows=[n_a, n_b], final_norm=True)
    return y_a.reshape(x_a.shape), y_b.reshape(x_b.shape)


def kernel(x_prompt, x_sample, norm_ffn1, ffn1_w_gate, ffn1_w_up, ffn1_w_down, norm_mix, w_in, shift_mu, rwkv_w0, rwkv_w2, rwkv_a0, rwkv_a2, rwkv_g2, rwkv_k_k, rwkv_k_a, rwkv_r_k, rwkv_ln_w, rwkv_ln_b, s5_a_re, s5_a_im, s5_log_step, s5_b_re, s5_b_im, s5_c_re, s5_c_im, s5_d, s5_w_glu, s5_b_glu, proj_rwkv, proj_s5, w_out, norm_ffn2, ffn2_w_gate, ffn2_w_up, ffn2_w_down, norm_final):
    layer = (norm_ffn1, ffn1_w_gate, ffn1_w_up, ffn1_w_down, norm_mix, w_in, shift_mu,
             rwkv_w0, rwkv_w2, rwkv_a0, rwkv_a2, rwkv_g2, rwkv_k_k, rwkv_k_a, rwkv_r_k,
             rwkv_ln_w, rwkv_ln_b, s5_a_re, s5_a_im, s5_log_step, s5_b_re, s5_b_im,
             s5_c_re, s5_c_im, s5_d, s5_w_glu, s5_b_glu, proj_rwkv, proj_s5, w_out,
             norm_ffn2, ffn2_w_gate, ffn2_w_up, ffn2_w_down)
    assert all(p.shape[0] == 1 for p in layer), "single-layer block"
    assert x_prompt.shape[1:] == x_sample.shape[1:]
    return _forward(x_prompt, x_sample, *[p[0] for p in layer], norm_final)
```

```python
import functools
import math

import jax
import jax.numpy as jnp
from jax import lax
from jax.experimental import pallas as pl
from jax.experimental.pallas import tpu as pltpu

F32 = jnp.float32
BF16 = jnp.bfloat16

D_MODEL = 2048
D_FF = 5504
HEADS = 16
HEAD_DIM = 64
RWKV_DIM = HEADS * HEAD_DIM
LORA = 64
GATE_LORA = 128
S5_GROUPS = 32
S5_CH = 16
S5_DIM = S5_GROUPS * S5_CH
S5_STATE = 64
S5_STATES = S5_GROUPS * S5_STATE
SHIFT_COLS = 3 * RWKV_DIM + 4 * LORA + GATE_LORA
RMS_EPS = 1e-6
GN_EPS = 64e-5
KK_EPS = 1e-12

LANES = 128
SUBLANES = 8
VMEM_LIMIT = 56 * 1024 * 1024

FF_TILE = 512
TOK_TILE = 512
CHUNK = 64
QUAD = 4 * HEAD_DIM
SCAN_HEADS = 2
GROUP = SCAN_HEADS * HEAD_DIM
SCAN_CHUNKS = 4
S5_SEQ_PAD = SUBLANES
S5_TILE = 64
S5_SPLIT = 2
PROJ_TOK_TILE = 1024
PROJ_COL_TILE = 1024
PROJ_U_OFF = 3584
PROJ_GATE_OFF = 4096
PROJ_COLS = 8192


def _cparams(sem):
    return pltpu.CompilerParams(dimension_semantics=sem, vmem_limit_bytes=VMEM_LIMIT)


def _dot(a, b):
    return jnp.dot(a.astype(BF16), b.astype(BF16), preferred_element_type=F32)


def _dot_nt(a, b):
    return lax.dot_general(a.astype(BF16), b.astype(BF16), (((1,), (1,)), ((), ())),
                           preferred_element_type=F32)


def _dot_tn(a, b):
    return lax.dot_general(a.astype(BF16), b.astype(BF16), (((0,), (0,)), ((), ())),
                           preferred_element_type=F32)


def _split_dot(x, w, pieces):
    acc = None
    rem = x
    for _ in range(pieces):
        part = rem.astype(BF16)
        term = jnp.dot(part, w, preferred_element_type=F32)
        acc = term if acc is None else acc + term
        rem = rem - part.astype(F32)
    return acc


def _sigmoid(x):
    return 1.0 / (1.0 + jnp.exp(-x))


def _rms(x, g):
    ms = jnp.mean(x * x, axis=-1, keepdims=True)
    return x * lax.rsqrt(ms + RMS_EPS) * g


def _ffn_kernel(*refs, n_x, n_out, first_tiles, final_norm):
    x_refs = refs[:n_x]
    g_ref, wg_ref, wu_ref, wd_ref, gf_ref = refs[n_x:n_x + 5]
    o_refs = refs[n_x + 5:n_x + 5 + n_out]
    xn_ref, acc_ref = refs[n_x + 5 + n_out:]
    i = pl.program_id(0)
    j = pl.program_id(1)
    in_first = i < first_tiles
    parts = [(in_first, x_refs[0], o_refs[0]), (jnp.logical_not(in_first), x_refs[-1], o_refs[-1])]
    if n_x == 1 and n_out == 1:
        parts = [(True, x_refs[0], o_refs[0])]

    for cond, x_ref, _ in parts:
        @pl.when((j == 0) & cond)
        def _(x_ref=x_ref):
            xn_ref[...] = _rms(x_ref[...], g_ref[...]).astype(BF16)
            acc_ref[...] = jnp.zeros_like(acc_ref)

    xn = xn_ref[...]
    hg = jnp.dot(xn, wg_ref[...], preferred_element_type=F32)
    hu = jnp.dot(xn, wu_ref[...], preferred_element_type=F32)
    act = (hg * _sigmoid(hg)) * hu
    acc_ref[...] += jnp.dot(act.astype(BF16), wd_ref[...], preferred_element_type=F32)

    for cond, x_ref, o_ref in parts:
        @pl.when((j == pl.num_programs(1) - 1) & cond)
        def _(x_ref=x_ref, o_ref=o_ref):
            y = x_ref[...] + 0.5 * acc_ref[...]
            o_ref[...] = _rms(y, gf_ref[...]) if final_norm else y


def _ffn(xs, g, wg, wu, wd, gf, *, out_rows, final_norm):
    d = xs[0].shape[1]
    f = wg.shape[1]
    tm, tf = TOK_TILE, FF_TILE
    n = sum(x.shape[0] for x in xs)
    assert n == sum(out_rows) and all(r % tm == 0 for r in out_rows)
    assert all(x.shape[0] % tm == 0 for x in xs)
    first_rows = xs[0].shape[0] if len(xs) == 2 else out_rows[0]
    if len(xs) == 2 and len(out_rows) == 2:
        assert xs[0].shape[0] == out_rows[0]
    first_tiles = first_rows // tm

    def split_specs(count):
        if count == 1:
            return [pl.BlockSpec((tm, d), lambda i, j: (i, 0))]
        return [pl.BlockSpec((tm, d), lambda i, j: (jnp.minimum(i, first_tiles - 1), 0)),
                pl.BlockSpec((tm, d), lambda i, j: (jnp.maximum(i - first_tiles, 0), 0))]

    outs = pl.pallas_call(
        functools.partial(_ffn_kernel, n_x=len(xs), n_out=len(out_rows), first_tiles=first_tiles,
                          final_norm=final_norm),
        grid=(n // tm, f // tf),
        in_specs=split_specs(len(xs)) + [
            pl.BlockSpec((1, d), lambda i, j: (0, 0)),
            pl.BlockSpec((d, tf), lambda i, j: (0, j)),
            pl.BlockSpec((d, tf), lambda i, j: (0, j)),
            pl.BlockSpec((tf, d), lambda i, j: (j, 0)),
            pl.BlockSpec((1, d), lambda i, j: (0, 0)),
        ],
        out_specs=split_specs(len(out_rows)),
        out_shape=[jax.ShapeDtypeStruct((r, d), F32) for r in out_rows],
        scratch_shapes=[pltpu.VMEM((tm, d), BF16), pltpu.VMEM((tm, d), F32)],
        compiler_params=_cparams(("arbitrary", "arbitrary")),
        name="ffn_final" if final_norm else "ffn",
    )(*xs, g, wg, wu, wd, gf)
    return outs


def _in_proj_kernel(x_ref, g_ref, w_ref, o_ref, xn_ref):
    @pl.when(pl.program_id(1) == 0)
    def _():
        xn_ref[...] = _rms(x_ref[...], g_ref[...]).astype(BF16)

    o_ref[...] = jnp.dot(xn_ref[...], w_ref[...], preferred_element_type=F32)


def _in_proj(x, g, w):
    n, d = x.shape
    c = w.shape[1]
    tm, tn = PROJ_TOK_TILE, PROJ_COL_TILE
    return pl.pallas_call(
        _in_proj_kernel,
        grid=(n // tm, c // tn),
        in_specs=[
            pl.BlockSpec((tm, d), lambda i, j: (i, 0)),
            pl.BlockSpec((1, d), lambda i, j: (0, 0)),
            pl.BlockSpec((d, tn), lambda i, j: (0, j)),
        ],
        out_specs=pl.BlockSpec((tm, tn), lambda i, j: (i, j)),
        out_shape=jax.ShapeDtypeStruct((n, c), F32),
        scratch_shapes=[pltpu.VMEM((tm, d), BF16)],
        compiler_params=_cparams(("parallel", "arbitrary")),
        name="proj_in",
    )(x, g, w)


PREP_TILE = 256


def _head_sum(x, ones_quad):
    return jnp.concatenate([_split_dot(x[:, c:c + QUAD], ones_quad, 2)
                            for c in range(0, x.shape[1], QUAD)], axis=1)


def _rwkv_prep_kernel(main_ref, prev_ref, next_ref, mu_ref, w2_ref, w0_ref, a2_ref, a0_ref, g2_ref,
                      kk_ref, ka_ref, rk_ref, ones_ref, tri_ref,
                      at_out, rt_out, bt_out, kt_out, wt_out, v_out, g_out, bonus_out):
    i = pl.program_id(1)
    tl = main_ref.shape[1]
    x = main_ref[0]
    row = lax.broadcasted_iota(jnp.int32, (tl, 1), 0)
    before = jnp.where(i == 0, 0.0, prev_ref[0, SUBLANES - 1:SUBLANES, :])
    after = jnp.where(i == pl.num_programs(1) - 1, 0.0, next_ref[0, 0:1, :])
    prev = jnp.where(row == 0, before, pltpu.roll(x, 1, 0))
    nxt = jnp.where(row == tl - 1, after, pltpu.roll(x, tl - 1, 0))
    p = x + (0.5 * (prev + nxt) - x) * mu_ref[...]

    c0 = 3 * RWKV_DIM
    r = p[:, 0:RWKV_DIM]
    k = p[:, RWKV_DIM:2 * RWKV_DIM]
    v = p[:, 2 * RWKV_DIM:c0]
    wlow = jnp.tanh(p[:, c0:c0 + 2 * LORA])
    alow = p[:, c0 + 2 * LORA:c0 + 4 * LORA]
    glow = p[:, c0 + 4 * LORA:c0 + 4 * LORA + GATE_LORA]
    ones_bd = ones_ref[...]

    wpre = _dot(wlow, w2_ref[...]) + w0_ref[...]
    lw = (-math.exp(-0.5) * math.log2(math.e)) * _sigmoid(wpre)
    a = _sigmoid(_dot(alow, a2_ref[...]) + a0_ref[...])

    kk = k * kk_ref[...]
    kkn = kk * lax.rsqrt(_head_sum(kk * kk, ones_bd) + KK_EPS)
    ka = ka_ref[...]
    ksum = None
    n_chunks = tl // CHUNK
    for d in range(2):
        a_d = a[:, d * RWKV_DIM:(d + 1) * RWKV_DIM]
        lw_d = lw[:, d * RWKV_DIM:(d + 1) * RWKV_DIM]
        kd = k * (1.0 + (a_d - 1.0) * ka)
        cum = _cumsum(tri_ref[d], lw_d)
        e_out = jnp.exp2(-cum)
        at_out[d, 0] = (-kkn * jnp.exp2(cum - lw_d)).astype(BF16)
        rt_out[d, 0] = (r * jnp.exp2(cum)).astype(BF16)
        bt_out[d, 0] = (kkn * a_d * e_out).astype(BF16)
        kt_out[d, 0] = (kd * e_out).astype(BF16)
        total = jnp.exp2(jnp.sum(lw_d.reshape(n_chunks, CHUNK, RWKV_DIM), axis=1))
        wt_out[d, 0] = jnp.broadcast_to(total[:, None, :], (n_chunks, SUBLANES, RWKV_DIM)).reshape(
            n_chunks * SUBLANES, RWKV_DIM)
        ksum = kd if ksum is None else ksum + kd
    v_out[0] = v.astype(BF16)
    g_out[0] = _dot(_sigmoid(glow), g2_ref[...])
    bonus_out[0] = _head_sum(r * ksum * rk_ref[...], ones_bd) * v


def _rwkv_prep(pshift, mu, w2cat, w0cat, a2cat, a0cat, g2, k_k, k_a, r_k, ones_bd):
    b, l, _ = pshift.shape
    c = SHIFT_COLS
    tl = PREP_TILE
    nt = l // tl
    hb = tl // SUBLANES
    nhb = l // SUBLANES
    full = lambda shape: pl.BlockSpec(shape, lambda bi, i: (0,) * len(shape))
    tok = pl.BlockSpec((1, tl, RWKV_DIM), lambda bi, i: (bi, i, 0))
    tok2 = pl.BlockSpec((2, 1, tl, RWKV_DIM), lambda bi, i: (0, bi, i, 0))
    wrows = tl // CHUNK * SUBLANES
    tokw = pl.BlockSpec((2, 1, wrows, RWKV_DIM), lambda bi, i: (0, bi, i, 0))
    s1 = jax.ShapeDtypeStruct((b, l, RWKV_DIM), F32)
    s1h = jax.ShapeDtypeStruct((b, l, RWKV_DIM), BF16)
    s2h = jax.ShapeDtypeStruct((2, b, l, RWKV_DIM), BF16)
    sw = jax.ShapeDtypeStruct((2, b, l // CHUNK * SUBLANES, RWKV_DIM), F32)
    pos = jnp.arange(tl)
    same_chunk = (pos[:, None] // CHUNK) == (pos[None, :] // CHUNK)
    tri = jnp.stack([same_chunk & (pos[None, :] <= pos[:, None]),
                     same_chunk & (pos[None, :] >= pos[:, None])]).astype(BF16)
    return pl.pallas_call(
        _rwkv_prep_kernel,
        grid=(b, nt),
        in_specs=[
            pl.BlockSpec((1, tl, c), lambda bi, i: (bi, i, 0)),
            pl.BlockSpec((1, SUBLANES, c), lambda bi, i: (bi, jnp.maximum(i * hb - 1, 0), 0)),
            pl.BlockSpec((1, SUBLANES, c), lambda bi, i: (bi, jnp.minimum((i + 1) * hb, nhb - 1), 0)),
            full((1, c)),
            full(w2cat.shape), full(w0cat.shape), full(a2cat.shape), full(a0cat.shape), full(g2.shape),
            full((1, RWKV_DIM)), full((1, RWKV_DIM)), full((1, RWKV_DIM)), full(ones_bd.shape),
            full(tri.shape),
        ],
        out_specs=[tok2, tok2, tok2, tok2, tokw, tok, tok, tok],
        out_shape=[s2h, s2h, s2h, s2h, sw, s1h, s1, s1],
        compiler_params=_cparams(("parallel", "arbitrary")),
        name="rwkv_prep",
    )(pshift, pshift, pshift, mu, w2cat, w0cat, a2cat, a0cat, g2, k_k, k_a, r_k, ones_bd, tri)


def _stack_heads(x, head_masks):
    return jnp.concatenate([jnp.where(m, x, 0.0) for m in head_masks], axis=0)


def _rwkv_scan_kernel(atf_ref, rtf_ref, btf_ref, ktf_ref, wtf_ref, vf_ref,
                      atb_ref, rtb_ref, btb_ref, ktb_ref, wtb_ref, vb_ref,
                      of_ref, ob_ref, s_ref):
    t = CHUNK

    @pl.when(pl.program_id(1) == 0)
    def _():
        s_ref[...] = jnp.zeros_like(s_ref)

    row = lax.broadcasted_iota(jnp.int32, (t, GROUP), 0)
    lane = lax.broadcasted_iota(jnp.int32, (t, GROUP), 1)
    col = lane & (HEAD_DIM - 1)
    same16 = (row >> 4) == (col >> 4)
    same32 = (row >> 5) == (col >> 5)
    in32_off16 = same32 & jnp.logical_not(same16)
    eye = jnp.where(col == row, 1.0, 0.0)
    head_masks = [(lane >> 6) == h for h in range(SCAN_HEADS)]
    strict = (col < row, col > row)
    incl = (col <= row, col >= row)
    qrow = lax.broadcasted_iota(jnp.int32, (GROUP, GROUP), 0) >> 6
    qcol = lax.broadcasted_iota(jnp.int32, (GROUP, GROUP), 1) >> 6
    same_head = qrow == qcol
    stack = lambda x: _stack_heads(x, head_masks)
    cat = lambda a, b: jnp.concatenate([a, b], axis=0)

    refs = ((atf_ref, rtf_ref, btf_ref, ktf_ref, wtf_ref, vf_ref, of_ref),
            (atb_ref, rtb_ref, btb_ref, ktb_ref, wtb_ref, vb_ref, ob_ref))
    n_sub = vf_ref.shape[1] // t
    chains = [(d, q, c) for c in range(n_sub) for q in range(RWKV_DIM // GROUP) for d in range(2)]
    each = lambda fn, *lists: [fn(*args) for args in zip(*lists)]
    sl = lambda q: slice(q * GROUP, (q + 1) * GROUP)

    def chunk(d, c):
        return c if d == 0 else n_sub - 1 - c

    def rows(d, c):
        return slice(chunk(d, c) * t, (chunk(d, c) + 1) * t)

    load = lambda j: [refs[d][j][0, 0, rows(d, c), sl(q)].astype(F32) for d, q, c in chains]
    at, rt, bt, kt = load(0), load(1), load(2), load(3)
    v = [refs[d][5][0, rows(d, c), sl(q)].astype(F32) for d, q, c in chains]
    w_total = [refs[d][4][0, 0, chunk(d, c) * SUBLANES:chunk(d, c) * SUBLANES + 1, sl(q)]
               for d, q, c in chains]
    lhs = each(cat, at, rt)
    bk = each(cat, bt, kt)

    gram = each(lambda l_, b_, k_: _dot_nt(l_, cat(stack(b_), stack(k_))), lhs, bt, kt)
    a_ab = [jnp.where(strict[d], g[:t, :GROUP], 0.0) for (d, _, _), g in zip(chains, gram)]
    a_ak = [jnp.where(strict[d], g[:t, GROUP:], 0.0) for (d, _, _), g in zip(chains, gram)]
    a_rb = [jnp.where(incl[d], g[t:, :GROUP], 0.0) for (d, _, _), g in zip(chains, gram)]
    a_rk = [jnp.where(incl[d], g[t:, GROUP:], 0.0) for (d, _, _), g in zip(chains, gram)]
    from_v = each(lambda p, q_, v_: _dot(cat(p, q_), stack(v_)), a_ak, a_rk, v)

    a_d = each(lambda a: jnp.where(same16, a, 0.0), a_ab)
    pw = each(lambda a: eye + a, a_d)
    sq = each(lambda a: _dot(a, stack(a)), a_d)
    for _ in range(2):
        both = each(lambda p, s: _dot(cat(p, s), stack(s)), pw, sq)
        pw = each(lambda p, bo: p + bo[:t], pw, both)
        sq = each(lambda bo: bo[t:], both)
    x0 = each(lambda p, s: p + _dot(p, stack(s)), pw, sq)
    m1 = each(lambda x, a: _dot(x, stack(jnp.where(in32_off16, a, 0.0))), x0, a_ab)
    x1 = each(lambda x, m: x + _dot(m, stack(x)), x0, m1)
    m2 = each(lambda x, a: _dot(x, stack(jnp.where(same32, 0.0, a))), x1, a_ab)
    x2 = each(lambda x, m: x + _dot(m, stack(x)), x1, m2)

    per_chunk = len(chains) // n_sub
    state = [s_ref[d, q] for d, q, _ in chains[:per_chunk]]
    for c in range(n_sub):
        pick = lambda xs: xs[c * per_chunk:(c + 1) * per_chunk]
        from_state = each(_dot_nt, pick(lhs), state)
        u = each(lambda x, fs, fv: _dot(x, stack(fs[:t] + fv[:t])), pick(x2), from_state, pick(from_v))
        y = each(lambda fs, fv, a, u_: fs[t:] + fv[t:] + _dot(a, stack(u_)),
                 from_state, pick(from_v), pick(a_rb), u)
        upd = each(lambda u_, v_, bk_: _dot_tn(cat(u_, v_), bk_), u, pick(v), pick(bk))
        for (d, q, _), y_ in zip(pick(chains), y):
            refs[d][6][0, rows(d, c), sl(q)] = y_
        state = each(lambda s_, up, w: (s_ + jnp.where(same_head, up, 0.0)) * w, state, upd, pick(w_total))
    for (d, q, _), s_ in zip(chains[:per_chunk], state):
        s_ref[d, q] = s_


def _cumsum(tri, lw):
    acc = None
    rem = lw
    for _ in range(3):
        part = rem.astype(BF16)
        term = jnp.dot(tri, part, preferred_element_type=F32)
        acc = term if acc is None else acc + term
        rem = rem - part.astype(F32)
    return acc


def _rwkv_scan(at, rt, bt, kt, wt, v):
    b, l, c = v.shape
    tl = SCAN_CHUNKS * CHUNK
    nc = l // tl
    wrows = SCAN_CHUNKS * SUBLANES
    fwd = pl.BlockSpec((1, tl, c), lambda bi, i: (bi, i, 0))
    bwd = pl.BlockSpec((1, tl, c), lambda bi, i: (bi, nc - 1 - i, 0))
    fwd_d = pl.BlockSpec((1, 1, tl, c), lambda bi, i: (0, bi, i, 0))
    bwd_d = pl.BlockSpec((1, 1, tl, c), lambda bi, i: (1, bi, nc - 1 - i, 0))
    fwd_w = pl.BlockSpec((1, 1, wrows, c), lambda bi, i: (0, bi, i, 0))
    bwd_w = pl.BlockSpec((1, 1, wrows, c), lambda bi, i: (1, bi, nc - 1 - i, 0))
    return pl.pallas_call(
        _rwkv_scan_kernel,
        grid=(b, nc),
        in_specs=[fwd_d, fwd_d, fwd_d, fwd_d, fwd_w, fwd, bwd_d, bwd_d, bwd_d, bwd_d, bwd_w, bwd],
        out_specs=[fwd, bwd],
        out_shape=[jax.ShapeDtypeStruct((b, l, c), F32)] * 2,
        scratch_shapes=[pltpu.VMEM((2, c // GROUP, GROUP, GROUP), F32)],
        compiler_params=_cparams(("parallel", "arbitrary")),
        name="rwkv_scan",
    )(at, rt, bt, kt, wt, v, at, rt, bt, kt, wt, v)


def _s5_scan_kernel(uf_ref, ub_ref, bdense_ref, cdense_ref, lam_ref, of_ref, ob_ref,
                    ubuf_ref, st_ref, s_ref):
    n_seq, tl = uf_ref.shape[0], uf_ref.shape[1]
    ns = S5_STATES
    u_refs = (uf_ref, ub_ref)
    o_refs = (of_ref, ob_ref)

    @pl.when(pl.program_id(0) == 0)
    def _():
        s_ref[...] = jnp.zeros_like(s_ref)
        ubuf_ref[...] = jnp.zeros_like(ubuf_ref)

    rows = tl * S5_SEQ_PAD
    hs = ns // S5_SPLIT
    hc = S5_DIM // S5_SPLIT

    def project_in(d):
        for b in range(n_seq):
            ubuf_ref[d, :, b, :] = u_refs[d][b]
        u = ubuf_ref[d].reshape(rows, S5_DIM)
        for h in range(S5_SPLIT):
            bu = _dot(u[:, h * hc:(h + 1) * hc], bdense_ref[d, h])
            st_ref[d, :, :, h * hs:(h + 1) * hs] = bu[:, :hs].reshape(tl, S5_SEQ_PAD, hs)
            st_ref[d, :, :, ns + h * hs:ns + (h + 1) * hs] = bu[:, hs:].reshape(tl, S5_SEQ_PAD, hs)

    def recur(d):
        lam_re = lam_ref[d, 0]
        lam_im = lam_ref[d, 1]
        s_re, s_im = s_ref[d, 0], s_ref[d, 1]
        for j in range(tl):
            tt = j if d == 0 else tl - 1 - j
            s_re, s_im = (lam_re * s_re - lam_im * s_im + st_ref[d, tt, :, :ns],
                          lam_re * s_im + lam_im * s_re + st_ref[d, tt, :, ns:])
            st_ref[d, tt, :, :ns] = s_re
            st_ref[d, tt, :, ns:] = s_im
        s_ref[d, 0] = s_re
        s_ref[d, 1] = s_im

    def project_out(d):
        for h in range(S5_SPLIT):
            st_re = st_ref[d, :, :, h * hs:(h + 1) * hs].reshape(rows, hs)
            st_im = st_ref[d, :, :, ns + h * hs:ns + (h + 1) * hs].reshape(rows, hs)
            y = _dot(st_re, cdense_ref[h, :hs]) + _dot(st_im, cdense_ref[h, hs:])
            o_refs[d][:, :, h * hc:(h + 1) * hc] = y.reshape(tl, S5_SEQ_PAD, hc)

    project_in(0)
    project_in(1)
    recur(0)
    project_out(0)
    recur(1)
    project_out(1)


def _s5_scan(proj, bdense, cdense, lam):
    b, l, _ = proj.shape
    assert b <= S5_SEQ_PAD and PROJ_U_OFF % S5_DIM == 0
    tl = S5_TILE
    nt = l // tl
    ucol = PROJ_U_OFF // S5_DIM
    const = lambda a: pl.BlockSpec(a.shape, lambda i: (0,) * a.ndim, pipeline_mode=pl.Buffered(1))
    out = jax.ShapeDtypeStruct((l, S5_SEQ_PAD, S5_DIM), F32)
    return pl.pallas_call(
        _s5_scan_kernel,
        grid=(nt,),
        in_specs=[
            pl.BlockSpec((b, tl, S5_DIM), lambda i: (0, i, ucol)),
            pl.BlockSpec((b, tl, S5_DIM), lambda i: (0, nt - 1 - i, ucol)),
            const(bdense), const(cdense), const(lam),
        ],
        out_specs=[pl.BlockSpec((tl, S5_SEQ_PAD, S5_DIM), lambda i: (i, 0, 0)),
                   pl.BlockSpec((tl, S5_SEQ_PAD, S5_DIM), lambda i: (nt - 1 - i, 0, 0))],
        out_shape=[out, out],
        scratch_shapes=[pltpu.VMEM((2, tl, S5_SEQ_PAD, S5_DIM), F32),
                        pltpu.VMEM((2, tl, S5_SEQ_PAD, 2 * S5_STATES), F32),
                        pltpu.VMEM((2, 2, S5_SEQ_PAD, S5_STATES), F32)],
        compiler_params=_cparams(("arbitrary",)),
        name="s5_scan",
    )(proj, proj, bdense, cdense, lam)


def _s5_post_kernel(yf_ref, yb_ref, u_ref, dskip_ref, wglu_ref, bglu_ref, o_ref):
    n_seq, tl = u_ref.shape[0], u_ref.shape[1]
    per_seq = lambda ref, b: ref[:, b, :]
    y = jnp.concatenate([per_seq(yf_ref, b) + per_seq(yb_ref, b) + dskip_ref[...] * u_ref[b]
                         for b in range(n_seq)], axis=0)
    y = 0.5 * y * (1.0 + jnp.tanh(math.sqrt(2.0 / math.pi) * (y + 0.044715 * (y * y * y))))
    gate = _sigmoid(_dot(y, wglu_ref[...]) + bglu_ref[...])
    o_ref[...] = (y * gate).astype(BF16).reshape(n_seq, tl, S5_DIM)


def _s5_post(yf, yb, proj, d_skip, w_glu, b_glu):
    b, l, _ = proj.shape
    tl = S5_TILE
    row = pl.BlockSpec((1, S5_DIM), lambda i: (0, 0))
    scan_out = pl.BlockSpec((tl, S5_SEQ_PAD, S5_DIM), lambda i: (i, 0, 0))
    return pl.pallas_call(
        _s5_post_kernel,
        grid=(l // tl,),
        in_specs=[
            scan_out, scan_out,
            pl.BlockSpec((b, tl, S5_DIM), lambda i: (0, i, PROJ_U_OFF // S5_DIM)),
            row,
            pl.BlockSpec((S5_DIM, S5_DIM), lambda i: (0, 0)),
            row,
        ],
        out_specs=pl.BlockSpec((b, tl, S5_DIM), lambda i: (0, i, 0)),
        out_shape=jax.ShapeDtypeStruct((b, l, S5_DIM), BF16),
        compiler_params=_cparams(("parallel",)),
        name="s5_post",
    )(yf, yb, proj, d_skip, w_glu, b_glu)


def _s5_params(a_re, a_im, log_step, b_re, b_im, c_re, c_im):
    dt = jnp.exp(log_step)[..., None]
    z_re, z_im = a_re * dt, a_im * dt
    mag = jnp.exp(z_re)
    lam_re, lam_im = mag * jnp.cos(z_im), mag * jnp.sin(z_im)
    den = a_re * a_re + a_im * a_im
    q_re = ((lam_re - 1.0) * a_re + lam_im * a_im) / den
    q_im = (lam_im * a_re - (lam_re - 1.0) * a_im) / den
    bb_re = q_re[..., None] * b_re - q_im[..., None] * b_im
    bb_im = q_re[..., None] * b_im + q_im[..., None] * b_re
    gb = S5_GROUPS // S5_SPLIT
    eye = jnp.eye(gb, dtype=F32)

    def dense_b(x):
        x = x.reshape(2, S5_SPLIT, gb, S5_STATE, S5_CH)
        return jnp.einsum('dhgpc,gk->dhgckp', x, eye).reshape(2, S5_SPLIT, gb * S5_CH, gb * S5_STATE)

    def dense_c(x):
        x = x.reshape(S5_SPLIT, gb, S5_CH, S5_STATE)
        return jnp.einsum('hgcp,gk->hgpkc', x, eye).reshape(S5_SPLIT, gb * S5_STATE, gb * S5_CH)

    bdense = jnp.concatenate([dense_b(bb_re), dense_b(bb_im)], axis=-1).astype(BF16)
    cdense = jnp.concatenate([dense_c(c_re), -dense_c(c_im)], axis=1).astype(BF16)
    lam = jnp.stack([lam_re.reshape(2, S5_STATES), lam_im.reshape(2, S5_STATES)], axis=1)
    lam = jnp.broadcast_to(lam[:, :, None, :], (2, 2, S5_SEQ_PAD, S5_STATES))
    return bdense, cdense, lam


MERGE_TILE = 256


def _merge_kernel(x_ref, yf_ref, yb_ref, bonus_ref, g_ref, ys_ref, gate_ref,
                  lnw_ref, lnb_ref, ones_ref, pr_ref, ps_ref, wo_ref, o_ref):
    ones_bd = ones_ref[...]
    y = yf_ref[...] + yb_ref[...]
    mean = _head_sum(y, ones_bd) * (1.0 / HEAD_DIM)
    yc = y - mean
    var = _head_sum(yc * yc, ones_bd) * (1.0 / HEAD_DIM)
    yn = yc * lax.rsqrt(var + GN_EPS) * lnw_ref[...] + lnb_ref[...]
    yr = (yn + bonus_ref[...]) * g_ref[...]
    y_rwkv = _dot(yr, pr_ref[...])
    y_s5 = jnp.dot(ys_ref[...], ps_ref[...], preferred_element_type=F32)
    gates = _sigmoid(gate_ref[...])
    merged = gates[:, :D_MODEL] * y_rwkv + gates[:, D_MODEL:] * y_s5
    o_ref[...] = x_ref[...] + _dot(merged, wo_ref[...])


def _merge(x, yf, yb, bonus, g, ys, proj, ln_w, ln_b, ones_bd, proj_rwkv, proj_s5, w_out):
    n, d = x.shape
    tm = MERGE_TILE
    assert PROJ_GATE_OFF == 2 * d and proj.shape[1] == PROJ_GATE_OFF + 2 * d
    tok = lambda c: pl.BlockSpec((tm, c), lambda i: (i, 0))
    const = lambda a: pl.BlockSpec(a.shape, lambda i: (0, 0), pipeline_mode=pl.Buffered(1))
    return pl.pallas_call(
        _merge_kernel,
        grid=(n // tm,),
        in_specs=[tok(d), tok(RWKV_DIM), tok(RWKV_DIM), tok(RWKV_DIM), tok(RWKV_DIM), tok(S5_DIM),
                  pl.BlockSpec((tm, 2 * d), lambda i: (i, 1)),
                  const(ln_w), const(ln_b), const(ones_bd), const(proj_rwkv), const(proj_s5), const(w_out)],
        out_specs=tok(d),
        out_shape=jax.ShapeDtypeStruct((n, d), F32),
        compiler_params=_cparams(("parallel",)),
        name="merge",
    )(x, yf, yb, bonus, g, ys, proj, ln_w, ln_b, ones_bd, proj_rwkv, proj_s5, w_out)


def _cast_pad_kernel(x_ref, o_ref, *, valid_tiles):
    cols = x_ref.shape[1]

    @pl.when(pl.program_id(0) < valid_tiles)
    def _():
        o_ref[:, :cols] = x_ref[...].astype(BF16)
        if o_ref.shape[1] > cols:
            o_ref[:, cols:] = jnp.zeros((o_ref.shape[0], o_ref.shape[1] - cols), BF16)

    @pl.when(pl.program_id(0) >= valid_tiles)
    def _():
        o_ref[...] = jnp.zeros_like(o_ref)


def _pad_ff(w, axis):
    rows, cols = w.shape
    pad = (-w.shape[axis]) % FF_TILE
    out_rows, out_cols = (rows + pad, cols) if axis == 0 else (rows, cols + pad)
    tr = LANES if axis == 0 else 2 * LANES
    assert rows % tr == 0 and out_rows % tr == 0 and cols % LANES == 0
    valid_tiles = rows // tr
    return pl.pallas_call(
        functools.partial(_cast_pad_kernel, valid_tiles=valid_tiles),
        grid=(out_rows // tr,),
        in_specs=[pl.BlockSpec((tr, cols), lambda i: (jnp.minimum(i, valid_tiles - 1), 0))],
        out_specs=pl.BlockSpec((tr, out_cols), lambda i: (i, 0)),
        out_shape=jax.ShapeDtypeStruct((out_rows, out_cols), BF16),
        compiler_params=_cparams(("parallel",)),
        name="cast_pad",
    )(w)


def _proj_weight_kernel(w_ref, o_ref):
    o_ref[:, :SHIFT_COLS] = w_ref[:, :SHIFT_COLS].astype(BF16)
    o_ref[:, SHIFT_COLS:PROJ_U_OFF] = jnp.zeros((o_ref.shape[0], PROJ_U_OFF - SHIFT_COLS), BF16)
    o_ref[:, PROJ_U_OFF:] = w_ref[:, SHIFT_COLS:].astype(BF16)


def _proj_weight(w_in):
    d, cols = w_in.shape
    assert cols + PROJ_U_OFF - SHIFT_COLS == PROJ_COLS
    tr = 2 * LANES
    return pl.pallas_call(
        _proj_weight_kernel,
        grid=(d // tr,),
        in_specs=[pl.BlockSpec((tr, cols), lambda i: (i, 0))],
        out_specs=pl.BlockSpec((tr, PROJ_COLS), lambda i: (i, 0)),
        out_shape=jax.ShapeDtypeStruct((d, PROJ_COLS), BF16),
        compiler_params=_cparams(("parallel",)),
        name="proj_weight",
    )(w_in)


def _lora_cat(w):
    z = jnp.zeros_like(w[0])
    return jnp.concatenate([jnp.concatenate([w[0], z], axis=1),
                            jnp.concatenate([z, w[1]], axis=1)], axis=0).astype(BF16)


def _forward(x_a, x_b, norm_ffn1, ffn1_w_gate, ffn1_w_up, ffn1_w_down, norm_mix, w_in, shift_mu,
             rwkv_w0, rwkv_w2, rwkv_a0, rwkv_a2, rwkv_g2, rwkv_k_k, rwkv_k_a, rwkv_r_k,
             rwkv_ln_w, rwkv_ln_b, s5_a_re, s5_a_im, s5_log_step, s5_b_re, s5_b_im,
             s5_c_re, s5_c_im, s5_d, s5_w_glu, s5_b_glu, proj_rwkv, proj_s5, w_out,
             norm_ffn2, ffn2_w_gate, ffn2_w_up, ffn2_w_down, norm_final):
    l, d = x_a.shape[1:]
    b = x_a.shape[0] + x_b.shape[0]
    n_a, n_b = x_a.shape[0] * l, x_b.shape[0] * l
    n = n_a + n_b
    row = lambda p: p.reshape(1, -1)

    (x1,) = _ffn([x_a.reshape(n_a, d), x_b.reshape(n_b, d)], row(norm_ffn1), _pad_ff(ffn1_w_gate, 1),
                 _pad_ff(ffn1_w_up, 1), _pad_ff(ffn1_w_down, 0), row(norm_final),
                 out_rows=[n], final_norm=False)

    assert w_in.shape[1] == SHIFT_COLS + S5_DIM + 2 * d
    w_proj = _proj_weight(w_in)
    proj = _in_proj(x1, row(norm_mix), w_proj)
    proj3 = proj.reshape(b, l, PROJ_COLS)

    head_id = jnp.arange(QUAD) // HEAD_DIM
    ones_bd = (head_id[:, None] == head_id[None, :]).astype(BF16)
    at, rt, bt, kt, wt, v, g, bonus = _rwkv_prep(
        proj3, row(shift_mu),
        _lora_cat(rwkv_w2), rwkv_w0.reshape(1, -1), _lora_cat(rwkv_a2), rwkv_a0.reshape(1, -1),
        rwkv_g2.astype(BF16), row(rwkv_k_k), row(rwkv_k_a), row(rwkv_r_k), ones_bd)
    yf, yb = _rwkv_scan(at, rt, bt, kt, wt, v)

    bdense, cdense, lam = _s5_params(s5_a_re, s5_a_im, s5_log_step, s5_b_re, s5_b_im, s5_c_re, s5_c_im)
    ysf, ysb = _s5_scan(proj3, bdense, cdense, lam)
    ys = _s5_post(ysf, ysb, proj3, row(s5_d), s5_w_glu.astype(BF16), row(s5_b_glu)).reshape(n, S5_DIM)

    flat = lambda a: a.reshape(n, RWKV_DIM)
    x2 = _merge(x1, flat(yf), flat(yb), flat(bonus), flat(g), ys, proj, row(rwkv_ln_w), row(rwkv_ln_b),
                ones_bd, proj_rwkv.astype(BF16), proj_s5.astype(BF16), w_out.astype(BF16))
    y_a, y_b = _ffn([x2], row(norm_ffn2), _pad_ff(ffn2_w_gate, 1), _pad_ff(ffn2_w_up, 1),
                    _pad_ff(ffn2_w_down, 0), row(norm_final), out_rows=[n_a, n_b], final_norm=True)
    return y_a.reshape(x_a.shape), y_b.reshape(x_b.shape)


def kernel(x_prompt, x_sample, norm_ffn1, ffn1_w_gate, ffn1_w_up, ffn1_w_down, norm_mix, w_in, shift_mu, rwkv_w0, rwkv_w2, rwkv_a0, rwkv_a2, rwkv_g2, rwkv_k_k, rwkv_k_a, rwkv_r_k, rwkv_ln_w, rwkv_ln_b, s5_a_re, s5_a_im, s5_log_step, s5_b_re, s5_b_im, s5_c_re, s5_c_im, s5_d, s5_w_glu, s5_b_glu, proj_rwkv, proj_s5, w_out, norm_ffn2, ffn2_w_gate, ffn2_w_up, ffn2_w_down, norm_final):
    layer = (norm_ffn1, ffn1_w_gate, ffn1_w_up, ffn1_w_down, norm_mix, w_in, shift_mu,
             rwkv_w0, rwkv_w2, rwkv_a0, rwkv_a2, rwkv_g2, rwkv_k_k, rwkv_k_a, rwkv_r_k,
             rwkv_ln_w, rwkv_ln_b, s5_a_re, s5_a_im, s5_log_step, s5_b_re, s5_b_im,
             s5_c_re, s5_c_im, s5_d, s5_w_glu, s5_b_glu, proj_rwkv, proj_s5, w_out,
             norm_ffn2, ffn2_w_gate, ffn2_w_up, ffn2_w_down)
    assert all(p.shape[0] == 1 for p in layer), "single-layer block"
    assert x_prompt.shape[1:] == x_sample.shape[1:]
    return _forward(x_prompt, x_sample, *[p[0] for p in layer], norm_final)
```

```python
import functools
import math

import jax
import jax.numpy as jnp
from jax import lax
from jax.experimental import pallas as pl
from jax.experimental.pallas import tpu as pltpu

F32 = jnp.float32
BF16 = jnp.bfloat16

D_MODEL = 2048
D_FF = 5504
HEADS = 16
HEAD_DIM = 64
RWKV_DIM = HEADS * HEAD_DIM
LORA = 64
GATE_LORA = 128
S5_GROUPS = 32
S5_CH = 16
S5_DIM = S5_GROUPS * S5_CH
S5_STATE = 64
S5_STATES = S5_GROUPS * S5_STATE
SHIFT_COLS = 3 * RWKV_DIM + 4 * LORA + GATE_LORA
RMS_EPS = 1e-6
GN_EPS = 64e-5
KK_EPS = 1e-12

LANES = 128
SUBLANES = 8
VMEM_LIMIT = 56 * 1024 * 1024

FF_TILE = 512
TOK_TILE = 512
CHUNK = 64
QUAD = 4 * HEAD_DIM
SCAN_HEADS = 2
GROUP = SCAN_HEADS * HEAD_DIM
SCAN_CHUNKS = 4
S5_SEQ_PAD = SUBLANES
S5_TILE = 64
S5_SPLIT = 2
PROJ_TOK_TILE = 1024
PROJ_COL_TILE = 1024
PROJ_U_OFF = 3584
PROJ_GATE_OFF = 4096
PROJ_COLS = 8192


def _cparams(sem):
    return pltpu.CompilerParams(dimension_semantics=sem, vmem_limit_bytes=VMEM_LIMIT)


def _dot(a, b):
    return jnp.dot(a.astype(BF16), b.astype(BF16), preferred_element_type=F32)


def _dot_nt(a, b):
    return lax.dot_general(a.astype(BF16), b.astype(BF16), (((1,), (1,)), ((), ())),
                           preferred_element_type=F32)


def _dot_tn(a, b):
    return lax.dot_general(a.astype(BF16), b.astype(BF16), (((0,), (0,)), ((), ())),
                           preferred_element_type=F32)


def _split_dot(x, w, pieces):
    acc = None
    rem = x
    for _ in range(pieces):
        part = rem.astype(BF16)
        term = jnp.dot(part, w, preferred_element_type=F32)
        acc = term if acc is None else acc + term
        rem = rem - part.astype(F32)
    return acc


def _sigmoid(x):
    return 1.0 / (1.0 + jnp.exp(-x))


def _rms(x, g):
    ms = jnp.mean(x * x, axis=-1, keepdims=True)
    return x * lax.rsqrt(ms + RMS_EPS) * g


def _ffn_kernel(*refs, n_x, n_out, first_tiles, final_norm):
    x_refs = refs[:n_x]
    g_ref, wg_ref, wu_ref, wd_ref, wgt_ref, wut_ref, wdt_ref, gf_ref = refs[n_x:n_x + 8]
    o_refs = refs[n_x + 8:n_x + 8 + n_out]
    xn_ref, acc_ref = refs[n_x + 8 + n_out:]
    i = pl.program_id(0)
    j = pl.program_id(1)
    last = j == pl.num_programs(1) - 1
    in_first = i < first_tiles
    parts = [(in_first, x_refs[0], o_refs[0]), (jnp.logical_not(in_first), x_refs[-1], o_refs[-1])]
    if n_x == 1 and n_out == 1:
        parts = [(True, x_refs[0], o_refs[0])]

    for cond, x_ref, _ in parts:
        @pl.when((j == 0) & cond)
        def _(x_ref=x_ref):
            xn_ref[...] = _rms(x_ref[...], g_ref[...]).astype(BF16)
            acc_ref[...] = jnp.zeros_like(acc_ref)

    def swiglu_down(wg, wu, wd):
        xn = xn_ref[...]
        hg = jnp.dot(xn, wg[...], preferred_element_type=F32)
        hu = jnp.dot(xn, wu[...], preferred_element_type=F32)
        act = (hg * _sigmoid(hg)) * hu
        return jnp.dot(act.astype(BF16), wd[...], preferred_element_type=F32)

    acc_ref[...] += swiglu_down(wg_ref, wu_ref, wd_ref)

    @pl.when(last)
    def _():
        acc_ref[...] += swiglu_down(wgt_ref, wut_ref, wdt_ref)

    for cond, x_ref, o_ref in parts:
        @pl.when(last & cond)
        def _(x_ref=x_ref, o_ref=o_ref):
            y = x_ref[...] + 0.5 * acc_ref[...]
            o_ref[...] = _rms(y, gf_ref[...]) if final_norm else y


def _ffn(xs, g, wg, wu, wd, gf, *, out_rows, final_norm):
    d = xs[0].shape[1]
    tm, tf = TOK_TILE, FF_TILE
    f = wg.shape[1] // tf * tf
    assert 0 < wg.shape[1] - f and (wg.shape[1] - f) % LANES == 0
    wgt, wut, wdt = wg[:, f:], wu[:, f:], wd[f:, :]
    const = lambda a: pl.BlockSpec(a.shape, lambda i, j: (0, 0), pipeline_mode=pl.Buffered(1))
    n = sum(x.shape[0] for x in xs)
    assert n == sum(out_rows) and all(r % tm == 0 for r in out_rows)
    assert all(x.shape[0] % tm == 0 for x in xs)
    first_rows = xs[0].shape[0] if len(xs) == 2 else out_rows[0]
    if len(xs) == 2 and len(out_rows) == 2:
        assert xs[0].shape[0] == out_rows[0]
    first_tiles = first_rows // tm

    def split_specs(count):
        if count == 1:
            return [pl.BlockSpec((tm, d), lambda i, j: (i, 0))]
        return [pl.BlockSpec((tm, d), lambda i, j: (jnp.minimum(i, first_tiles - 1), 0)),
                pl.BlockSpec((tm, d), lambda i, j: (jnp.maximum(i - first_tiles, 0), 0))]

    outs = pl.pallas_call(
        functools.partial(_ffn_kernel, n_x=len(xs), n_out=len(out_rows), first_tiles=first_tiles,
                          final_norm=final_norm),
        grid=(n // tm, f // tf),
        in_specs=split_specs(len(xs)) + [
            pl.BlockSpec((1, d), lambda i, j: (0, 0)),
            pl.BlockSpec((d, tf), lambda i, j: (0, j)),
            pl.BlockSpec((d, tf), lambda i, j: (0, j)),
            pl.BlockSpec((tf, d), lambda i, j: (j, 0)),
            const(wgt), const(wut), const(wdt),
            pl.BlockSpec((1, d), lambda i, j: (0, 0)),
        ],
        out_specs=split_specs(len(out_rows)),
        out_shape=[jax.ShapeDtypeStruct((r, d), F32) for r in out_rows],
        scratch_shapes=[pltpu.VMEM((tm, d), BF16), pltpu.VMEM((tm, d), F32)],
        compiler_params=_cparams(("arbitrary", "arbitrary")),
        name="ffn_final" if final_norm else "ffn",
    )(*xs, g, wg, wu, wd, wgt, wut, wdt, gf)
    return outs


def _in_proj_kernel(x_ref, g_ref, w_ref, o_ref, xn_ref):
    @pl.when(pl.program_id(1) == 0)
    def _():
        xn_ref[...] = _rms(x_ref[...], g_ref[...]).astype(BF16)

    o_ref[...] = jnp.dot(xn_ref[...], w_ref[...], preferred_element_type=F32)


def _in_proj(x, g, w):
    n, d = x.shape
    c = w.shape[1]
    tm, tn = PROJ_TOK_TILE, PROJ_COL_TILE
    return pl.pallas_call(
        _in_proj_kernel,
        grid=(n // tm, c // tn),
        in_specs=[
            pl.BlockSpec((tm, d), lambda i, j: (i, 0)),
            pl.BlockSpec((1, d), lambda i, j: (0, 0)),
            pl.BlockSpec((d, tn), lambda i, j: (0, j)),
        ],
        out_specs=pl.BlockSpec((tm, tn), lambda i, j: (i, j)),
        out_shape=jax.ShapeDtypeStruct((n, c), F32),
        scratch_shapes=[pltpu.VMEM((tm, d), BF16)],
        compiler_params=_cparams(("parallel", "arbitrary")),
        name="proj_in",
    )(x, g, w)


PREP_TILE = 256


def _head_sum(x, ones_quad):
    return jnp.concatenate([_split_dot(x[:, c:c + QUAD], ones_quad, 2)
                            for c in range(0, x.shape[1], QUAD)], axis=1)


def _rwkv_prep_kernel(main_ref, prev_ref, next_ref, mu_ref, w2_ref, w0_ref, a2_ref, a0_ref, g2_ref,
                      kk_ref, ka_ref, rk_ref, ones_ref, tri_ref,
                      at_out, rt_out, bt_out, kt_out, wt_out, v_out, g_out, bonus_out):
    i = pl.program_id(1)
    tl = main_ref.shape[1]
    x = main_ref[0]
    row = lax.broadcasted_iota(jnp.int32, (tl, 1), 0)
    before = jnp.where(i == 0, 0.0, prev_ref[0, SUBLANES - 1:SUBLANES, :])
    after = jnp.where(i == pl.num_programs(1) - 1, 0.0, next_ref[0, 0:1, :])
    prev = jnp.where(row == 0, before, pltpu.roll(x, 1, 0))
    nxt = jnp.where(row == tl - 1, after, pltpu.roll(x, tl - 1, 0))
    p = x + (0.5 * (prev + nxt) - x) * mu_ref[...]

    c0 = 3 * RWKV_DIM
    r = p[:, 0:RWKV_DIM]
    k = p[:, RWKV_DIM:2 * RWKV_DIM]
    v = p[:, 2 * RWKV_DIM:c0]
    wlow = jnp.tanh(p[:, c0:c0 + 2 * LORA])
    alow = p[:, c0 + 2 * LORA:c0 + 4 * LORA]
    glow = p[:, c0 + 4 * LORA:c0 + 4 * LORA + GATE_LORA]
    ones_bd = ones_ref[...]

    wpre = _dot(wlow, w2_ref[...]) + w0_ref[...]
    lw = (-math.exp(-0.5) * math.log2(math.e)) * _sigmoid(wpre)
    a = _sigmoid(_dot(alow, a2_ref[...]) + a0_ref[...])

    kk = k * kk_ref[...]
    kkn = kk * lax.rsqrt(_head_sum(kk * kk, ones_bd) + KK_EPS)
    ka = ka_ref[...]
    ksum = None
    n_chunks = tl // CHUNK
    for d in range(2):
        a_d = a[:, d * RWKV_DIM:(d + 1) * RWKV_DIM]
        lw_d = lw[:, d * RWKV_DIM:(d + 1) * RWKV_DIM]
        kd = k * (1.0 + (a_d - 1.0) * ka)
        cum = _cumsum(tri_ref[d], lw_d)
        e_out = jnp.exp2(-cum)
        at_out[d, 0] = (-kkn * jnp.exp2(cum - lw_d)).astype(BF16)
        rt_out[d, 0] = (r * jnp.exp2(cum)).astype(BF16)
        bt_out[d, 0] = (kkn * a_d * e_out).astype(BF16)
        kt_out[d, 0] = (kd * e_out).astype(BF16)
        total = jnp.exp2(jnp.sum(lw_d.reshape(n_chunks, CHUNK, RWKV_DIM), axis=1))
        wt_out[d, 0] = jnp.broadcast_to(total[:, None, :], (n_chunks, SUBLANES, RWKV_DIM)).reshape(
            n_chunks * SUBLANES, RWKV_DIM)
        ksum = kd if ksum is None else ksum + kd
    v_out[0] = v.astype(BF16)
    g_out[0] = _dot(_sigmoid(glow), g2_ref[...])
    bonus_out[0] = _head_sum(r * ksum * rk_ref[...], ones_bd) * v


def _rwkv_prep(pshift, mu, w2cat, w0cat, a2cat, a0cat, g2, k_k, k_a, r_k, ones_bd):
    b, l, _ = pshift.shape
    c = SHIFT_COLS
    tl = PREP_TILE
    nt = l // tl
    hb = tl // SUBLANES
    nhb = l // SUBLANES
    full = lambda shape: pl.BlockSpec(shape, lambda bi, i: (0,) * len(shape))
    tok = pl.BlockSpec((1, tl, RWKV_DIM), lambda bi, i: (bi, i, 0))
    tok2 = pl.BlockSpec((2, 1, tl, RWKV_DIM), lambda bi, i: (0, bi, i, 0))
    wrows = tl // CHUNK * SUBLANES
    tokw = pl.BlockSpec((2, 1, wrows, RWKV_DIM), lambda bi, i: (0, bi, i, 0))
    s1 = jax.ShapeDtypeStruct((b, l, RWKV_DIM), F32)
    s1h = jax.ShapeDtypeStruct((b, l, RWKV_DIM), BF16)
    s2h = jax.ShapeDtypeStruct((2, b, l, RWKV_DIM), BF16)
    sw = jax.ShapeDtypeStruct((2, b, l // CHUNK * SUBLANES, RWKV_DIM), F32)
    pos = jnp.arange(tl)
    same_chunk = (pos[:, None] // CHUNK) == (pos[None, :] // CHUNK)
    tri = jnp.stack([same_chunk & (pos[None, :] <= pos[:, None]),
                     same_chunk & (pos[None, :] >= pos[:, None])]).astype(BF16)
    return pl.pallas_call(
        _rwkv_prep_kernel,
        grid=(b, nt),
        in_specs=[
            pl.BlockSpec((1, tl, c), lambda bi, i: (bi, i, 0)),
            pl.BlockSpec((1, SUBLANES, c), lambda bi, i: (bi, jnp.maximum(i * hb - 1, 0), 0)),
            pl.BlockSpec((1, SUBLANES, c), lambda bi, i: (bi, jnp.minimum((i + 1) * hb, nhb - 1), 0)),
            full((1, c)),
            full(w2cat.shape), full(w0cat.shape), full(a2cat.shape), full(a0cat.shape), full(g2.shape),
            full((1, RWKV_DIM)), full((1, RWKV_DIM)), full((1, RWKV_DIM)), full(ones_bd.shape),
            full(tri.shape),
        ],
        out_specs=[tok2, tok2, tok2, tok2, tokw, tok, tok, tok],
        out_shape=[s2h, s2h, s2h, s2h, sw, s1h, s1, s1],
        compiler_params=_cparams(("parallel", "arbitrary")),
        name="rwkv_prep",
    )(pshift, pshift, pshift, mu, w2cat, w0cat, a2cat, a0cat, g2, k_k, k_a, r_k, ones_bd, tri)


def _stack_heads(x, head_masks):
    return jnp.concatenate([jnp.where(m, x, 0.0) for m in head_masks], axis=0)


def _rwkv_scan_kernel(atf_ref, rtf_ref, btf_ref, ktf_ref, wtf_ref, vf_ref,
                      atb_ref, rtb_ref, btb_ref, ktb_ref, wtb_ref, vb_ref,
                      of_ref, ob_ref, s_ref):
    t = CHUNK

    @pl.when(pl.program_id(1) == 0)
    def _():
        s_ref[...] = jnp.zeros_like(s_ref)

    row = lax.broadcasted_iota(jnp.int32, (t, GROUP), 0)
    lane = lax.broadcasted_iota(jnp.int32, (t, GROUP), 1)
    col = lane & (HEAD_DIM - 1)
    same16 = (row >> 4) == (col >> 4)
    same32 = (row >> 5) == (col >> 5)
    in32_off16 = same32 & jnp.logical_not(same16)
    eye = jnp.where(col == row, 1.0, 0.0)
    head_masks = [(lane >> 6) == h for h in range(SCAN_HEADS)]
    strict = (col < row, col > row)
    incl = (col <= row, col >= row)
    qrow = lax.broadcasted_iota(jnp.int32, (GROUP, GROUP), 0) >> 6
    qcol = lax.broadcasted_iota(jnp.int32, (GROUP, GROUP), 1) >> 6
    same_head = qrow == qcol
    stack = lambda x: _stack_heads(x, head_masks)
    cat = lambda a, b: jnp.concatenate([a, b], axis=0)

    refs = ((atf_ref, rtf_ref, btf_ref, ktf_ref, wtf_ref, vf_ref, of_ref),
            (atb_ref, rtb_ref, btb_ref, ktb_ref, wtb_ref, vb_ref, ob_ref))
    n_sub = vf_ref.shape[1] // t
    chains = [(d, q, c) for c in range(n_sub) for q in range(RWKV_DIM // GROUP) for d in range(2)]
    each = lambda fn, *lists: [fn(*args) for args in zip(*lists)]
    sl = lambda q: slice(q * GROUP, (q + 1) * GROUP)

    def chunk(d, c):
        return c if d == 0 else n_sub - 1 - c

    def rows(d, c):
        return slice(chunk(d, c) * t, (chunk(d, c) + 1) * t)

    load = lambda j: [refs[d][j][0, 0, rows(d, c), sl(q)].astype(F32) for d, q, c in chains]
    at, rt, bt, kt = load(0), load(1), load(2), load(3)
    v = [refs[d][5][0, rows(d, c), sl(q)].astype(F32) for d, q, c in chains]
    w_total = [refs[d][4][0, 0, chunk(d, c) * SUBLANES:chunk(d, c) * SUBLANES + 1, sl(q)]
               for d, q, c in chains]
    lhs = each(cat, at, rt)
    bk = each(cat, bt, kt)

    gram = each(lambda l_, b_, k_: _dot_nt(l_, cat(stack(b_), stack(k_))), lhs, bt, kt)
    a_ab = [jnp.where(strict[d], g[:t, :GROUP], 0.0) for (d, _, _), g in zip(chains, gram)]
    a_ak = [jnp.where(strict[d], g[:t, GROUP:], 0.0) for (d, _, _), g in zip(chains, gram)]
    a_rb = [jnp.where(incl[d], g[t:, :GROUP], 0.0) for (d, _, _), g in zip(chains, gram)]
    a_rk = [jnp.where(incl[d], g[t:, GROUP:], 0.0) for (d, _, _), g in zip(chains, gram)]
    from_v = each(lambda p, q_, v_: _dot(cat(p, q_), stack(v_)), a_ak, a_rk, v)

    a_d = each(lambda a: jnp.where(same16, a, 0.0), a_ab)
    pw = each(lambda a: eye + a, a_d)
    sq = each(lambda a: _dot(a, stack(a)), a_d)
    for _ in range(2):
        both = each(lambda p, s: _dot(cat(p, s), stack(s)), pw, sq)
        pw = each(lambda p, bo: p + bo[:t], pw, both)
        sq = each(lambda bo: bo[t:], both)
    x0 = each(lambda p, s: p + _dot(p, stack(s)), pw, sq)
    m1 = each(lambda x, a: _dot(x, stack(jnp.where(in32_off16, a, 0.0))), x0, a_ab)
    x1 = each(lambda x, m: x + _dot(m, stack(x)), x0, m1)
    m2 = each(lambda x, a: _dot(x, stack(jnp.where(same32, 0.0, a))), x1, a_ab)
    x2 = each(lambda x, m: x + _dot(m, stack(x)), x1, m2)

    per_chunk = len(chains) // n_sub
    state = [s_ref[d, q] for d, q, _ in chains[:per_chunk]]
    for c in range(n_sub):
        pick = lambda xs: xs[c * per_chunk:(c + 1) * per_chunk]
        from_state = each(_dot_nt, pick(lhs), state)
        u = each(lambda x, fs, fv: _dot(x, stack(fs[:t] + fv[:t])), pick(x2), from_state, pick(from_v))
        y = each(lambda fs, fv, a, u_: fs[t:] + fv[t:] + _dot(a, stack(u_)),
                 from_state, pick(from_v), pick(a_rb), u)
        upd = each(lambda u_, v_, bk_: _dot_tn(cat(u_, v_), bk_), u, pick(v), pick(bk))
        for (d, q, _), y_ in zip(pick(chains), y):
            refs[d][6][0, rows(d, c), sl(q)] = y_
        state = each(lambda s_, up, w: (s_ + jnp.where(same_head, up, 0.0)) * w, state, upd, pick(w_total))
    for (d, q, _), s_ in zip(chains[:per_chunk], state):
        s_ref[d, q] = s_


def _cumsum(tri, lw):
    acc = None
    rem = lw
    for _ in range(3):
        part = rem.astype(BF16)
        term = jnp.dot(tri, part, preferred_element_type=F32)
        acc = term if acc is None else acc + term
        rem = rem - part.astype(F32)
    return acc


def _rwkv_scan(at, rt, bt, kt, wt, v):
    b, l, c = v.shape
    tl = SCAN_CHUNKS * CHUNK
    nc = l // tl
    wrows = SCAN_CHUNKS * SUBLANES
    fwd = pl.BlockSpec((1, tl, c), lambda bi, i: (bi, i, 0))
    bwd = pl.BlockSpec((1, tl, c), lambda bi, i: (bi, nc - 1 - i, 0))
    fwd_d = pl.BlockSpec((1, 1, tl, c), lambda bi, i: (0, bi, i, 0))
    bwd_d = pl.BlockSpec((1, 1, tl, c), lambda bi, i: (1, bi, nc - 1 - i, 0))
    fwd_w = pl.BlockSpec((1, 1, wrows, c), lambda bi, i: (0, bi, i, 0))
    bwd_w = pl.BlockSpec((1, 1, wrows, c), lambda bi, i: (1, bi, nc - 1 - i, 0))
    return pl.pallas_call(
        _rwkv_scan_kernel,
        grid=(b, nc),
        in_specs=[fwd_d, fwd_d, fwd_d, fwd_d, fwd_w, fwd, bwd_d, bwd_d, bwd_d, bwd_d, bwd_w, bwd],
        out_specs=[fwd, bwd],
        out_shape=[jax.ShapeDtypeStruct((b, l, c), F32)] * 2,
        scratch_shapes=[pltpu.VMEM((2, c // GROUP, GROUP, GROUP), F32)],
        compiler_params=_cparams(("parallel", "arbitrary")),
        name="rwkv_scan",
    )(at, rt, bt, kt, wt, v, at, rt, bt, kt, wt, v)


def _s5_scan_kernel(uf_ref, ub_ref, bdense_ref, cdense_ref, lam_ref, of_ref, ob_ref,
                    ubuf_ref, st_ref, s_ref):
    n_seq, tl = uf_ref.shape[0], uf_ref.shape[1]
    ns = S5_STATES
    u_refs = (uf_ref, ub_ref)
    o_refs = (of_ref, ob_ref)

    @pl.when(pl.program_id(0) == 0)
    def _():
        s_ref[...] = jnp.zeros_like(s_ref)
        ubuf_ref[...] = jnp.zeros_like(ubuf_ref)

    rows = tl * S5_SEQ_PAD
    hs = ns // S5_SPLIT
    hc = S5_DIM // S5_SPLIT

    def project_in(d):
        for b in range(n_seq):
            ubuf_ref[d, :, b, :] = u_refs[d][b]
        u = ubuf_ref[d].reshape(rows, S5_DIM)
        for h in range(S5_SPLIT):
            bu = _dot(u[:, h * hc:(h + 1) * hc], bdense_ref[d, h])
            st_ref[d, :, :, h * hs:(h + 1) * hs] = bu[:, :hs].reshape(tl, S5_SEQ_PAD, hs)
            st_ref[d, :, :, ns + h * hs:ns + (h + 1) * hs] = bu[:, hs:].reshape(tl, S5_SEQ_PAD, hs)

    def recur(d):
        lam_re = lam_ref[d, 0]
        lam_im = lam_ref[d, 1]
        s_re, s_im = s_ref[d, 0], s_ref[d, 1]
        for j in range(tl):
            tt = j if d == 0 else tl - 1 - j
            s_re, s_im = (lam_re * s_re - lam_im * s_im + st_ref[d, tt, :, :ns],
                          lam_re * s_im + lam_im * s_re + st_ref[d, tt, :, ns:])
            st_ref[d, tt, :, :ns] = s_re
            st_ref[d, tt, :, ns:] = s_im
        s_ref[d, 0] = s_re
        s_ref[d, 1] = s_im

    def project_out(d):
        for h in range(S5_SPLIT):
            st_re = st_ref[d, :, :, h * hs:(h + 1) * hs].reshape(rows, hs)
            st_im = st_ref[d, :, :, ns + h * hs:ns + (h + 1) * hs].reshape(rows, hs)
            y = _dot(st_re, cdense_ref[h, :hs]) + _dot(st_im, cdense_ref[h, hs:])
            o_refs[d][:, :, h * hc:(h + 1) * hc] = y.reshape(tl, S5_SEQ_PAD, hc)

    project_in(0)
    project_in(1)
    recur(0)
    project_out(0)
    recur(1)
    project_out(1)


def _s5_scan(proj, bdense, cdense, lam):
    b, l, _ = proj.shape
    assert b <= S5_SEQ_PAD and PROJ_U_OFF % S5_DIM == 0
    tl = S5_TILE
    nt = l // tl
    ucol = PROJ_U_OFF // S5_DIM
    const = lambda a: pl.BlockSpec(a.shape, lambda i: (0,) * a.ndim, pipeline_mode=pl.Buffered(1))
    out = jax.ShapeDtypeStruct((l, S5_SEQ_PAD, S5_DIM), F32)
    return pl.pallas_call(
        _s5_scan_kernel,
        grid=(nt,),
        in_specs=[
            pl.BlockSpec((b, tl, S5_DIM), lambda i: (0, i, ucol)),
            pl.BlockSpec((b, tl, S5_DIM), lambda i: (0, nt - 1 - i, ucol)),
            const(bdense), const(cdense), const(lam),
        ],
        out_specs=[pl.BlockSpec((tl, S5_SEQ_PAD, S5_DIM), lambda i: (i, 0, 0)),
                   pl.BlockSpec((tl, S5_SEQ_PAD, S5_DIM), lambda i: (nt - 1 - i, 0, 0))],
        out_shape=[out, out],
        scratch_shapes=[pltpu.VMEM((2, tl, S5_SEQ_PAD, S5_DIM), F32),
                        pltpu.VMEM((2, tl, S5_SEQ_PAD, 2 * S5_STATES), F32),
                        pltpu.VMEM((2, 2, S5_SEQ_PAD, S5_STATES), F32)],
        compiler_params=_cparams(("arbitrary",)),
        name="s5_scan",
    )(proj, proj, bdense, cdense, lam)


def _s5_post_kernel(yf_ref, yb_ref, u_ref, dskip_ref, wglu_ref, bglu_ref, o_ref):
    n_seq, tl = u_ref.shape[0], u_ref.shape[1]
    per_seq = lambda ref, b: ref[:, b, :]
    y = jnp.concatenate([per_seq(yf_ref, b) + per_seq(yb_ref, b) + dskip_ref[...] * u_ref[b]
                         for b in range(n_seq)], axis=0)
    y = 0.5 * y * (1.0 + jnp.tanh(math.sqrt(2.0 / math.pi) * (y + 0.044715 * (y * y * y))))
    gate = _sigmoid(_dot(y, wglu_ref[...]) + bglu_ref[...])
    o_ref[...] = (y * gate).astype(BF16).reshape(n_seq, tl, S5_DIM)


def _s5_post(yf, yb, proj, d_skip, w_glu, b_glu):
    b, l, _ = proj.shape
    tl = S5_TILE
    row = pl.BlockSpec((1, S5_DIM), lambda i: (0, 0))
    scan_out = pl.BlockSpec((tl, S5_SEQ_PAD, S5_DIM), lambda i: (i, 0, 0))
    return pl.pallas_call(
        _s5_post_kernel,
        grid=(l // tl,),
        in_specs=[
            scan_out, scan_out,
            pl.BlockSpec((b, tl, S5_DIM), lambda i: (0, i, PROJ_U_OFF // S5_DIM)),
            row,
            pl.BlockSpec((S5_DIM, S5_DIM), lambda i: (0, 0)),
            row,
        ],
        out_specs=pl.BlockSpec((b, tl, S5_DIM), lambda i: (0, i, 0)),
        out_shape=jax.ShapeDtypeStruct((b, l, S5_DIM), BF16),
        compiler_params=_cparams(("parallel",)),
        name="s5_post",
    )(yf, yb, proj, d_skip, w_glu, b_glu)


def _s5_params(a_re, a_im, log_step, b_re, b_im, c_re, c_im):
    dt = jnp.exp(log_step)[..., None]
    z_re, z_im = a_re * dt, a_im * dt
    mag = jnp.exp(z_re)
    lam_re, lam_im = mag * jnp.cos(z_im), mag * jnp.sin(z_im)
    den = a_re * a_re + a_im * a_im
    q_re = ((lam_re - 1.0) * a_re + lam_im * a_im) / den
    q_im = (lam_im * a_re - (lam_re - 1.0) * a_im) / den
    bb_re = q_re[..., None] * b_re - q_im[..., None] * b_im
    bb_im = q_re[..., None] * b_im + q_im[..., None] * b_re
    gb = S5_GROUPS // S5_SPLIT
    eye = jnp.eye(gb, dtype=F32)

    def dense_b(x):
        x = x.reshape(2, S5_SPLIT, gb, S5_STATE, S5_CH)
        return jnp.einsum('dhgpc,gk->dhgckp', x, eye).reshape(2, S5_SPLIT, gb * S5_CH, gb * S5_STATE)

    def dense_c(x):
        x = x.reshape(S5_SPLIT, gb, S5_CH, S5_STATE)
        return jnp.einsum('hgcp,gk->hgpkc', x, eye).reshape(S5_SPLIT, gb * S5_STATE, gb * S5_CH)

    bdense = jnp.concatenate([dense_b(bb_re), dense_b(bb_im)], axis=-1).astype(BF16)
    cdense = jnp.concatenate([dense_c(c_re), -dense_c(c_im)], axis=1).astype(BF16)
    lam = jnp.stack([lam_re.reshape(2, S5_STATES), lam_im.reshape(2, S5_STATES)], axis=1)
    lam = jnp.broadcast_to(lam[:, :, None, :], (2, 2, S5_SEQ_PAD, S5_STATES))
    return bdense, cdense, lam


MERGE_TILE = 256


def _merge_kernel(x_ref, yf_ref, yb_ref, bonus_ref, g_ref, ys_ref, gate_ref,
                  lnw_ref, lnb_ref, ones_ref, pr_ref, ps_ref, wo_ref, o_ref):
    ones_bd = ones_ref[...]
    y = yf_ref[...] + yb_ref[...]
    mean = _head_sum(y, ones_bd) * (1.0 / HEAD_DIM)
    yc = y - mean
    var = _head_sum(yc * yc, ones_bd) * (1.0 / HEAD_DIM)
    yn = yc * lax.rsqrt(var + GN_EPS) * lnw_ref[...] + lnb_ref[...]
    yr = (yn + bonus_ref[...]) * g_ref[...]
    y_rwkv = _dot(yr, pr_ref[...])
    y_s5 = jnp.dot(ys_ref[...], ps_ref[...], preferred_element_type=F32)
    gates = _sigmoid(gate_ref[...])
    merged = gates[:, :D_MODEL] * y_rwkv + gates[:, D_MODEL:] * y_s5
    o_ref[...] = x_ref[...] + _dot(merged, wo_ref[...])


def _merge(x, yf, yb, bonus, g, ys, proj, ln_w, ln_b, ones_bd, proj_rwkv, proj_s5, w_out):
    n, d = x.shape
    tm = MERGE_TILE
    assert PROJ_GATE_OFF == 2 * d and proj.shape[1] == PROJ_GATE_OFF + 2 * d
    tok = lambda c: pl.BlockSpec((tm, c), lambda i: (i, 0))
    const = lambda a: pl.BlockSpec(a.shape, lambda i: (0, 0), pipeline_mode=pl.Buffered(1))
    return pl.pallas_call(
        _merge_kernel,
        grid=(n // tm,),
        in_specs=[tok(d), tok(RWKV_DIM), tok(RWKV_DIM), tok(RWKV_DIM), tok(RWKV_DIM), tok(S5_DIM),
                  pl.BlockSpec((tm, 2 * d), lambda i: (i, 1)),
                  const(ln_w), const(ln_b), const(ones_bd), const(proj_rwkv), const(proj_s5), const(w_out)],
        out_specs=tok(d),
        out_shape=jax.ShapeDtypeStruct((n, d), F32),
        compiler_params=_cparams(("parallel",)),
        name="merge",
    )(x, yf, yb, bonus, g, ys, proj, ln_w, ln_b, ones_bd, proj_rwkv, proj_s5, w_out)


def _proj_weight_kernel(w_ref, o_ref):
    o_ref[:, :SHIFT_COLS] = w_ref[:, :SHIFT_COLS].astype(BF16)
    o_ref[:, SHIFT_COLS:PROJ_U_OFF] = jnp.zeros((o_ref.shape[0], PROJ_U_OFF - SHIFT_COLS), BF16)
    o_ref[:, PROJ_U_OFF:] = w_ref[:, SHIFT_COLS:].astype(BF16)


def _proj_weight(w_in):
    d, cols = w_in.shape
    assert cols + PROJ_U_OFF - SHIFT_COLS == PROJ_COLS
    tr = 2 * LANES
    return pl.pallas_call(
        _proj_weight_kernel,
        grid=(d // tr,),
        in_specs=[pl.BlockSpec((tr, cols), lambda i: (i, 0))],
        out_specs=pl.BlockSpec((tr, PROJ_COLS), lambda i: (i, 0)),
        out_shape=jax.ShapeDtypeStruct((d, PROJ_COLS), BF16),
        compiler_params=_cparams(("parallel",)),
        name="proj_weight",
    )(w_in)


def _lora_cat(w):
    z = jnp.zeros_like(w[0])
    return jnp.concatenate([jnp.concatenate([w[0], z], axis=1),
                            jnp.concatenate([z, w[1]], axis=1)], axis=0).astype(BF16)


def _forward(x_a, x_b, norm_ffn1, ffn1_w_gate, ffn1_w_up, ffn1_w_down, norm_mix, w_in, shift_mu,
             rwkv_w0, rwkv_w2, rwkv_a0, rwkv_a2, rwkv_g2, rwkv_k_k, rwkv_k_a, rwkv_r_k,
             rwkv_ln_w, rwkv_ln_b, s5_a_re, s5_a_im, s5_log_step, s5_b_re, s5_b_im,
             s5_c_re, s5_c_im, s5_d, s5_w_glu, s5_b_glu, proj_rwkv, proj_s5, w_out,
             norm_ffn2, ffn2_w_gate, ffn2_w_up, ffn2_w_down, norm_final):
    l, d = x_a.shape[1:]
    b = x_a.shape[0] + x_b.shape[0]
    n_a, n_b = x_a.shape[0] * l, x_b.shape[0] * l
    n = n_a + n_b
    row = lambda p: p.reshape(1, -1)

    (x1,) = _ffn([x_a.reshape(n_a, d), x_b.reshape(n_b, d)], row(norm_ffn1), ffn1_w_gate.astype(BF16),
                 ffn1_w_up.astype(BF16), ffn1_w_down.astype(BF16), row(norm_final),
                 out_rows=[n], final_norm=False)

    assert w_in.shape[1] == SHIFT_COLS + S5_DIM + 2 * d
    w_proj = _proj_weight(w_in)
    proj = _in_proj(x1, row(norm_mix), w_proj)
    proj3 = proj.reshape(b, l, PROJ_COLS)

    head_id = jnp.arange(QUAD) // HEAD_DIM
    ones_bd = (head_id[:, None] == head_id[None, :]).astype(BF16)
    at, rt, bt, kt, wt, v, g, bonus = _rwkv_prep(
        proj3, row(shift_mu),
        _lora_cat(rwkv_w2), rwkv_w0.reshape(1, -1), _lora_cat(rwkv_a2), rwkv_a0.reshape(1, -1),
        rwkv_g2.astype(BF16), row(rwkv_k_k), row(rwkv_k_a), row(rwkv_r_k), ones_bd)
    yf, yb = _rwkv_scan(at, rt, bt, kt, wt, v)

    bdense, cdense, lam = _s5_params(s5_a_re, s5_a_im, s5_log_step, s5_b_re, s5_b_im, s5_c_re, s5_c_im)
    ysf, ysb = _s5_scan(proj3, bdense, cdense, lam)
    ys = _s5_post(ysf, ysb, proj3, row(s5_d), s5_w_glu.astype(BF16), row(s5_b_glu)).reshape(n, S5_DIM)

    flat = lambda a: a.reshape(n, RWKV_DIM)
    x2 = _merge(x1, flat(yf), flat(yb), flat(bonus), flat(g), ys, proj, row(rwkv_ln_w), row(rwkv_ln_b),
                ones_bd, proj_rwkv.astype(BF16), proj_s5.astype(BF16), w_out.astype(BF16))
    y_a, y_b = _ffn([x2], row(norm_ffn2), ffn2_w_gate.astype(BF16), ffn2_w_up.astype(BF16),
                    ffn2_w_down.astype(BF16), row(norm_final), out_rows=[n_a, n_b], final_norm=True)
    return y_a.reshape(x_a.shape), y_b.reshape(x_b.shape)


def kernel(x_prompt, x_sample, norm_ffn1, ffn1_w_gate, ffn1_w_up, ffn1_w_down, norm_mix, w_in, shift_mu, rwkv_w0, rwkv_w2, rwkv_a0, rwkv_a2, rwkv_g2, rwkv_k_k, rwkv_k_a, rwkv_r_k, rwkv_ln_w, rwkv_ln_b, s5_a_re, s5_a_im, s5_log_step, s5_b_re, s5_b_im, s5_c_re, s5_c_im, s5_d, s5_w_glu, s5_b_glu, proj_rwkv, proj_s5, w_out, norm_ffn2, ffn2_w_gate, ffn2_w_up, ffn2_w_down, norm_final):
    layer = (norm_ffn1, ffn1_w_gate, ffn1_w_up, ffn1_w_down, norm_mix, w_in, shift_mu,
             rwkv_w0, rwkv_w2, rwkv_a0, rwkv_a2, rwkv_g2, rwkv_k_k, rwkv_k_a, rwkv_r_k,
             rwkv_ln_w, rwkv_ln_b, s5_a_re, s5_a_im, s5_log_step, s5_b_re, s5_b_im,
             s5_c_re, s5_c_im, s5_d, s5_w_glu, s5_b_glu, proj_rwkv, proj_s5, w_out,
             norm_ffn2, ffn2_w_gate, ffn2_w_up, ffn2_w_down)
    assert all(p.shape[0] == 1 for p in layer), "single-layer block"
    assert x_prompt.shape[1:] == x_sample.shape[1:]
    return _forward(x_prompt, x_sample, *[p[0] for p in layer], norm_final)
```

```python
import functools
import math

import jax
import jax.numpy as jnp
from jax import lax
from jax.experimental import pallas as pl
from jax.experimental.pallas import tpu as pltpu

F32 = jnp.float32
BF16 = jnp.bfloat16

D_MODEL = 2048
D_FF = 5504
HEADS = 16
HEAD_DIM = 64
RWKV_DIM = HEADS * HEAD_DIM
LORA = 64
GATE_LORA = 128
S5_GROUPS = 32
S5_CH = 16
S5_DIM = S5_GROUPS * S5_CH
S5_STATE = 64
S5_STATES = S5_GROUPS * S5_STATE
SHIFT_COLS = 3 * RWKV_DIM + 4 * LORA + GATE_LORA
RMS_EPS = 1e-6
GN_EPS = 64e-5
KK_EPS = 1e-12

LANES = 128
SUBLANES = 8
VMEM_LIMIT = 56 * 1024 * 1024

FF_TILE = 512
TOK_TILE = 512
CHUNK = 64
INV_BLOCK = 16
QUAD = 4 * HEAD_DIM
SCAN_HEADS = 2
GROUP = SCAN_HEADS * HEAD_DIM
SCAN_CHUNKS = 4
S5_SEQ_PAD = SUBLANES
S5_TILE = 64
S5_SPLIT = 2
S5_PARTS = 2
PROJ_TOK_TILE = 1024
PROJ_COL_TILE = 1024
PROJ_U_OFF = 3584
PROJ_GATE_OFF = 4096
PROJ_COLS = 8192


def _cparams(sem):
    return pltpu.CompilerParams(dimension_semantics=sem, vmem_limit_bytes=VMEM_LIMIT)


def _dot(a, b):
    return jnp.dot(a.astype(BF16), b.astype(BF16), preferred_element_type=F32)


def _dot_nt(a, b):
    return lax.dot_general(a.astype(BF16), b.astype(BF16), (((1,), (1,)), ((), ())),
                           preferred_element_type=F32)


def _dot_tn(a, b):
    return lax.dot_general(a.astype(BF16), b.astype(BF16), (((0,), (0,)), ((), ())),
                           preferred_element_type=F32)


def _split_dot(x, w, pieces):
    acc = None
    rem = x
    for _ in range(pieces):
        part = rem.astype(BF16)
        term = jnp.dot(part, w, preferred_element_type=F32)
        acc = term if acc is None else acc + term
        rem = rem - part.astype(F32)
    return acc


def _sigmoid(x):
    return 1.0 / (1.0 + jnp.exp(-x))


def _rms(x, g):
    ms = jnp.mean(x * x, axis=-1, keepdims=True)
    return x * lax.rsqrt(ms + RMS_EPS) * g


def _ffn_kernel(*refs, n_x, n_out, first_tiles, final_norm):
    x_refs = refs[:n_x]
    g_ref, wg_ref, wu_ref, wd_ref, wgt_ref, wut_ref, wdt_ref, gf_ref = refs[n_x:n_x + 8]
    o_refs = refs[n_x + 8:n_x + 8 + n_out]
    xn_ref, acc_ref = refs[n_x + 8 + n_out:]
    i = pl.program_id(0)
    j = pl.program_id(1)
    last = j == pl.num_programs(1) - 1
    in_first = i < first_tiles
    parts = [(in_first, x_refs[0], o_refs[0]), (jnp.logical_not(in_first), x_refs[-1], o_refs[-1])]
    if n_x == 1 and n_out == 1:
        parts = [(True, x_refs[0], o_refs[0])]

    for cond, x_ref, _ in parts:
        @pl.when((j == 0) & cond)
        def _(x_ref=x_ref):
            xn_ref[...] = _rms(x_ref[...], g_ref[...]).astype(BF16)
            acc_ref[...] = jnp.zeros_like(acc_ref)

    def swiglu_down(wg, wu, wd):
        xn = xn_ref[...]
        hg = jnp.dot(xn, wg[...], preferred_element_type=F32)
        hu = jnp.dot(xn, wu[...], preferred_element_type=F32)
        act = (hg * _sigmoid(hg)) * hu
        return jnp.dot(act.astype(BF16), wd[...], preferred_element_type=F32)

    acc_ref[...] += swiglu_down(wg_ref, wu_ref, wd_ref)

    @pl.when(last)
    def _():
        acc_ref[...] += swiglu_down(wgt_ref, wut_ref, wdt_ref)

    for cond, x_ref, o_ref in parts:
        @pl.when(last & cond)
        def _(x_ref=x_ref, o_ref=o_ref):
            y = x_ref[...] + 0.5 * acc_ref[...]
            o_ref[...] = _rms(y, gf_ref[...]) if final_norm else y


def _ffn(xs, g, wg, wu, wd, gf, *, out_rows, final_norm):
    d = xs[0].shape[1]
    tm, tf = TOK_TILE, FF_TILE
    f = wg.shape[1] // tf * tf
    assert 0 < wg.shape[1] - f and (wg.shape[1] - f) % LANES == 0
    wgt, wut, wdt = wg[:, f:], wu[:, f:], wd[f:, :]
    const = lambda a: pl.BlockSpec(a.shape, lambda i, j: (0, 0), pipeline_mode=pl.Buffered(1))
    n = sum(x.shape[0] for x in xs)
    assert n == sum(out_rows) and all(r % tm == 0 for r in out_rows)
    assert all(x.shape[0] % tm == 0 for x in xs)
    first_rows = xs[0].shape[0] if len(xs) == 2 else out_rows[0]
    if len(xs) == 2 and len(out_rows) == 2:
        assert xs[0].shape[0] == out_rows[0]
    first_tiles = first_rows // tm

    def split_specs(count):
        if count == 1:
            return [pl.BlockSpec((tm, d), lambda i, j: (i, 0))]
        return [pl.BlockSpec((tm, d), lambda i, j: (jnp.minimum(i, first_tiles - 1), 0)),
                pl.BlockSpec((tm, d), lambda i, j: (jnp.maximum(i - first_tiles, 0), 0))]

    outs = pl.pallas_call(
        functools.partial(_ffn_kernel, n_x=len(xs), n_out=len(out_rows), first_tiles=first_tiles,
                          final_norm=final_norm),
        grid=(n // tm, f // tf),
        in_specs=split_specs(len(xs)) + [
            pl.BlockSpec((1, d), lambda i, j: (0, 0)),
            pl.BlockSpec((d, tf), lambda i, j: (0, j)),
            pl.BlockSpec((d, tf), lambda i, j: (0, j)),
            pl.BlockSpec((tf, d), lambda i, j: (j, 0)),
            const(wgt), const(wut), const(wdt),
            pl.BlockSpec((1, d), lambda i, j: (0, 0)),
        ],
        out_specs=split_specs(len(out_rows)),
        out_shape=[jax.ShapeDtypeStruct((r, d), F32) for r in out_rows],
        scratch_shapes=[pltpu.VMEM((tm, d), BF16), pltpu.VMEM((tm, d), F32)],
        compiler_params=_cparams(("arbitrary", "arbitrary")),
        name="ffn_final" if final_norm else "ffn",
    )(*xs, g, wg, wu, wd, wgt, wut, wdt, gf)
    return outs


def _in_proj_kernel(x_ref, g_ref, w_ref, o_ref, xn_ref):
    @pl.when(pl.program_id(1) == 0)
    def _():
        xn_ref[...] = _rms(x_ref[...], g_ref[...]).astype(BF16)

    o_ref[...] = jnp.dot(xn_ref[...], w_ref[...], preferred_element_type=F32)


def _in_proj(x, g, w):
    n, d = x.shape
    c = w.shape[1]
    tm, tn = PROJ_TOK_TILE, PROJ_COL_TILE
    return pl.pallas_call(
        _in_proj_kernel,
        grid=(n // tm, c // tn),
        in_specs=[
            pl.BlockSpec((tm, d), lambda i, j: (i, 0)),
            pl.BlockSpec((1, d), lambda i, j: (0, 0)),
            pl.BlockSpec((d, tn), lambda i, j: (0, j)),
        ],
        out_specs=pl.BlockSpec((tm, tn), lambda i, j: (i, j)),
        out_shape=jax.ShapeDtypeStruct((n, c), F32),
        scratch_shapes=[pltpu.VMEM((tm, d), BF16)],
        compiler_params=_cparams(("parallel", "arbitrary")),
        name="proj_in",
    )(x, g, w)


PREP_TILE = 256


def _head_sum(x, ones_quad):
    return jnp.concatenate([_split_dot(x[:, c:c + QUAD], ones_quad, 2)
                            for c in range(0, x.shape[1], QUAD)], axis=1)


def _rwkv_prep_kernel(main_ref, prev_ref, next_ref, mu_ref, w2_ref, w0_ref, a2_ref, a0_ref, g2_ref,
                      kk_ref, ka_ref, rk_ref, ones_ref, tri_ref,
                      at_out, rt_out, bt_out, kt_out, wt_out, v_out, g_out, bonus_out):
    i = pl.program_id(1)
    tl = main_ref.shape[1]
    x = main_ref[0]
    row = lax.broadcasted_iota(jnp.int32, (tl, 1), 0)
    before = jnp.where(i == 0, 0.0, prev_ref[0, SUBLANES - 1:SUBLANES, :])
    after = jnp.where(i == pl.num_programs(1) - 1, 0.0, next_ref[0, 0:1, :])
    prev = jnp.where(row == 0, before, pltpu.roll(x, 1, 0))
    nxt = jnp.where(row == tl - 1, after, pltpu.roll(x, tl - 1, 0))
    p = x + (0.5 * (prev + nxt) - x) * mu_ref[...]

    c0 = 3 * RWKV_DIM
    r = p[:, 0:RWKV_DIM]
    k = p[:, RWKV_DIM:2 * RWKV_DIM]
    v = p[:, 2 * RWKV_DIM:c0]
    wlow = jnp.tanh(p[:, c0:c0 + 2 * LORA])
    alow = p[:, c0 + 2 * LORA:c0 + 4 * LORA]
    glow = p[:, c0 + 4 * LORA:c0 + 4 * LORA + GATE_LORA]
    ones_bd = ones_ref[...]

    wpre = _dot(wlow, w2_ref[...]) + w0_ref[...]
    lw = (-math.exp(-0.5) * math.log2(math.e)) * _sigmoid(wpre)
    a = _sigmoid(_dot(alow, a2_ref[...]) + a0_ref[...])

    kk = k * kk_ref[...]
    kkn = kk * lax.rsqrt(_head_sum(kk * kk, ones_bd) + KK_EPS)
    ka = ka_ref[...]
    ksum = None
    n_chunks = tl // CHUNK
    for d in range(2):
        a_d = a[:, d * RWKV_DIM:(d + 1) * RWKV_DIM]
        lw_d = lw[:, d * RWKV_DIM:(d + 1) * RWKV_DIM]
        kd = k * (1.0 + (a_d - 1.0) * ka)
        cum = _cumsum(tri_ref[d], lw_d)
        e_out = jnp.exp2(-cum)
        at_out[d, 0] = (-kkn * jnp.exp2(cum - lw_d)).astype(BF16)
        rt_out[d, 0] = (r * jnp.exp2(cum)).astype(BF16)
        bt_out[d, 0] = (kkn * a_d * e_out).astype(BF16)
        kt_out[d, 0] = (kd * e_out).astype(BF16)
        total = jnp.exp2(jnp.sum(lw_d.reshape(n_chunks, CHUNK, RWKV_DIM), axis=1))
        wt_out[d, 0] = jnp.broadcast_to(total[:, None, :], (n_chunks, SUBLANES, RWKV_DIM)).reshape(
            n_chunks * SUBLANES, RWKV_DIM)
        ksum = kd if ksum is None else ksum + kd
    v_out[0] = v.astype(BF16)
    g_out[0] = _dot(_sigmoid(glow), g2_ref[...])
    bonus_out[0] = _head_sum(r * ksum * rk_ref[...], ones_bd) * v


def _rwkv_prep(pshift, mu, w2cat, w0cat, a2cat, a0cat, g2, k_k, k_a, r_k, ones_bd):
    b, l, _ = pshift.shape
    c = SHIFT_COLS
    tl = PREP_TILE
    nt = l // tl
    hb = tl // SUBLANES
    nhb = l // SUBLANES
    full = lambda shape: pl.BlockSpec(shape, lambda bi, i: (0,) * len(shape))
    tok = pl.BlockSpec((1, tl, RWKV_DIM), lambda bi, i: (bi, i, 0))
    tok2 = pl.BlockSpec((2, 1, tl, RWKV_DIM), lambda bi, i: (0, bi, i, 0))
    wrows = tl // CHUNK * SUBLANES
    tokw = pl.BlockSpec((2, 1, wrows, RWKV_DIM), lambda bi, i: (0, bi, i, 0))
    s1 = jax.ShapeDtypeStruct((b, l, RWKV_DIM), F32)
    s1h = jax.ShapeDtypeStruct((b, l, RWKV_DIM), BF16)
    s2h = jax.ShapeDtypeStruct((2, b, l, RWKV_DIM), BF16)
    sw = jax.ShapeDtypeStruct((2, b, l // CHUNK * SUBLANES, RWKV_DIM), F32)
    pos = jnp.arange(tl)
    same_chunk = (pos[:, None] // CHUNK) == (pos[None, :] // CHUNK)
    tri = jnp.stack([same_chunk & (pos[None, :] <= pos[:, None]),
                     same_chunk & (pos[None, :] >= pos[:, None])]).astype(BF16)
    return pl.pallas_call(
        _rwkv_prep_kernel,
        grid=(b, nt),
        in_specs=[
            pl.BlockSpec((1, tl, c), lambda bi, i: (bi, i, 0)),
            pl.BlockSpec((1, SUBLANES, c), lambda bi, i: (bi, jnp.maximum(i * hb - 1, 0), 0)),
            pl.BlockSpec((1, SUBLANES, c), lambda bi, i: (bi, jnp.minimum((i + 1) * hb, nhb - 1), 0)),
            full((1, c)),
            full(w2cat.shape), full(w0cat.shape), full(a2cat.shape), full(a0cat.shape), full(g2.shape),
            full((1, RWKV_DIM)), full((1, RWKV_DIM)), full((1, RWKV_DIM)), full(ones_bd.shape),
            full(tri.shape),
        ],
        out_specs=[tok2, tok2, tok2, tok2, tokw, tok, tok, tok],
        out_shape=[s2h, s2h, s2h, s2h, sw, s1h, s1, s1],
        compiler_params=_cparams(("parallel", "arbitrary")),
        name="rwkv_prep",
    )(pshift, pshift, pshift, mu, w2cat, w0cat, a2cat, a0cat, g2, k_k, k_a, r_k, ones_bd, tri)


def _stack_heads(x, head_masks):
    return jnp.concatenate([jnp.where(m, x, 0.0) for m in head_masks], axis=0)


def _rwkv_scan_kernel(atf_ref, rtf_ref, btf_ref, ktf_ref, wtf_ref, vf_ref,
                      atb_ref, rtb_ref, btb_ref, ktb_ref, wtb_ref, vb_ref,
                      of_ref, ob_ref, s_ref):
    t = CHUNK

    @pl.when(pl.program_id(1) == 0)
    def _():
        s_ref[...] = jnp.zeros_like(s_ref)

    row = lax.broadcasted_iota(jnp.int32, (t, GROUP), 0)
    lane = lax.broadcasted_iota(jnp.int32, (t, GROUP), 1)
    head_shift = HEAD_DIM.bit_length() - 1
    blk_shift = INV_BLOCK.bit_length() - 1
    col = lane & (HEAD_DIM - 1)
    same16 = (row >> blk_shift) == (col >> blk_shift)
    same32 = (row >> (blk_shift + 1)) == (col >> (blk_shift + 1))
    in32_off16 = same32 & jnp.logical_not(same16)
    eye = jnp.where(col == row, 1.0, 0.0)
    head_masks = [(lane >> head_shift) == h for h in range(SCAN_HEADS)]
    strict = (col < row, col > row)
    incl = (col <= row, col >= row)
    qrow = lax.broadcasted_iota(jnp.int32, (GROUP, GROUP), 0) >> head_shift
    qcol = lax.broadcasted_iota(jnp.int32, (GROUP, GROUP), 1) >> head_shift
    same_head = qrow == qcol
    stack = lambda x: _stack_heads(x, head_masks)
    cat = lambda a, b: jnp.concatenate([a, b], axis=0)

    refs = ((atf_ref, rtf_ref, btf_ref, ktf_ref, wtf_ref, vf_ref, of_ref),
            (atb_ref, rtb_ref, btb_ref, ktb_ref, wtb_ref, vb_ref, ob_ref))
    n_sub = vf_ref.shape[1] // t
    chains = [(d, q, c) for c in range(n_sub) for q in range(RWKV_DIM // GROUP) for d in range(2)]
    each = lambda fn, *lists: [fn(*args) for args in zip(*lists)]
    sl = lambda q: slice(q * GROUP, (q + 1) * GROUP)

    def chunk(d, c):
        return c if d == 0 else n_sub - 1 - c

    def rows(d, c):
        return slice(chunk(d, c) * t, (chunk(d, c) + 1) * t)

    load = lambda j: [refs[d][j][0, 0, rows(d, c), sl(q)].astype(F32) for d, q, c in chains]
    at, rt, bt, kt = load(0), load(1), load(2), load(3)
    v = [refs[d][5][0, rows(d, c), sl(q)].astype(F32) for d, q, c in chains]
    w_total = [refs[d][4][0, 0, chunk(d, c) * SUBLANES:chunk(d, c) * SUBLANES + 1, sl(q)]
               for d, q, c in chains]
    lhs = each(cat, at, rt)
    bk = each(cat, bt, kt)

    gram = each(lambda l_, b_, k_: _dot_nt(l_, cat(stack(b_), stack(k_))), lhs, bt, kt)
    a_ab = [jnp.where(strict[d], g[:t, :GROUP], 0.0) for (d, _, _), g in zip(chains, gram)]
    a_ak = [jnp.where(strict[d], g[:t, GROUP:], 0.0) for (d, _, _), g in zip(chains, gram)]
    a_rb = [jnp.where(incl[d], g[t:, :GROUP], 0.0) for (d, _, _), g in zip(chains, gram)]
    a_rk = [jnp.where(incl[d], g[t:, GROUP:], 0.0) for (d, _, _), g in zip(chains, gram)]
    from_v = each(lambda p, q_, v_: _dot(cat(p, q_), stack(v_)), a_ak, a_rk, v)

    a_d = each(lambda a: jnp.where(same16, a, 0.0), a_ab)
    pw = each(lambda a: eye + a, a_d)
    sq = each(lambda a: _dot(a, stack(a)), a_d)
    for _ in range(2):
        both = each(lambda p, s: _dot(cat(p, s), stack(s)), pw, sq)
        pw = each(lambda p, bo: p + bo[:t], pw, both)
        sq = each(lambda bo: bo[t:], both)
    x0 = each(lambda p, s: p + _dot(p, stack(s)), pw, sq)
    m1 = each(lambda x, a: _dot(x, stack(jnp.where(in32_off16, a, 0.0))), x0, a_ab)
    x1 = each(lambda x, m: x + _dot(m, stack(x)), x0, m1)
    m2 = each(lambda x, a: _dot(x, stack(jnp.where(same32, 0.0, a))), x1, a_ab)
    x2 = each(lambda x, m: x + _dot(m, stack(x)), x1, m2)

    per_chunk = len(chains) // n_sub
    state = [s_ref[d, q] for d, q, _ in chains[:per_chunk]]
    for c in range(n_sub):
        pick = lambda xs: xs[c * per_chunk:(c + 1) * per_chunk]
        from_state = each(_dot_nt, pick(lhs), state)
        u = each(lambda x, fs, fv: _dot(x, stack(fs[:t] + fv[:t])), pick(x2), from_state, pick(from_v))
        y = each(lambda fs, fv, a, u_: fs[t:] + fv[t:] + _dot(a, stack(u_)),
                 from_state, pick(from_v), pick(a_rb), u)
        upd = each(lambda u_, v_, bk_: _dot_tn(cat(u_, v_), bk_), u, pick(v), pick(bk))
        for (d, q, _), y_ in zip(pick(chains), y):
            refs[d][6][0, rows(d, c), sl(q)] = y_
        state = each(lambda s_, up, w: (s_ + jnp.where(same_head, up, 0.0)) * w, state, upd, pick(w_total))
    for (d, q, _), s_ in zip(chains[:per_chunk], state):
        s_ref[d, q] = s_


def _cumsum(tri, lw):
    acc = None
    rem = lw
    for _ in range(3):
        part = rem.astype(BF16)
        term = jnp.dot(tri, part, preferred_element_type=F32)
        acc = term if acc is None else acc + term
        rem = rem - part.astype(F32)
    return acc


def _rwkv_scan(at, rt, bt, kt, wt, v):
    b, l, c = v.shape
    tl = SCAN_CHUNKS * CHUNK
    nc = l // tl
    wrows = SCAN_CHUNKS * SUBLANES
    fwd = pl.BlockSpec((1, tl, c), lambda bi, i: (bi, i, 0))
    bwd = pl.BlockSpec((1, tl, c), lambda bi, i: (bi, nc - 1 - i, 0))
    fwd_d = pl.BlockSpec((1, 1, tl, c), lambda bi, i: (0, bi, i, 0))
    bwd_d = pl.BlockSpec((1, 1, tl, c), lambda bi, i: (1, bi, nc - 1 - i, 0))
    fwd_w = pl.BlockSpec((1, 1, wrows, c), lambda bi, i: (0, bi, i, 0))
    bwd_w = pl.BlockSpec((1, 1, wrows, c), lambda bi, i: (1, bi, nc - 1 - i, 0))
    return pl.pallas_call(
        _rwkv_scan_kernel,
        grid=(b, nc),
        in_specs=[fwd_d, fwd_d, fwd_d, fwd_d, fwd_w, fwd, bwd_d, bwd_d, bwd_d, bwd_d, bwd_w, bwd],
        out_specs=[fwd, bwd],
        out_shape=[jax.ShapeDtypeStruct((b, l, c), F32)] * 2,
        scratch_shapes=[pltpu.VMEM((2, c // GROUP, GROUP, GROUP), F32)],
        compiler_params=_cparams(("parallel", "arbitrary")),
        name="rwkv_scan",
    )(at, rt, bt, kt, wt, v, at, rt, bt, kt, wt, v)


def _s5_scan_kernel(uf_ref, ub_ref, bdense_ref, cdense_ref, lam_ref, of_ref, ob_ref,
                    ubuf_ref, st_ref, s_ref):
    n_seq, tl = uf_ref.shape[0], uf_ref.shape[1]
    ns = S5_STATES
    u_refs = (uf_ref, ub_ref)
    o_refs = (of_ref, ob_ref)

    @pl.when(pl.program_id(0) == 0)
    def _():
        s_ref[...] = jnp.zeros_like(s_ref)
        ubuf_ref[...] = jnp.zeros_like(ubuf_ref)

    hs = ns // S5_SPLIT
    hc = S5_DIM // S5_SPLIT
    tp = tl // S5_PARTS
    rows = tp * S5_SEQ_PAD

    def tokens(d, p):
        p = p if d == 0 else S5_PARTS - 1 - p
        return slice(p * tp, (p + 1) * tp)

    def project_in(d, p):
        tk = tokens(d, p)
        for b in range(n_seq):
            ubuf_ref[d, tk, b, :] = u_refs[d][b, tk, :]
        u = ubuf_ref[d, tk].reshape(rows, S5_DIM)
        for h in range(S5_SPLIT):
            bu = _dot(u[:, h * hc:(h + 1) * hc], bdense_ref[d, h])
            st_ref[d, tk, :, h * hs:(h + 1) * hs] = bu[:, :hs].reshape(tp, S5_SEQ_PAD, hs)
            st_ref[d, tk, :, ns + h * hs:ns + (h + 1) * hs] = bu[:, hs:].reshape(tp, S5_SEQ_PAD, hs)

    def recur(d, p):
        lam_re = lam_ref[d, 0]
        lam_im = lam_ref[d, 1]
        s_re, s_im = s_ref[d, 0], s_ref[d, 1]
        first = tokens(d, p).start
        for j in range(tp):
            tt = first + (j if d == 0 else tp - 1 - j)
            s_re, s_im = (lam_re * s_re - lam_im * s_im + st_ref[d, tt, :, :ns],
                          lam_re * s_im + lam_im * s_re + st_ref[d, tt, :, ns:])
            st_ref[d, tt, :, :ns] = s_re
            st_ref[d, tt, :, ns:] = s_im
        s_ref[d, 0] = s_re
        s_ref[d, 1] = s_im

    def project_out(d, p):
        tk = tokens(d, p)
        for h in range(S5_SPLIT):
            st_re = st_ref[d, tk, :, h * hs:(h + 1) * hs].reshape(rows, hs)
            st_im = st_ref[d, tk, :, ns + h * hs:ns + (h + 1) * hs].reshape(rows, hs)
            y = _dot(st_re, cdense_ref[h, :hs]) + _dot(st_im, cdense_ref[h, hs:])
            o_refs[d][tk, :, h * hc:(h + 1) * hc] = y.reshape(tp, S5_SEQ_PAD, hc)

    units = [(d, p) for p in range(S5_PARTS) for d in range(2)]
    project_in(*units[0])
    for k, unit in enumerate(units):
        if k + 1 < len(units):
            project_in(*units[k + 1])
        recur(*unit)
        if k >= 1:
            project_out(*units[k - 1])
    project_out(*units[-1])


def _s5_scan(proj, bdense, cdense, lam):
    b, l, _ = proj.shape
    assert b <= S5_SEQ_PAD and PROJ_U_OFF % S5_DIM == 0
    tl = S5_TILE
    nt = l // tl
    ucol = PROJ_U_OFF // S5_DIM
    const = lambda a: pl.BlockSpec(a.shape, lambda i: (0,) * a.ndim, pipeline_mode=pl.Buffered(1))
    out = jax.ShapeDtypeStruct((l, S5_SEQ_PAD, S5_DIM), F32)
    return pl.pallas_call(
        _s5_scan_kernel,
        grid=(nt,),
        in_specs=[
            pl.BlockSpec((b, tl, S5_DIM), lambda i: (0, i, ucol)),
            pl.BlockSpec((b, tl, S5_DIM), lambda i: (0, nt - 1 - i, ucol)),
            const(bdense), const(cdense), const(lam),
        ],
        out_specs=[pl.BlockSpec((tl, S5_SEQ_PAD, S5_DIM), lambda i: (i, 0, 0)),
                   pl.BlockSpec((tl, S5_SEQ_PAD, S5_DIM), lambda i: (nt - 1 - i, 0, 0))],
        out_shape=[out, out],
        scratch_shapes=[pltpu.VMEM((2, tl, S5_SEQ_PAD, S5_DIM), F32),
                        pltpu.VMEM((2, tl, S5_SEQ_PAD, 2 * S5_STATES), F32),
                        pltpu.VMEM((2, 2, S5_SEQ_PAD, S5_STATES), F32)],
        compiler_params=_cparams(("arbitrary",)),
        name="s5_scan",
    )(proj, proj, bdense, cdense, lam)


def _s5_post_kernel(yf_ref, yb_ref, u_ref, dskip_ref, wglu_ref, bglu_ref, o_ref):
    n_seq, tl = u_ref.shape[0], u_ref.shape[1]
    per_seq = lambda ref, b: ref[:, b, :]
    y = jnp.concatenate([per_seq(yf_ref, b) + per_seq(yb_ref, b) + dskip_ref[...] * u_ref[b]
                         for b in range(n_seq)], axis=0)
    y = 0.5 * y * (1.0 + jnp.tanh(math.sqrt(2.0 / math.pi) * (y + 0.044715 * (y * y * y))))
    gate = _sigmoid(_dot(y, wglu_ref[...]) + bglu_ref[...])
    o_ref[...] = (y * gate).astype(BF16).reshape(n_seq, tl, S5_DIM)


def _s5_post(yf, yb, proj, d_skip, w_glu, b_glu):
    b, l, _ = proj.shape
    tl = S5_TILE
    row = pl.BlockSpec((1, S5_DIM), lambda i: (0, 0))
    scan_out = pl.BlockSpec((tl, S5_SEQ_PAD, S5_DIM), lambda i: (i, 0, 0))
    return pl.pallas_call(
        _s5_post_kernel,
        grid=(l // tl,),
        in_specs=[
            scan_out, scan_out,
            pl.BlockSpec((b, tl, S5_DIM), lambda i: (0, i, PROJ_U_OFF // S5_DIM)),
            row,
            pl.BlockSpec((S5_DIM, S5_DIM), lambda i: (0, 0)),
            row,
        ],
        out_specs=pl.BlockSpec((b, tl, S5_DIM), lambda i: (0, i, 0)),
        out_shape=jax.ShapeDtypeStruct((b, l, S5_DIM), BF16),
        compiler_params=_cparams(("parallel",)),
        name="s5_post",
    )(yf, yb, proj, d_skip, w_glu, b_glu)


def _s5_params(a_re, a_im, log_step, b_re, b_im, c_re, c_im):
    dt = jnp.exp(log_step)[..., None]
    z_re, z_im = a_re * dt, a_im * dt
    mag = jnp.exp(z_re)
    lam_re, lam_im = mag * jnp.cos(z_im), mag * jnp.sin(z_im)
    den = a_re * a_re + a_im * a_im
    q_re = ((lam_re - 1.0) * a_re + lam_im * a_im) / den
    q_im = (lam_im * a_re - (lam_re - 1.0) * a_im) / den
    bb_re = q_re[..., None] * b_re - q_im[..., None] * b_im
    bb_im = q_re[..., None] * b_im + q_im[..., None] * b_re
    gb = S5_GROUPS // S5_SPLIT
    eye = jnp.eye(gb, dtype=F32)

    def dense_b(x):
        x = x.reshape(2, S5_SPLIT, gb, S5_STATE, S5_CH)
        return jnp.einsum('dhgpc,gk->dhgckp', x, eye).reshape(2, S5_SPLIT, gb * S5_CH, gb * S5_STATE)

    def dense_c(x):
        x = x.reshape(S5_SPLIT, gb, S5_CH, S5_STATE)
        return jnp.einsum('hgcp,gk->hgpkc', x, eye).reshape(S5_SPLIT, gb * S5_STATE, gb * S5_CH)

    bdense = jnp.concatenate([dense_b(bb_re), dense_b(bb_im)], axis=-1).astype(BF16)
    cdense = jnp.concatenate([dense_c(c_re), -dense_c(c_im)], axis=1).astype(BF16)
    lam = jnp.stack([lam_re.reshape(2, S5_STATES), lam_im.reshape(2, S5_STATES)], axis=1)
    lam = jnp.broadcast_to(lam[:, :, None, :], (2, 2, S5_SEQ_PAD, S5_STATES))
    return bdense, cdense, lam


MERGE_TILE = 256


def _merge_kernel(x_ref, yf_ref, yb_ref, bonus_ref, g_ref, ys_ref, gate_ref,
                  lnw_ref, lnb_ref, ones_ref, pr_ref, ps_ref, wo_ref, o_ref):
    ones_bd = ones_ref[...]
    y = yf_ref[...] + yb_ref[...]
    mean = _head_sum(y, ones_bd) * (1.0 / HEAD_DIM)
    yc = y - mean
    var = _head_sum(yc * yc, ones_bd) * (1.0 / HEAD_DIM)
    yn = yc * lax.rsqrt(var + GN_EPS) * lnw_ref[...] + lnb_ref[...]
    yr = (yn + bonus_ref[...]) * g_ref[...]
    y_rwkv = _dot(yr, pr_ref[...])
    y_s5 = jnp.dot(ys_ref[...], ps_ref[...], preferred_element_type=F32)
    gates = _sigmoid(gate_ref[...])
    merged = gates[:, :D_MODEL] * y_rwkv + gates[:, D_MODEL:] * y_s5
    o_ref[...] = x_ref[...] + _dot(merged, wo_ref[...])


def _merge(x, yf, yb, bonus, g, ys, proj, ln_w, ln_b, ones_bd, proj_rwkv, proj_s5, w_out):
    n, d = x.shape
    tm = MERGE_TILE
    assert PROJ_GATE_OFF == 2 * d and proj.shape[1] == PROJ_GATE_OFF + 2 * d
    tok = lambda c: pl.BlockSpec((tm, c), lambda i: (i, 0))
    const = lambda a: pl.BlockSpec(a.shape, lambda i: (0, 0), pipeline_mode=pl.Buffered(1))
    return pl.pallas_call(
        _merge_kernel,
        grid=(n // tm,),
        in_specs=[tok(d), tok(RWKV_DIM), tok(RWKV_DIM), tok(RWKV_DIM), tok(RWKV_DIM), tok(S5_DIM),
                  pl.BlockSpec((tm, 2 * d), lambda i: (i, 1)),
                  const(ln_w), const(ln_b), const(ones_bd), const(proj_rwkv), const(proj_s5), const(w_out)],
        out_specs=tok(d),
        out_shape=jax.ShapeDtypeStruct((n, d), F32),
        compiler_params=_cparams(("parallel",)),
        name="merge",
    )(x, yf, yb, bonus, g, ys, proj, ln_w, ln_b, ones_bd, proj_rwkv, proj_s5, w_out)


def _proj_weight_kernel(w_ref, o_ref):
    o_ref[:, :SHIFT_COLS] = w_ref[:, :SHIFT_COLS].astype(BF16)
    o_ref[:, SHIFT_COLS:PROJ_U_OFF] = jnp.zeros((o_ref.shape[0], PROJ_U_OFF - SHIFT_COLS), BF16)
    o_ref[:, PROJ_U_OFF:] = w_ref[:, SHIFT_COLS:].astype(BF16)


def _proj_weight(w_in):
    d, cols = w_in.shape
    assert cols + PROJ_U_OFF - SHIFT_COLS == PROJ_COLS
    tr = 2 * LANES
    return pl.pallas_call(
        _proj_weight_kernel,
        grid=(d // tr,),
        in_specs=[pl.BlockSpec((tr, cols), lambda i: (i, 0))],
        out_specs=pl.BlockSpec((tr, PROJ_COLS), lambda i: (i, 0)),
        out_shape=jax.ShapeDtypeStruct((d, PROJ_COLS), BF16),
        compiler_params=_cparams(("parallel",)),
        name="proj_weight",
    )(w_in)


def _lora_cat(w):
    z = jnp.zeros_like(w[0])
    return jnp.concatenate([jnp.concatenate([w[0], z], axis=1),
                            jnp.concatenate([z, w[1]], axis=1)], axis=0).astype(BF16)


def _forward(x_a, x_b, norm_ffn1, ffn1_w_gate, ffn1_w_up, ffn1_w_down, norm_mix, w_in, shift_mu,
             rwkv_w0, rwkv_w2, rwkv_a0, rwkv_a2, rwkv_g2, rwkv_k_k, rwkv_k_a, rwkv_r_k,
             rwkv_ln_w, rwkv_ln_b, s5_a_re, s5_a_im, s5_log_step, s5_b_re, s5_b_im,
             s5_c_re, s5_c_im, s5_d, s5_w_glu, s5_b_glu, proj_rwkv, proj_s5, w_out,
             norm_ffn2, ffn2_w_gate, ffn2_w_up, ffn2_w_down, norm_final):
    l, d = x_a.shape[1:]
    b = x_a.shape[0] + x_b.shape[0]
    n_a, n_b = x_a.shape[0] * l, x_b.shape[0] * l
    n = n_a + n_b
    row = lambda p: p.reshape(1, -1)

    (x1,) = _ffn([x_a.reshape(n_a, d), x_b.reshape(n_b, d)], row(norm_ffn1), ffn1_w_gate.astype(BF16),
                 ffn1_w_up.astype(BF16), ffn1_w_down.astype(BF16), row(norm_final),
                 out_rows=[n], final_norm=False)

    assert w_in.shape[1] == SHIFT_COLS + S5_DIM + 2 * d
    w_proj = _proj_weight(w_in)
    proj = _in_proj(x1, row(norm_mix), w_proj)
    proj3 = proj.reshape(b, l, PROJ_COLS)

    head_id = jnp.arange(QUAD) // HEAD_DIM
    ones_bd = (head_id[:, None] == head_id[None, :]).astype(BF16)
    at, rt, bt, kt, wt, v, g, bonus = _rwkv_prep(
        proj3, row(shift_mu),
        _lora_cat(rwkv_w2), rwkv_w0.reshape(1, -1), _lora_cat(rwkv_a2), rwkv_a0.reshape(1, -1),
        rwkv_g2.astype(BF16), row(rwkv_k_k), row(rwkv_k_a), row(rwkv_r_k), ones_bd)
    yf, yb = _rwkv_scan(at, rt, bt, kt, wt, v)

    bdense, cdense, lam = _s5_params(s5_a_re, s5_a_im, s5_log_step, s5_b_re, s5_b_im, s5_c_re, s5_c_im)
    ysf, ysb = _s5_scan(proj3, bdense, cdense, lam)
    ys = _s5_post(ysf, ysb, proj3, row(s5_d), s5_w_glu.astype(BF16), row(s5_b_glu)).reshape(n, S5_DIM)

    flat = lambda a: a.reshape(n, RWKV_DIM)
    x2 = _merge(x1, flat(yf), flat(yb), flat(bonus), flat(g), ys, proj, row(rwkv_ln_w), row(rwkv_ln_b),
                ones_bd, proj_rwkv.astype(BF16), proj_s5.astype(BF16), w_out.astype(BF16))
    y_a, y_b = _ffn([x2], row(norm_ffn2), ffn2_w_gate.astype(BF16), ffn2_w_up.astype(BF16),
                    ffn2_w_down.astype(BF16), row(norm_final), out_rows=[n_a, n_b], final_norm=True)
    return y_a.reshape(x_a.shape), y_b.reshape(x_b.shape)


def kernel(x_prompt, x_sample, norm_ffn1, ffn1_w_gate, ffn1_w_up, ffn1_w_down, norm_mix, w_in, shift_mu, rwkv_w0, rwkv_w2, rwkv_a0, rwkv_a2, rwkv_g2, rwkv_k_k, rwkv_k_a, rwkv_r_k, rwkv_ln_w, rwkv_ln_b, s5_a_re, s5_a_im, s5_log_step, s5_b_re, s5_b_im, s5_c_re, s5_c_im, s5_d, s5_w_glu, s5_b_glu, proj_rwkv, proj_s5, w_out, norm_ffn2, ffn2_w_gate, ffn2_w_up, ffn2_w_down, norm_final):
    layer = (norm_ffn1, ffn1_w_gate, ffn1_w_up, ffn1_w_down, norm_mix, w_in, shift_mu,
             rwkv_w0, rwkv_w2, rwkv_a0, rwkv_a2, rwkv_g2, rwkv_k_k, rwkv_k_a, rwkv_r_k,
             rwkv_ln_w, rwkv_ln_b, s5_a_re, s5_a_im, s5_log_step, s5_b_re, s5_b_im,
             s5_c_re, s5_c_im, s5_d, s5_w_glu, s5_b_glu, proj_rwkv, proj_s5, w_out,
             norm_ffn2, ffn2_w_gate, ffn2_w_up, ffn2_w_down)
    assert all(p.shape[0] == 1 for p in layer), "single-layer block"
    assert x_prompt.shape[1:] == x_sample.shape[1:]
    return _forward(x_prompt, x_sample, *[p[0] for p in layer], norm_final)
```
